```python
import math
import jax, jax.numpy as jnp
from jax import lax
import numpy as np

D_MODEL = 1024
BATCH = 16
SEQ = 2048
DEPTH = 1
DEC_BATCH = 16
DEC_SEQ = 32
PAST_LEN = 4096

CHUNK = 64
N_META = 16
QBLOCK = 128
SB_HEADS = 8
SB_DIM = 64
SB_WIDTH = SB_HEADS * SB_DIM
DF_HEADS = 4
DF_DIM = 64
DF_VDIM = 2 * DF_DIM
DF_QK = 2 * DF_HEADS * DF_DIM
DF_WIDTH = DF_HEADS * DF_VDIM
MIX_WIDTH = SB_WIDTH + DF_WIDTH
IN_COLS = 3 * SB_WIDTH + 2 * DF_QK + DF_WIDTH
ROPE_THETA = 500000.0
ROT_DIM = DF_DIM // 4
N_GROUPS = 4
EXPERTS_PER_GROUP = 8
N_EXPERTS = N_GROUPS * EXPERTS_PER_GROUP
TOP_K_IN_GROUP = 2
D_EXPERT = 256
EPS = 1e-6

kernel_name = 'hymba_stickbreak_diffattn_hmoe_stream_step'


def lambda_init(layer_idx):
    return 0.8 - 0.6 * math.exp(-0.3 * layer_idx)


def rms_norm(x, g):
    xf = x.astype(jnp.float32)
    y = xf * lax.rsqrt(jnp.mean(xf * xf, axis=-1, keepdims=True) + EPS)
    return (y * g.astype(jnp.float32)).astype(x.dtype)


def partial_rope(x, pos):
    half = ROT_DIM // 2
    inv = ROPE_THETA ** (-jnp.arange(half, dtype=jnp.float32) * 2.0 / ROT_DIM)
    ang = pos.astype(jnp.float32)[:, None] * inv[None, :]
    cos = jnp.cos(ang)[None, :, None, :]
    sin = jnp.sin(ang)[None, :, None, :]
    xf = x.astype(jnp.float32)
    x1, x2 = xf[..., :half], xf[..., half:ROT_DIM]
    out = jnp.concatenate([x1 * cos - x2 * sin, x2 * cos + x1 * sin, xf[..., ROT_DIM:]], axis=-1)
    return out.astype(x.dtype)


def project(xn, w_in, pos):
    B, T, _ = xn.shape
    proj = jnp.einsum('btd,dc->btc', xn, w_in)
    idx = [SB_WIDTH, 2 * SB_WIDTH, 3 * SB_WIDTH, 3 * SB_WIDTH + DF_QK, 3 * SB_WIDTH + 2 * DF_QK]
    sb_q, sb_k, sb_v, df_q, df_k, df_v = jnp.split(proj, idx, axis=-1)
    sb_q = sb_q.reshape(B, T, SB_HEADS, SB_DIM)
    sb_k = sb_k.reshape(B, T, SB_HEADS, SB_DIM)
    sb_v = sb_v.reshape(B, T, SB_HEADS, SB_DIM)
    df_q = partial_rope(df_q.reshape(B, T, 2 * DF_HEADS, DF_DIM), pos)
    df_k = partial_rope(df_k.reshape(B, T, 2 * DF_HEADS, DF_DIM), pos)
    df_v = df_v.reshape(B, T, DF_HEADS, DF_VDIM)
    return sb_q, sb_k, sb_v, df_q, df_k, df_v


def stick_breaking_block(q, k, v, q_pos, k_pos):
    z = jnp.einsum('bqhd,bkhd->bhqk', q, k, preferred_element_type=jnp.float32) * (SB_DIM ** -0.5)
    visible = (k_pos[None, :] < q_pos[:, None])[None, None]
    log_keep = jnp.where(visible, -jax.nn.softplus(z), 0.0)
    between = lax.cumsum(log_keep, axis=3, reverse=True) - log_keep
    w = jnp.where(visible, jnp.exp(jax.nn.log_sigmoid(z) + between), 0.0)
    return jnp.einsum('bhqk,bkhd->bqhd', w, v.astype(jnp.float32))


def diff_attn_block(q, k, v, q_chunk, k_chunk, lam, subln_g, lam_init):
    s = jnp.einsum('bqhd,bkhd->bhqk', q, k, preferred_element_type=jnp.float32) * (DF_DIM ** -0.5)
    visible = (k_chunk[None, :] <= q_chunk[:, None])[None, None]
    p = jax.nn.softmax(jnp.where(visible, s, -jnp.inf), axis=-1)
    B, _, Q, K = p.shape
    p = p.reshape(B, DF_HEADS, 2, Q, K)
    a = p[:, :, 0] - lam * p[:, :, 1]
    o = jnp.einsum('bhqk,bkhe->bqhe', a, v.astype(jnp.float32))
    return rms_norm(o, subln_g) * (1.0 - lam_init)


def token_mixers(sb_q, df_q, keys, q_pos, q_chunk, k_pos, k_chunk, lam, subln_g, lam_init):
    sb_k, sb_v, df_k, df_v = keys

    def block(args):
        sbq, dfq, qp, qc = args
        Bq, Qb = sbq.shape[:2]
        o_sb = stick_breaking_block(sbq, sb_k, sb_v, qp, k_pos)
        o_df = diff_attn_block(dfq, df_k, df_v, qc, k_chunk, lam, subln_g, lam_init)
        out = jnp.concatenate([o_sb.reshape(Bq, Qb, SB_WIDTH), o_df.reshape(Bq, Qb, DF_WIDTH)], axis=-1)
        return out.astype(sbq.dtype)

    B, T = sb_q.shape[:2]
    if T <= QBLOCK:
        return block((sb_q, df_q, q_pos, q_chunk))
    nb = -(-T // QBLOCK)
    pad = nb * QBLOCK - T

    def to_blocks(a):
        a = jnp.pad(a, ((0, 0), (0, pad), (0, 0), (0, 0)))
        return jnp.moveaxis(a.reshape(B, nb, QBLOCK, *a.shape[2:]), 1, 0)

    qp = jnp.pad(q_pos, (0, pad), mode='edge').reshape(nb, QBLOCK)
    qc = jnp.pad(q_chunk, (0, pad), mode='edge').reshape(nb, QBLOCK)
    out = lax.map(block, (to_blocks(sb_q), to_blocks(df_q), qp, qc))
    return jnp.moveaxis(out, 0, 1).reshape(B, nb * QBLOCK, MIX_WIDTH)[:, :T]


def hier_moe(xn, w_group, b_group, w_router, b_router, w_gate, w_up, w_down):
    shape = xn.shape
    xf = xn.reshape(-1, D_MODEL)
    T = xf.shape[0]
    g_logits = jnp.einsum('td,dg->tg', xf, w_group, preferred_element_type=jnp.float32) + b_group.astype(jnp.float32)
    g_prob = jax.nn.softmax(g_logits, axis=-1)
    g_idx = jnp.argmax(g_logits, axis=-1)
    g_w = jnp.take_along_axis(g_prob, g_idx[:, None], axis=1)
    e_logits = jnp.einsum('td,de->te', xf, w_router, preferred_element_type=jnp.float32) + b_router.astype(jnp.float32)
    e_logits = e_logits.reshape(T, N_GROUPS, EXPERTS_PER_GROUP)
    e_logits = jnp.take_along_axis(e_logits, g_idx[:, None, None], axis=1)[:, 0]
    top_p, top_i = lax.top_k(jax.nn.softmax(e_logits, axis=-1), TOP_K_IN_GROUP)
    top_p = top_p / jnp.sum(top_p, axis=-1, keepdims=True)
    expert_id = g_idx[:, None] * EXPERTS_PER_GROUP + top_i
    gates = jnp.sum(jax.nn.one_hot(expert_id, N_EXPERTS, dtype=jnp.float32) * (g_w * top_p)[..., None], axis=1)

    def expert(acc, ew):
        wg, wu, wd, gate = ew
        h = jax.nn.silu(xf @ wg) * (xf @ wu)
        return acc + (h @ wd).astype(jnp.float32) * gate[:, None], None

    acc, _ = lax.scan(expert, jnp.zeros((T, D_MODEL), jnp.float32), (w_gate, w_up, w_down, gates.T))
    return acc.astype(xn.dtype).reshape(shape)


def layer_step(h, q_pos, q_chunk, past, past_pos, past_chunk, norm_mix_g, w_in, lam, subln_g, lam_init,
               w_out, norm_ffn_g, w_group, b_group, w_router, b_router, w_gate, w_up, w_down):
    xn = rms_norm(h, norm_mix_g)
    sb_q, sb_k, sb_v, df_q, df_k, df_v = project(xn, w_in, q_pos)
    new_rows = (sb_k, sb_v, df_k, df_v)
    if past is None:
        keys, k_pos, k_chunk = new_rows, q_pos, q_chunk
    else:
        keys = tuple(jnp.concatenate([c.astype(n.dtype), n], axis=1) for c, n in zip(past, new_rows))
        k_pos = jnp.concatenate([past_pos, q_pos])
        k_chunk = jnp.concatenate([past_chunk, q_chunk])
    mix = token_mixers(sb_q, df_q, keys, q_pos, q_chunk, k_pos, k_chunk, lam, subln_g, lam_init)
    h = h + jnp.einsum('btc,cd->btd', mix, w_out)
    h = h + hier_moe(rms_norm(h, norm_ffn_g), w_group, b_group, w_router, b_router, w_gate, w_up, w_down)
    return h, new_rows


def setup_inputs(seed: int = 0) -> dict:
    key = jax.random.key(seed)
    ks = jax.random.split(key, 24)
    f32 = jnp.float32
    nrm = lambda k, shape, s=1.0: (jax.random.normal(k, shape, f32) * s)
    return {
        'x_prompt': nrm(ks[0], (BATCH, SEQ, D_MODEL)),
        'x_sample': nrm(ks[1], (DEC_BATCH, DEC_SEQ, D_MODEL)),
        'cache_sb_k': nrm(ks[2], (DEPTH, DEC_BATCH, PAST_LEN, SB_HEADS, SB_DIM)),
        'cache_sb_v': nrm(ks[3], (DEPTH, DEC_BATCH, PAST_LEN, SB_HEADS, SB_DIM)),
        'cache_diff_k': nrm(ks[4], (DEPTH, DEC_BATCH, PAST_LEN, 2 * DF_HEADS, DF_DIM)),
        'cache_diff_v': nrm(ks[5], (DEPTH, DEC_BATCH, PAST_LEN, DF_HEADS, DF_VDIM)),
        'meta_tokens': nrm(ks[6], (N_META, D_MODEL)),
        'norm_mix_g': 1.0 + nrm(ks[7], (DEPTH, D_MODEL), 0.02),
        'w_in': nrm(ks[8], (DEPTH, D_MODEL, IN_COLS), D_MODEL ** -0.5),
        'lambda_q1': nrm(ks[9], (DEPTH, DF_DIM), 0.1),
        'lambda_k1': nrm(ks[10], (DEPTH, DF_DIM), 0.1),
        'lambda_q2': nrm(ks[11], (DEPTH, DF_DIM), 0.1),
        'lambda_k2': nrm(ks[12], (DEPTH, DF_DIM), 0.1),
        'subln_g': 1.0 + nrm(ks[13], (DEPTH, DF_VDIM), 0.02),
        'w_out': nrm(ks[14], (DEPTH, MIX_WIDTH, D_MODEL), MIX_WIDTH ** -0.5),
        'norm_ffn_g': 1.0 + nrm(ks[15], (DEPTH, D_MODEL), 0.02),
        'w_group': nrm(ks[16], (DEPTH, D_MODEL, N_GROUPS), D_MODEL ** -0.5),
        'b_group': nrm(ks[17], (DEPTH, N_GROUPS), 0.01),
        'w_router': nrm(ks[18], (DEPTH, D_MODEL, N_EXPERTS), D_MODEL ** -0.5),
        'b_router': nrm(ks[19], (DEPTH, N_EXPERTS), 0.01),
        'w_gate': nrm(ks[20], (DEPTH, N_EXPERTS, D_MODEL, D_EXPERT), D_MODEL ** -0.5),
        'w_up': nrm(ks[21], (DEPTH, N_EXPERTS, D_MODEL, D_EXPERT), D_MODEL ** -0.5),
        'w_down': nrm(ks[22], (DEPTH, N_EXPERTS, D_EXPERT, D_MODEL), D_EXPERT ** -0.5),
        'final_norm_g': 1.0 + nrm(ks[23], (D_MODEL,), 0.02),
    }


def reference(x_prompt, x_sample, cache_sb_k, cache_sb_v, cache_diff_k, cache_diff_v, meta_tokens,
              norm_mix_g, w_in, lambda_q1, lambda_k1, lambda_q2, lambda_k2, subln_g, w_out, norm_ffn_g,
              w_group, b_group, w_router, b_router, w_gate, w_up, w_down, final_norm_g):
    B, S, _ = x_prompt.shape
    Bs, Ss, _ = x_sample.shape
    past_len = cache_sb_k.shape[2]
    h_p = jnp.concatenate([jnp.broadcast_to(meta_tokens.astype(x_prompt.dtype)[None], (B, N_META, D_MODEL)), x_prompt], axis=1)
    pos_p = jnp.arange(N_META + S, dtype=jnp.int32)
    chunk_p = (pos_p - N_META) // CHUNK
    pos_s = past_len + jnp.arange(Ss, dtype=jnp.int32)
    chunk_s = pos_s // CHUNK
    past_pos = jnp.arange(past_len, dtype=jnp.int32)
    past_chunk = past_pos // CHUNK
    h_s = x_sample
    rows_p, rows_s = [], []
    for l in range(DEPTH):
        lam = (jnp.exp(jnp.sum(lambda_q1[l].astype(jnp.float32) * lambda_k1[l].astype(jnp.float32)))
               - jnp.exp(jnp.sum(lambda_q2[l].astype(jnp.float32) * lambda_k2[l].astype(jnp.float32)))
               + lambda_init(l))
        shared = (norm_mix_g[l], w_in[l], lam, subln_g[l], lambda_init(l), w_out[l], norm_ffn_g[l],
                  w_group[l], b_group[l], w_router[l], b_router[l], w_gate[l], w_up[l], w_down[l])
        h_p, new_p = layer_step(h_p, pos_p, chunk_p, None, None, None, *shared)
        past = (cache_sb_k[l], cache_sb_v[l], cache_diff_k[l], cache_diff_v[l])
        h_s, new_s = layer_step(h_s, pos_s, chunk_s, past, past_pos, past_chunk, *shared)
        rows_p.append(new_p)
        rows_s.append(new_s)
    y_prompt = rms_norm(h_p, final_norm_g)[:, N_META:]
    y_sample = rms_norm(h_s, final_norm_g)
    sb_k_prompt = jnp.stack([r[0] for r in rows_p])
    sb_v_prompt = jnp.stack([r[1] for r in rows_p])
    diff_k_prompt = jnp.stack([r[2] for r in rows_p])
    diff_v_prompt = jnp.stack([r[3] for r in rows_p])
    sb_k_sample = jnp.stack([r[0] for r in rows_s])
    sb_v_sample = jnp.stack([r[1] for r in rows_s])
    diff_k_sample = jnp.stack([r[2] for r in rows_s])
    diff_v_sample = jnp.stack([r[3] for r in rows_s])
    return (y_prompt, y_sample, sb_k_prompt, sb_v_prompt, diff_k_prompt, diff_v_prompt,
            sb_k_sample, sb_v_sample, diff_k_sample, diff_v_sample)
```

```python
import functools
import math

import jax
import jax.numpy as jnp
from jax import lax
from jax.experimental import pallas as pl
from jax.experimental.pallas import tpu as pltpu

F32 = jnp.float32
BF16 = jnp.bfloat16

CHUNK = 64
SB_HEADS = 8
DF_HEADS = 4
HEAD_DIM = 64
GROUP_W = 512
ROPE_THETA = 500000.0
ROT_DIM = 16
N_GROUPS = 4
EXPERTS_PER_GROUP = 8
N_EXPERTS = N_GROUPS * EXPERTS_PER_GROUP
EPS = 1e-6
LANES = 128
SB_CUTOFF = -88.0
VMEM_LIMIT = 56 * 1024 * 1024


def _cparams(sem):
    return pltpu.CompilerParams(dimension_semantics=sem, vmem_limit_bytes=VMEM_LIMIT)


def _inproj_kernel(x_ref, g_ref, w_ref, cos_ref, sin_ref,
                   qsb_ref, qdf_ref, ksb_ref, vsb_ref, kdf_ref, vdf_ref):
    x = x_ref[...]
    ms = jnp.mean(x * x, axis=-1, keepdims=True)
    xn = (x * lax.rsqrt(ms + EPS) * g_ref[...]).astype(BF16)
    proj = jnp.dot(xn, w_ref[...], preferred_element_type=F32)
    cos = cos_ref[...]
    sin = sin_ref[...]
    lane = lax.broadcasted_iota(jnp.int32, cos.shape, 1)
    first_half = (lane % ROT_DIM) < (ROT_DIM // 2)

    def rope(a):
        cols = []
        for j in range(GROUP_W // LANES):
            aj = a[:, j * LANES:(j + 1) * LANES]
            partner = jnp.where(first_half,
                                pltpu.roll(aj, LANES - ROT_DIM // 2, 1),
                                pltpu.roll(aj, ROT_DIM // 2, 1))
            cols.append(aj * cos + partner * sin)
        return jnp.concatenate(cols, axis=1)

    scale = HEAD_DIM ** -0.5
    qsb_ref[...] = (proj[:, 0:GROUP_W] * scale).astype(BF16)
    ksb_ref[...] = proj[:, GROUP_W:2 * GROUP_W]
    vsb_ref[...] = proj[:, 2 * GROUP_W:3 * GROUP_W]
    qdf_ref[...] = (rope(proj[:, 3 * GROUP_W:4 * GROUP_W]) * scale).astype(BF16)
    kdf_ref[...] = rope(proj[:, 4 * GROUP_W:5 * GROUP_W])
    vdf_ref[...] = proj[:, 5 * GROUP_W:6 * GROUP_W]


def _rope_tables(pos):
    half = ROT_DIM // 2
    inv = ROPE_THETA ** (-jnp.arange(half, dtype=F32) * 2.0 / ROT_DIM)
    ang = pos.astype(F32)[:, None] * inv[None, :]
    cos, sin = jnp.cos(ang), jnp.sin(ang)
    n = pos.shape[0]
    pad = HEAD_DIM - ROT_DIM
    cos_h = jnp.concatenate([cos, cos, jnp.ones((n, pad), F32)], axis=1)
    sin_h = jnp.concatenate([-sin, sin, jnp.zeros((n, pad), F32)], axis=1)
    reps = LANES // HEAD_DIM
    return jnp.tile(cos_h, (1, reps)), jnp.tile(sin_h, (1, reps))


def _inproj(x2d, g, w_bf, pos, tm):
    n, d = x2d.shape
    cos_t, sin_t = _rope_tables(pos)
    period = pos.shape[0]
    if period < tm:
        cos_t = jnp.tile(cos_t, (tm // period, 1))
        sin_t = jnp.tile(sin_t, (tm // period, 1))
    ntab = cos_t.shape[0] // tm
    row = lambda i: (i, 0)
    fixed = lambda i: (0, 0)
    tab = lambda i: (i % ntab, 0)
    out_spec = pl.BlockSpec((tm, GROUP_W), row)
    return pl.pallas_call(
        _inproj_kernel,
        grid=(n // tm,),
        in_specs=[pl.BlockSpec((tm, d), row), pl.BlockSpec((1, d), fixed),
                  pl.BlockSpec(w_bf.shape, fixed),
                  pl.BlockSpec((tm, LANES), tab), pl.BlockSpec((tm, LANES), tab)],
        out_specs=[out_spec] * 6,
        out_shape=[jax.ShapeDtypeStruct((n, GROUP_W), BF16)] * 2
                  + [jax.ShapeDtypeStruct((n, GROUP_W), F32)] * 4,
        compiler_params=_cparams(("parallel",)),
        name="inproj",
    )(x2d, g, w_bf, cos_t, sin_t)


def _sb_kernel(q_ref, kp_ref, vp_ref, ks_ref, vs_ref, o_ref, acc_ref, car_ref, *, tq, n_pre, bq, bp):
    lane = lax.broadcasted_iota(jnp.int32, (bq, LANES), 1)
    lo = lane < HEAD_DIM
    contract_last = (((1,), (1,)), ((), ()))

    def neg_suffix_matrix(n):
        j = lax.broadcasted_iota(jnp.int32, (n, n + LANES), 0)
        s = lax.broadcasted_iota(jnp.int32, (n, n + LANES), 1)
        return jnp.where((j > s) | (s >= n), -1.0, 0.0).astype(BF16)

    def sweep_block(k_ref, v_ref, q0, k0, n, causal):
        suffix = neg_suffix_matrix(n)
        if causal:
            r = lax.broadcasted_iota(jnp.int32, (bq, n), 0)
            c = lax.broadcasted_iota(jnp.int32, (bq, n), 1)
            vis = c < r
        for hp in range(SB_HEADS // 2):
            cols = slice(hp * LANES, (hp + 1) * LANES)
            k2 = k_ref[0, pl.ds(k0, n), cols].astype(BF16)
            v2 = v_ref[0, pl.ds(k0, n), cols].astype(BF16)
            q2 = q_ref[0, pl.ds(q0, bq), cols]
            pv = []
            for m in range(2):
                qm = jnp.where(lo if m == 0 else ~lo, q2, jnp.zeros_like(q2))
                s = lax.dot_general(qm, k2, contract_last, preferred_element_type=F32)
                sp = jnp.maximum(s, 0.0) + jnp.log(1.0 + jnp.exp(-jnp.abs(s)))
                spm = jnp.where(vis, sp, 0.0) if causal else sp
                cr = jnp.dot(spm.astype(BF16), suffix, preferred_element_type=F32)
                car = car_ref[2 * hp + m]
                p = jnp.exp(s - sp + cr[:, :n] + car[:, :n])
                if causal:
                    p = jnp.where(vis, p, 0.0)
                pv.append(jnp.dot(p.astype(BF16), v2, preferred_element_type=F32))
                car_ref[2 * hp + m] = car + cr[:, n:]
            acc_ref[:, cols] += jnp.where(lo, pv[0], pv[1])

    def more_needed():
        return jnp.max(car_ref[...]) > SB_CUTOFF

    def q_block(qi, carry):
        q0 = pl.multiple_of(qi * bq, bq)
        acc_ref[...] = jnp.zeros_like(acc_ref)
        car_ref[...] = jnp.zeros_like(car_ref)
        sweep_block(ks_ref, vs_ref, q0, q0, bq, True)

        def cond(st):
            return (st[0] >= 0) & st[1]

        def self_body(st):
            sweep_block(ks_ref, vs_ref, q0, pl.multiple_of(st[0] * bq, bq), bq, False)
            return st[0] - 1, more_needed()

        _, go = lax.while_loop(cond, self_body, (qi - 1, more_needed()))

        def pre_body(st):
            sweep_block(kp_ref, vp_ref, q0, pl.multiple_of(st[0] * bp, bp), bp, False)
            return st[0] - 1, more_needed()

        lax.while_loop(cond, pre_body, (jnp.int32(n_pre // bp - 1), go))
        o_ref[0, pl.ds(q0, bq), :] = acc_ref[...].astype(o_ref.dtype)
        return carry

    lax.fori_loop(0, tq // bq, q_block, 0)


def _sb_attention(q, k_pre, v_pre, k_self, v_self):
    b, tq, w = q.shape
    n_pre = k_pre.shape[1]
    bq = min(128, tq)
    bp = min(128, n_pre)
    pre_map = (lambda i: (i, 0, 0)) if k_pre.shape[0] == b else (lambda i: (0, 0, 0))
    bat = lambda i: (i, 0, 0)
    kern = functools.partial(_sb_kernel, tq=tq, n_pre=n_pre, bq=bq, bp=bp)
    return pl.pallas_call(
        kern,
        grid=(b,),
        in_specs=[pl.BlockSpec((1, tq, w), bat),
                  pl.BlockSpec((1, n_pre, w), pre_map), pl.BlockSpec((1, n_pre, w), pre_map),
                  pl.BlockSpec((1, tq, w), bat), pl.BlockSpec((1, tq, w), bat)],
        out_specs=pl.BlockSpec((1, tq, w), bat),
        out_shape=jax.ShapeDtypeStruct((b, tq, w), BF16),
        scratch_shapes=[pltpu.VMEM((bq, w), F32), pltpu.VMEM((SB_HEADS, bq, LANES), F32)],
        compiler_params=_cparams(("parallel",)),
        name="sb_attention",
    )(q, k_pre, v_pre, k_self, v_self)


def _df_kernel(q_ref, kp_ref, vp_ref, ks_ref, vs_ref, lq1_ref, lk1_ref, lq2_ref, lk2_ref, g_ref,
               o_ref, m_ref, l_ref, acc_ref, *, tq, n_pre, bq, bp, lam_init):
    lane = lax.broadcasted_iota(jnp.int32, (bq, LANES), 1)
    lo = lane < HEAD_DIM
    contract_last = (((1,), (1,)), ((), ()))
    lam = (jnp.exp(jnp.sum(lq1_ref[...] * lk1_ref[...], axis=-1, keepdims=True))
           - jnp.exp(jnp.sum(lq2_ref[...] * lk2_ref[...], axis=-1, keepdims=True)) + lam_init)

    def q_block(qi, carry):
        q0 = pl.multiple_of(qi * bq, bq)
        for h in range(DF_HEADS):
            cols = slice(h * LANES, (h + 1) * LANES)
            q2 = q_ref[0, pl.ds(q0, bq), cols]
            qm = [jnp.where(lo, q2, jnp.zeros_like(q2)), jnp.where(lo, jnp.zeros_like(q2), q2)]
            m_ref[...] = jnp.full_like(m_ref, -jnp.inf)
            l_ref[...] = jnp.zeros_like(l_ref)
            acc_ref[...] = jnp.zeros_like(acc_ref)

            def tile(k_ref, v_ref, k0, n, chunk_mask):
                k2 = k_ref[0, pl.ds(k0, n), cols].astype(BF16)
                v2 = v_ref[0, pl.ds(k0, n), cols].astype(BF16)
                for m in range(2):
                    s = lax.dot_general(qm[m], k2, contract_last, preferred_element_type=F32)
                    if chunk_mask:
                        r = lax.broadcasted_iota(jnp.int32, (bq, n), 0)
                        c = lax.broadcasted_iota(jnp.int32, (bq, n), 1)
                        s = jnp.where((c // CHUNK) <= (r // CHUNK), s, -jnp.inf)
                    m_prev = m_ref[m]
                    m_new = jnp.maximum(m_prev, jnp.max(s, axis=1, keepdims=True))
                    alpha = jnp.exp(m_prev - m_new)
                    p = jnp.exp(s - m_new[:, :1])
                    l_ref[m] = alpha * l_ref[m] + jnp.sum(p, axis=1, keepdims=True)
                    acc_ref[m] = alpha * acc_ref[m] + jnp.dot(p.astype(BF16), v2, preferred_element_type=F32)
                    m_ref[m] = m_new

            def pre_body(j, c):
                tile(kp_ref, vp_ref, pl.multiple_of(j * bp, bp), bp, False)
                return c

            lax.fori_loop(0, n_pre // bp, pre_body, 0)

            def self_body(j, c):
                tile(ks_ref, vs_ref, pl.multiple_of(j * bq, bq), bq, False)
                return c

            lax.fori_loop(0, qi, self_body, 0)
            tile(ks_ref, vs_ref, q0, bq, True)

            o = acc_ref[0] / l_ref[0] - lam * (acc_ref[1] / l_ref[1])
            y = o * lax.rsqrt(jnp.mean(o * o, axis=-1, keepdims=True) + EPS) * g_ref[...]
            o_ref[0, pl.ds(q0, bq), cols] = (y * (1.0 - lam_init)).astype(o_ref.dtype)
        return carry

    lax.fori_loop(0, tq // bq, q_block, 0)


def _df_attention(q, k_pre, v_pre, k_self, v_self, lq1, lk1, lq2, lk2, subln_g, lam_init):
    b, tq, w = q.shape
    n_pre = k_pre.shape[1]
    bq = min(256, tq)
    bp = min(256, n_pre)
    pre_map = (lambda i: (i, 0, 0)) if k_pre.shape[0] == b else (lambda i: (0, 0, 0))
    bat = lambda i: (i, 0, 0)
    fixed = lambda i: (0, 0)
    kern = functools.partial(_df_kernel, tq=tq, n_pre=n_pre, bq=bq, bp=bp, lam_init=lam_init)
    vec = pl.BlockSpec((1, HEAD_DIM), fixed)
    return pl.pallas_call(
        kern,
        grid=(b,),
        in_specs=[pl.BlockSpec((1, tq, w), bat),
                  pl.BlockSpec((1, n_pre, w), pre_map), pl.BlockSpec((1, n_pre, w), pre_map),
                  pl.BlockSpec((1, tq, w), bat), pl.BlockSpec((1, tq, w), bat),
                  vec, vec, vec, vec, pl.BlockSpec((1, LANES), fixed)],
        out_specs=pl.BlockSpec((1, tq, w), bat),
        out_shape=jax.ShapeDtypeStruct((b, tq, w), BF16),
        scratch_shapes=[pltpu.VMEM((2, bq, LANES), F32)] * 3,
        compiler_params=_cparams(("parallel",)),
        name="df_attention",
    )(q, k_pre, v_pre, k_self, v_self, lq1, lk1, lq2, lk2, subln_g)


def _outproj_kernel(x_ref, sb_ref, df_ref, wsb_ref, wdf_ref, g_ref, wr_ref, br_ref,
                    h_ref, xn_ref, route_ref):
    h = (x_ref[...]
         + jnp.dot(sb_ref[...], wsb_ref[...], preferred_element_type=F32)
         + jnp.dot(df_ref[...], wdf_ref[...], preferred_element_type=F32))
    h_ref[...] = h
    xn = h * lax.rsqrt(jnp.mean(h * h, axis=-1, keepdims=True) + EPS) * g_ref[...]
    xn_ref[...] = xn
    logits = jnp.dot(xn.astype(BF16), wr_ref[...], preferred_element_type=F32) + br_ref[...]
    lane = lax.broadcasted_iota(jnp.int32, logits.shape, 1)
    big = jnp.int32(LANES)
    neg = -jnp.inf

    def first_argmax(v):
        mx = jnp.max(v, axis=1, keepdims=True)
        idx = jnp.min(jnp.where(v == mx, lane, big), axis=1, keepdims=True)
        return mx, idx

    gl = jnp.where((lane >= N_EXPERTS) & (lane < N_EXPERTS + N_GROUPS), logits, neg)
    gmax, gidx = first_argmax(gl)
    g_w = 1.0 / jnp.sum(jnp.exp(gl - gmax), axis=1, keepdims=True)
    grp = gidx - N_EXPERTS
    el = jnp.where((lane < N_EXPERTS) & (lane // EXPERTS_PER_GROUP == grp), logits, neg)
    m1, i1 = first_argmax(el)
    m2, i2 = first_argmax(jnp.where(lane == i1, neg, el))
    e21 = jnp.exp(m2 - m1)
    t1 = 1.0 / (1.0 + e21)
    t2 = e21 / (1.0 + e21)
    route = jnp.where(lane == 0, i1.astype(F32),
            jnp.where(lane == 1, i2.astype(F32),
            jnp.where(lane == 2, g_w * t1,
            jnp.where(lane == 3, g_w * t2, 0.0))))
    route_ref[...] = route


def _outproj_route(x2d, sb, df, wsb, wdf, g, wr, br, tm):
    n, d = x2d.shape
    row = lambda i: (i, 0)
    fixed = lambda i: (0, 0)
    return pl.pallas_call(
        _outproj_kernel,
        grid=(n // tm,),
        in_specs=[pl.BlockSpec((tm, d), row), pl.BlockSpec((tm, GROUP_W), row), pl.BlockSpec((tm, GROUP_W), row),
                  pl.BlockSpec(wsb.shape, fixed), pl.BlockSpec(wdf.shape, fixed), pl.BlockSpec((1, d), fixed),
                  pl.BlockSpec(wr.shape, fixed), pl.BlockSpec((1, LANES), fixed)],
        out_specs=[pl.BlockSpec((tm, d), row), pl.BlockSpec((tm, d), row), pl.BlockSpec((tm, LANES), row)],
        out_shape=[jax.ShapeDtypeStruct((n, d), F32), jax.ShapeDtypeStruct((n, d), F32),
                   jax.ShapeDtypeStruct((n, LANES), F32)],
        compiler_params=_cparams(("parallel",)),
        name="outproj_route",
    )(x2d, sb, df, wsb, wdf, g, wr, br)


def _row_copy(src, s, dst, d, sem):
    return pltpu.make_async_copy(src.at[pl.ds(s, 1)], dst.at[pl.ds(d, 1)], sem)


def _dispatch_kernel(d0_ref, d1_ref, x_hbm, xs_in_hbm, xs_hbm, sem, *, tm):
    del xs_in_hbm
    i = pl.program_id(0)
    base = i * tm

    def issue(r, c):
        _row_copy(x_hbm, base + r, xs_hbm, d0_ref[0, 0, r], sem).start()
        _row_copy(x_hbm, base + r, xs_hbm, d1_ref[0, 0, r], sem).start()
        return c

    lax.fori_loop(0, tm, issue, 0)
    pltpu.make_async_copy(x_hbm.at[pl.ds(0, 2 * tm)], xs_hbm.at[pl.ds(0, 2 * tm)], sem).wait()


def _dispatch(dest0, dest1, x2d, xs, tm):
    n = x2d.shape[0]
    idx = lambda i: (i, 0, 0)
    smem = functools.partial(pl.BlockSpec, memory_space=pltpu.SMEM)
    any_spec = pl.BlockSpec(memory_space=pl.ANY)
    return pl.pallas_call(
        functools.partial(_dispatch_kernel, tm=tm),
        grid=(n // tm,),
        in_specs=[smem((1, 1, tm), idx), smem((1, 1, tm), idx), any_spec, any_spec],
        out_specs=any_spec,
        out_shape=jax.ShapeDtypeStruct(xs.shape, xs.dtype),
        scratch_shapes=[pltpu.SemaphoreType.DMA(())],
        input_output_aliases={3: 0},
        compiler_params=_cparams(("arbitrary",)),
        name="moe_dispatch",
    )(dest0.reshape(n // tm, 1, tm), dest1.reshape(n // tm, 1, tm), x2d, xs)


def _expert_kernel(te_ref, nt_ref, x_ref, wgu_ref, wd_ref, y_ref, *, de):
    del te_ref
    active = pl.program_id(0) < nt_ref[0]

    @pl.when(jnp.logical_not(active))
    def _():
        y_ref[...] = jnp.zeros_like(y_ref)

    @pl.when(active)
    def _():
        x = x_ref[...].astype(BF16)
        gu = jnp.dot(x, wgu_ref[0], preferred_element_type=F32)
        g = gu[:, :de]
        h = (g * (1.0 / (1.0 + jnp.exp(-g)))) * gu[:, de:]
        y_ref[...] = jnp.dot(h.astype(BF16), wd_ref[0], preferred_element_type=F32)


def _experts(tile_expert, n_tiles, xs, wgu, wd, tm):
    p, d = xs.shape
    de = wd.shape[1]
    last = lambda i, te, nt: jnp.minimum(i, nt[0] - 1)
    grid_spec = pltpu.PrefetchScalarGridSpec(
        num_scalar_prefetch=2,
        grid=(p // tm,),
        in_specs=[pl.BlockSpec((tm, d), lambda i, te, nt: (last(i, te, nt), 0)),
                  pl.BlockSpec((1, d, 2 * de), lambda i, te, nt: (te[last(i, te, nt)], 0, 0)),
                  pl.BlockSpec((1, de, d), lambda i, te, nt: (te[last(i, te, nt)], 0, 0))],
        out_specs=pl.BlockSpec((tm, d), lambda i, te, nt: (i, 0)),
    )
    return pl.pallas_call(
        functools.partial(_expert_kernel, de=de),
        grid_spec=grid_spec,
        out_shape=jax.ShapeDtypeStruct((p, d), F32),
        compiler_params=_cparams(("arbitrary",)),
        name="moe_experts",
    )(tile_expert, n_tiles, xs, wgu, wd)


def _combine_kernel(d0_ref, d1_ref, h_ref, gate_ref, g_ref, ys_hbm, o_ref, y0_ref, y1_ref, sem, *, tm):
    def issue(r, c):
        _row_copy(ys_hbm, d0_ref[0, 0, r], y0_ref, r, sem).start()
        _row_copy(ys_hbm, d1_ref[0, 0, r], y1_ref, r, sem).start()
        return c

    lax.fori_loop(0, tm, issue, 0)
    pltpu.make_async_copy(ys_hbm.at[pl.ds(0, tm)], y0_ref, sem).wait()
    pltpu.make_async_copy(ys_hbm.at[pl.ds(0, tm)], y1_ref, sem).wait()
    gate = gate_ref[...]
    h = h_ref[...] + (y0_ref[...] * gate[:, 2:3] + y1_ref[...] * gate[:, 3:4])
    o_ref[...] = h * lax.rsqrt(jnp.mean(h * h, axis=-1, keepdims=True) + EPS) * g_ref[...]


def _combine(dest0, dest1, h2d, route, g, ys, tm):
    n, d = h2d.shape
    idx = lambda i: (i, 0, 0)
    row = lambda i: (i, 0)
    smem = functools.partial(pl.BlockSpec, memory_space=pltpu.SMEM)
    return pl.pallas_call(
        functools.partial(_combine_kernel, tm=tm),
        grid=(n // tm,),
        in_specs=[smem((1, 1, tm), idx), smem((1, 1, tm), idx),
                  pl.BlockSpec((tm, d), row), pl.BlockSpec((tm, LANES), row),
                  pl.BlockSpec((1, d), lambda i: (0, 0)), pl.BlockSpec(memory_space=pl.ANY)],
        out_specs=pl.BlockSpec((tm, d), row),
        out_shape=jax.ShapeDtypeStruct((n, d), F32),
        scratch_shapes=[pltpu.VMEM((tm, d), F32), pltpu.VMEM((tm, d), F32), pltpu.SemaphoreType.DMA(())],
        compiler_params=_cparams(("arbitrary",)),
        name="moe_combine",
    )(dest0.reshape(n // tm, 1, tm), dest1.reshape(n // tm, 1, tm), h2d, route, g, ys)


def _routing_plan(route, tm):
    t = route.shape[0]
    e = route[:, :2].astype(jnp.int32).reshape(-1)
    onehot = (e[:, None] == jnp.arange(N_EXPERTS, dtype=jnp.int32)[None, :])
    blk = 512
    oh = onehot.astype(BF16).reshape(2 * t // blk, blk, N_EXPERTS)
    tri = (jnp.arange(blk)[:, None] > jnp.arange(blk)[None, :]).astype(BF16)
    within = jnp.einsum("ij,bjk->bik", tri, oh, preferred_element_type=F32)
    totals = jnp.sum(oh.astype(F32), axis=1)
    before = jnp.cumsum(totals, axis=0) - totals
    rank_all = (within + before[:, None, :]).reshape(2 * t, N_EXPERTS)
    rank = jnp.sum(jnp.where(onehot, rank_all, 0.0), axis=1).astype(jnp.int32)
    counts = jnp.sum(totals, axis=0).astype(jnp.int32)
    padded = ((counts + tm - 1) // tm) * tm
    ends = jnp.cumsum(padded)
    starts = ends - padded
    dest = (starts[e] + rank).reshape(t, 2)
    n_slots = 2 * t + N_EXPERTS * tm
    tile_start = jnp.arange(n_slots // tm, dtype=jnp.int32) * tm
    tile_expert = jnp.minimum(jnp.sum(tile_start[:, None] >= ends[None, :], axis=1), N_EXPERTS - 1)
    n_tiles = (ends[-1] // tm).reshape(1)
    return dest[:, 0], dest[:, 1], tile_expert.astype(jnp.int32), n_tiles.astype(jnp.int32), n_slots


def _pick(n, prefs):
    for p in prefs:
        if n % p == 0:
            return p
    return n


def kernel(x_prompt, x_sample, cache_sb_k, cache_sb_v, cache_diff_k, cache_diff_v, meta_tokens, norm_mix_g, w_in, lambda_q1, lambda_k1, lambda_q2, lambda_k2, subln_g, w_out, norm_ffn_g, w_group, b_group, w_router, b_router, w_gate, w_up, w_down, final_norm_g):
    depth = w_in.shape[0]
    assert depth == 1, "meta-token rows are only dropped after a single layer"
    b, s, d = x_prompt.shape
    bs, ss, _ = x_sample.shape
    past = cache_sb_k.shape[2]
    n_meta = meta_tokens.shape[0]
    lam_init = 0.8 - 0.6 * math.exp(-0.3 * 0)

    g_mix = norm_mix_g[0].reshape(1, d)
    w_in_bf = w_in[0].astype(BF16)
    pos_p = n_meta + jnp.arange(s, dtype=jnp.int32)
    pos_s = past + jnp.arange(ss, dtype=jnp.int32)
    pos_m = jnp.arange(n_meta, dtype=jnp.int32)
    xp2 = x_prompt.reshape(b * s, d)
    xs2 = x_sample.reshape(bs * ss, d)

    tm_p = _pick(s, (512, 256, 128))
    qsb_p, qdf_p, ksb_p, vsb_p, kdf_p, vdf_p = _inproj(xp2, g_mix, w_in_bf, pos_p, tm_p)
    tm_s = _pick(bs * ss, (512, 256, 128))
    qsb_s, qdf_s, ksb_s, vsb_s, kdf_s, vdf_s = _inproj(xs2, g_mix, w_in_bf, pos_s, tm_s)
    _, _, ksb_m, vsb_m, kdf_m, vdf_m = _inproj(meta_tokens, g_mix, w_in_bf, pos_m, n_meta)

    r3 = lambda a, bb, t: a.reshape(bb, t, GROUP_W)
    lam_vecs = [v[0].reshape(1, HEAD_DIM) for v in (lambda_q1, lambda_k1, lambda_q2, lambda_k2)]
    sub_g = subln_g[0].reshape(1, LANES)

    sb_p = _sb_attention(r3(qsb_p, b, s), ksb_m[None], vsb_m[None], r3(ksb_p, b, s), r3(vsb_p, b, s))
    df_p = _df_attention(r3(qdf_p, b, s), kdf_m[None], vdf_m[None], r3(kdf_p, b, s), r3(vdf_p, b, s),
                         *lam_vecs, sub_g, lam_init)
    sb_s = _sb_attention(r3(qsb_s, bs, ss), cache_sb_k[0].reshape(bs, past, GROUP_W),
                         cache_sb_v[0].reshape(bs, past, GROUP_W), r3(ksb_s, bs, ss), r3(vsb_s, bs, ss))
    df_s = _df_attention(r3(qdf_s, bs, ss), cache_diff_k[0].reshape(bs, past, GROUP_W),
                         cache_diff_v[0].reshape(bs, past, GROUP_W), r3(kdf_s, bs, ss), r3(vdf_s, bs, ss),
                         *lam_vecs, sub_g, lam_init)

    w_out_bf = w_out[0].astype(BF16)
    wsb, wdf = w_out_bf[:GROUP_W], w_out_bf[GROUP_W:]
    g_ffn = norm_ffn_g[0].reshape(1, d)
    pad_cols = LANES - N_EXPERTS - N_GROUPS
    wr = jnp.concatenate([w_router[0], w_group[0], jnp.zeros((d, pad_cols), F32)], axis=1).astype(BF16)
    br = jnp.concatenate([b_router[0], b_group[0], jnp.zeros((pad_cols,), F32)]).reshape(1, LANES)

    h_p, xn_p, route_p = _outproj_route(xp2, sb_p.reshape(b * s, GROUP_W), df_p.reshape(b * s, GROUP_W),
                                        wsb, wdf, g_ffn, wr, br, tm_p)
    h_s, xn_s, route_s = _outproj_route(xs2, sb_s.reshape(bs * ss, GROUP_W), df_s.reshape(bs * ss, GROUP_W),
                                        wsb, wdf, g_ffn, wr, br, tm_s)

    tm_e = 512
    tm_d = 256
    n_p = b * s
    route = jnp.concatenate([route_p, route_s], axis=0)
    dest0, dest1, tile_expert, n_tiles, n_slots = _routing_plan(route, tm_e)
    xsorted = jnp.zeros((n_slots, d), F32)
    xsorted = _dispatch(dest0[:n_p], dest1[:n_p], xn_p, xsorted, tm_d)
    xsorted = _dispatch(dest0[n_p:], dest1[n_p:], xn_s, xsorted, tm_d)
    wgu = jnp.concatenate([w_gate[0], w_up[0]], axis=-1).astype(BF16)
    ysorted = _experts(tile_expert, n_tiles, xsorted, wgu, w_down[0].astype(BF16), tm_e)
    g_fin = final_norm_g.reshape(1, d)
    y_p = _combine(dest0[:n_p], dest1[:n_p], h_p, route_p, g_fin, ysorted, tm_d)
    y_s = _combine(dest0[n_p:], dest1[n_p:], h_s, route_s, g_fin, ysorted, tm_d)

    def with_meta(meta_rows, frames, heads, hd):
        m = jnp.broadcast_to(meta_rows[None], (b, n_meta, GROUP_W))
        return jnp.concatenate([m, frames.reshape(b, s, GROUP_W)], axis=1).reshape(1, b, n_meta + s, heads, hd)

    shp = lambda a, heads, hd: a.reshape(1, bs, ss, heads, hd)
    return (y_p.reshape(b, s, d), y_s.reshape(bs, ss, d),
            with_meta(ksb_m, ksb_p, SB_HEADS, HEAD_DIM), with_meta(vsb_m, vsb_p, SB_HEADS, HEAD_DIM),
            with_meta(kdf_m, kdf_p, 2 * DF_HEADS, HEAD_DIM), with_meta(vdf_m, vdf_p, DF_HEADS, 2 * HEAD_DIM),
            shp(ksb_s, SB_HEADS, HEAD_DIM), shp(vsb_s, SB_HEADS, HEAD_DIM),
            shp(kdf_s, 2 * DF_HEADS, HEAD_DIM), shp(vdf_s, DF_HEADS, 2 * HEAD_DIM))
```

```python
import functools
import math

import jax
import jax.numpy as jnp
from jax import lax
from jax.experimental import pallas as pl
from jax.experimental.pallas import tpu as pltpu

F32 = jnp.float32
BF16 = jnp.bfloat16

CHUNK = 64
SB_HEADS = 8
DF_HEADS = 4
HEAD_DIM = 64
GROUP_W = 512
ROPE_THETA = 500000.0
ROT_DIM = 16
N_GROUPS = 4
EXPERTS_PER_GROUP = 8
N_EXPERTS = N_GROUPS * EXPERTS_PER_GROUP
EPS = 1e-6
LANES = 128
SB_CUTOFF = -88.0
VMEM_LIMIT = 56 * 1024 * 1024


def _cparams(sem):
    return pltpu.CompilerParams(dimension_semantics=sem, vmem_limit_bytes=VMEM_LIMIT)


def _inproj_kernel(x_ref, g_ref, w_ref, cos_ref, sin_ref,
                   qsb_ref, qdf_ref, ksb_ref, vsb_ref, kdf_ref, vdf_ref,
                   ksb_bf_ref, vsb_bf_ref, kdf_bf_ref, vdf_bf_ref):
    x = x_ref[...]
    ms = jnp.mean(x * x, axis=-1, keepdims=True)
    xn = (x * lax.rsqrt(ms + EPS) * g_ref[...]).astype(BF16)
    proj = jnp.dot(xn, w_ref[...], preferred_element_type=F32)
    cos = cos_ref[...]
    sin = sin_ref[...]
    lane = lax.broadcasted_iota(jnp.int32, cos.shape, 1)
    first_half = (lane % ROT_DIM) < (ROT_DIM // 2)

    def rope(a):
        cols = []
        for j in range(GROUP_W // LANES):
            aj = a[:, j * LANES:(j + 1) * LANES]
            partner = jnp.where(first_half,
                                pltpu.roll(aj, LANES - ROT_DIM // 2, 1),
                                pltpu.roll(aj, ROT_DIM // 2, 1))
            cols.append(aj * cos + partner * sin)
        return jnp.concatenate(cols, axis=1)

    scale = HEAD_DIM ** -0.5
    qsb_ref[...] = (proj[:, 0:GROUP_W] * scale).astype(BF16)
    qdf_ref[...] = (rope(proj[:, 3 * GROUP_W:4 * GROUP_W]) * scale).astype(BF16)
    kdf = rope(proj[:, 4 * GROUP_W:5 * GROUP_W])
    for f32_ref, bf_ref, val in ((ksb_ref, ksb_bf_ref, proj[:, GROUP_W:2 * GROUP_W]),
                                 (vsb_ref, vsb_bf_ref, proj[:, 2 * GROUP_W:3 * GROUP_W]),
                                 (kdf_ref, kdf_bf_ref, kdf),
                                 (vdf_ref, vdf_bf_ref, proj[:, 5 * GROUP_W:6 * GROUP_W])):
        f32_ref[...] = val
        bf_ref[...] = val.astype(BF16)


def _rope_tables(pos):
    half = ROT_DIM // 2
    inv = ROPE_THETA ** (-jnp.arange(half, dtype=F32) * 2.0 / ROT_DIM)
    ang = pos.astype(F32)[:, None] * inv[None, :]
    cos, sin = jnp.cos(ang), jnp.sin(ang)
    n = pos.shape[0]
    pad = HEAD_DIM - ROT_DIM
    cos_h = jnp.concatenate([cos, cos, jnp.ones((n, pad), F32)], axis=1)
    sin_h = jnp.concatenate([-sin, sin, jnp.zeros((n, pad), F32)], axis=1)
    reps = LANES // HEAD_DIM
    return jnp.tile(cos_h, (1, reps)), jnp.tile(sin_h, (1, reps))


def _inproj(x2d, g, w_bf, pos, tm):
    n, d = x2d.shape
    cos_t, sin_t = _rope_tables(pos)
    period = pos.shape[0]
    if period < tm:
        cos_t = jnp.tile(cos_t, (tm // period, 1))
        sin_t = jnp.tile(sin_t, (tm // period, 1))
    ntab = cos_t.shape[0] // tm
    row = lambda i: (i, 0)
    fixed = lambda i: (0, 0)
    tab = lambda i: (i % ntab, 0)
    out_spec = pl.BlockSpec((tm, GROUP_W), row)
    return pl.pallas_call(
        _inproj_kernel,
        grid=(n // tm,),
        in_specs=[pl.BlockSpec((tm, d), row), pl.BlockSpec((1, d), fixed),
                  pl.BlockSpec(w_bf.shape, fixed),
                  pl.BlockSpec((tm, LANES), tab), pl.BlockSpec((tm, LANES), tab)],
        out_specs=[out_spec] * 10,
        out_shape=[jax.ShapeDtypeStruct((n, GROUP_W), BF16)] * 2
                  + [jax.ShapeDtypeStruct((n, GROUP_W), F32)] * 4
                  + [jax.ShapeDtypeStruct((n, GROUP_W), BF16)] * 4,
        compiler_params=_cparams(("parallel",)),
        name="inproj",
    )(x2d, g, w_bf, cos_t, sin_t)


def _sb_kernel(q_ref, kp_ref, vp_ref, ks_ref, vs_ref, o_ref, acc_ref, car_ref, *, tq, n_pre, bq, bp):
    lane = lax.broadcasted_iota(jnp.int32, (bq, LANES), 1)
    lo = lane < HEAD_DIM
    contract_last = (((1,), (1,)), ((), ()))
    wn = 2 * bq if tq >= 2 * bq else bq

    def neg_suffix_matrix(n):
        j = lax.broadcasted_iota(jnp.int32, (n, n + LANES), 0)
        s = lax.broadcasted_iota(jnp.int32, (n, n + LANES), 1)
        return jnp.where((j > s) | (s >= n), -1.0, 0.0).astype(BF16)

    suffixes = {n: neg_suffix_matrix(n) for n in {wn, bq, bp}}

    def sweep(k_ref, v_ref, q0, k0, n, causal, first):
        suffix = suffixes[n]
        if causal:
            r = lax.broadcasted_iota(jnp.int32, (bq, n), 0)
            c = lax.broadcasted_iota(jnp.int32, (bq, n), 1)
            vis = (c + (k0 - q0)) < r
        worst = None
        for hp in range(SB_HEADS // 2):
            cols = slice(hp * LANES, (hp + 1) * LANES)
            k2 = k_ref[0, pl.ds(k0, n), cols].astype(BF16)
            v2 = v_ref[0, pl.ds(k0, n), cols].astype(BF16)
            q2 = q_ref[0, pl.ds(q0, bq), cols]
            pv = []
            for m in range(2):
                qm = jnp.where(lo if m == 0 else ~lo, q2, jnp.zeros_like(q2))
                s = lax.dot_general(qm, k2, contract_last, preferred_element_type=F32)
                sp = jnp.maximum(s, 0.0) + jnp.log(1.0 + jnp.exp(-jnp.abs(s)))
                spm = jnp.where(vis, sp, 0.0) if causal else sp
                cr = jnp.dot(spm.astype(BF16), suffix, preferred_element_type=F32)
                between = cr[:, :n]
                car = cr[:, n:]
                if not first:
                    old = car_ref[2 * hp + m]
                    between = between + (jnp.tile(old, (1, n // LANES)) if n % LANES == 0 else old[:, :n])
                    car = car + old
                p = jnp.exp(s - sp + between)
                if causal:
                    p = jnp.where(vis, p, 0.0)
                pv.append(jnp.dot(p.astype(BF16), v2, preferred_element_type=F32))
                car_ref[2 * hp + m] = car
                worst = car if worst is None else jnp.maximum(worst, car)
            both = jnp.where(lo, pv[0], pv[1])
            if first:
                acc_ref[:, cols] = both
            else:
                acc_ref[:, cols] += both
        return jnp.max(worst) > SB_CUTOFF

    def q_block(qi, carry):
        q0 = pl.multiple_of(qi * bq, bq)
        first_blk = jnp.maximum(qi + 1 - wn // bq, 0)
        go = sweep(ks_ref, vs_ref, q0, pl.multiple_of(first_blk * bq, bq), wn, True, True)

        def cond(st):
            return (st[0] >= 0) & st[1]

        def self_body(st):
            more = sweep(ks_ref, vs_ref, q0, pl.multiple_of(st[0] * bq, bq), bq, False, False)
            return st[0] - 1, more

        _, go = lax.while_loop(cond, self_body, (first_blk - 1, go))

        def pre_body(st):
            more = sweep(kp_ref, vp_ref, q0, pl.multiple_of(st[0] * bp, bp), bp, False, False)
            return st[0] - 1, more

        lax.while_loop(cond, pre_body, (jnp.int32(n_pre // bp - 1), go))
        o_ref[0, pl.ds(q0, bq), :] = acc_ref[...].astype(o_ref.dtype)
        return carry

    lax.fori_loop(0, tq // bq, q_block, 0)


def _sb_attention(q, k_pre, v_pre, k_self, v_self):
    b, tq, w = q.shape
    n_pre = k_pre.shape[1]
    bq = min(128, tq)
    bp = min(128, n_pre)
    pre_map = (lambda i: (i, 0, 0)) if k_pre.shape[0] == b else (lambda i: (0, 0, 0))
    bat = lambda i: (i, 0, 0)
    kern = functools.partial(_sb_kernel, tq=tq, n_pre=n_pre, bq=bq, bp=bp)
    return pl.pallas_call(
        kern,
        grid=(b,),
        in_specs=[pl.BlockSpec((1, tq, w), bat),
                  pl.BlockSpec((1, n_pre, w), pre_map), pl.BlockSpec((1, n_pre, w), pre_map),
                  pl.BlockSpec((1, tq, w), bat), pl.BlockSpec((1, tq, w), bat)],
        out_specs=pl.BlockSpec((1, tq, w), bat),
        out_shape=jax.ShapeDtypeStruct((b, tq, w), BF16),
        scratch_shapes=[pltpu.VMEM((bq, w), F32), pltpu.VMEM((SB_HEADS, bq, LANES), F32)],
        compiler_params=_cparams(("parallel",)),
        name="sb_attention",
    )(q, k_pre, v_pre, k_self, v_self)


def _df_kernel(q_ref, kp_ref, vp_ref, ks_ref, vs_ref, lq1_ref, lk1_ref, lq2_ref, lk2_ref, g_ref,
               o_ref, vxp_ref, vxs_ref, qm_ref, mx_ref, acc_ref, *, tq, n_pre, bq, bp, lam_init):
    n_maps = 2 * DF_HEADS
    lane = lax.broadcasted_iota(jnp.int32, (bq, LANES), 1)
    lo = lane < HEAD_DIM
    contract_last = (((1,), (1,)), ((), ()))
    lam = (jnp.exp(jnp.sum(lq1_ref[...] * lk1_ref[...], axis=-1, keepdims=True))
           - jnp.exp(jnp.sum(lq2_ref[...] * lk2_ref[...], axis=-1, keepdims=True)) + lam_init)

    def extend_values(v_ref, vx_ref):
        ones = jnp.ones((v_ref.shape[1], LANES), BF16)
        for h in range(DF_HEADS):
            vx_ref[:, 2 * h * LANES:(2 * h + 1) * LANES] = v_ref[0, :, h * LANES:(h + 1) * LANES].astype(BF16)
            vx_ref[:, (2 * h + 1) * LANES:(2 * h + 2) * LANES] = ones

    extend_values(vp_ref, vxp_ref)
    extend_values(vs_ref, vxs_ref)

    def scores(c, k_ref, k0, n, chunk_mask):
        h = c // 2
        k2 = k_ref[0, pl.ds(k0, n), h * LANES:(h + 1) * LANES].astype(BF16)
        s = lax.dot_general(qm_ref[c], k2, contract_last, preferred_element_type=F32)
        if chunk_mask:
            r = lax.broadcasted_iota(jnp.int32, (bq, n), 0)
            col = lax.broadcasted_iota(jnp.int32, (bq, n), 1)
            s = jnp.where((col // CHUNK) <= (r // CHUNK), s, -jnp.inf)
        return s

    def max_tile(k_ref, k0, n, chunk_mask):
        for c in range(n_maps):
            s = scores(c, k_ref, k0, n, chunk_mask)
            if n % LANES == 0:
                m = s[:, :LANES]
                for j in range(1, n // LANES):
                    m = jnp.maximum(m, s[:, j * LANES:(j + 1) * LANES])
            else:
                m = jnp.broadcast_to(jnp.max(s, axis=1, keepdims=True), (bq, LANES))
            mx_ref[c] = jnp.maximum(mx_ref[c], m)

    def value_tile(k_ref, vx_ref, k0, n, chunk_mask):
        for c in range(n_maps):
            h = c // 2
            s = scores(c, k_ref, k0, n, chunk_mask)
            row_max = mx_ref[c]
            if n % LANES == 0:
                p = jnp.concatenate([jnp.exp(s[:, j * LANES:(j + 1) * LANES] - row_max)
                                     for j in range(n // LANES)], axis=1)
            else:
                p = jnp.exp(s - row_max[:, :n])
            vx = vx_ref[pl.ds(k0, n), 2 * h * LANES:(2 * h + 2) * LANES]
            acc_ref[c] += jnp.dot(p.astype(BF16), vx, preferred_element_type=F32)

    def q_block(qi, carry):
        q0 = pl.multiple_of(qi * bq, bq)
        for h in range(DF_HEADS):
            q2 = q_ref[0, pl.ds(q0, bq), h * LANES:(h + 1) * LANES]
            zero = jnp.zeros_like(q2)
            qm_ref[2 * h] = jnp.where(lo, q2, zero)
            qm_ref[2 * h + 1] = jnp.where(lo, zero, q2)
        mx_ref[...] = jnp.full_like(mx_ref, -jnp.inf)
        acc_ref[...] = jnp.zeros_like(acc_ref)

        def sweep(pre_fn, self_fn):
            def pre_body(j, c):
                pre_fn(pl.multiple_of(j * bp, bp))
                return c

            def self_body(j, c):
                self_fn(pl.multiple_of(j * bq, bq), False)
                return c

            lax.fori_loop(0, n_pre // bp, pre_body, 0)
            lax.fori_loop(0, qi, self_body, 0)
            self_fn(q0, True)

        sweep(lambda k0: max_tile(kp_ref, k0, bp, False),
              lambda k0, msk: max_tile(ks_ref, k0, bq, msk))
        for c in range(n_maps):
            mx_ref[c] = jnp.broadcast_to(jnp.max(mx_ref[c], axis=1, keepdims=True), (bq, LANES))
        sweep(lambda k0: value_tile(kp_ref, vxp_ref, k0, bp, False),
              lambda k0, msk: value_tile(ks_ref, vxs_ref, k0, bq, msk))

        for h in range(DF_HEADS):
            a0 = acc_ref[2 * h]
            a1 = acc_ref[2 * h + 1]
            o = a0[:, :LANES] / a0[:, LANES:] - lam * (a1[:, :LANES] / a1[:, LANES:])
            y = o * lax.rsqrt(jnp.mean(o * o, axis=-1, keepdims=True) + EPS) * g_ref[...]
            o_ref[0, pl.ds(q0, bq), h * LANES:(h + 1) * LANES] = (y * (1.0 - lam_init)).astype(o_ref.dtype)
        return carry

    lax.fori_loop(0, tq // bq, q_block, 0)


def _df_attention(q, k_pre, v_pre, k_self, v_self, lq1, lk1, lq2, lk2, subln_g, lam_init):
    b, tq, w = q.shape
    n_pre = k_pre.shape[1]
    bq = min(256, tq)
    bp = min(256, n_pre)
    pre_map = (lambda i: (i, 0, 0)) if k_pre.shape[0] == b else (lambda i: (0, 0, 0))
    bat = lambda i: (i, 0, 0)
    fixed = lambda i: (0, 0)
    kern = functools.partial(_df_kernel, tq=tq, n_pre=n_pre, bq=bq, bp=bp, lam_init=lam_init)
    vec = pl.BlockSpec((1, HEAD_DIM), fixed)
    return pl.pallas_call(
        kern,
        grid=(b,),
        in_specs=[pl.BlockSpec((1, tq, w), bat),
                  pl.BlockSpec((1, n_pre, w), pre_map), pl.BlockSpec((1, n_pre, w), pre_map),
                  pl.BlockSpec((1, tq, w), bat), pl.BlockSpec((1, tq, w), bat),
                  vec, vec, vec, vec, pl.BlockSpec((1, LANES), fixed)],
        out_specs=pl.BlockSpec((1, tq, w), bat),
        out_shape=jax.ShapeDtypeStruct((b, tq, w), BF16),
        scratch_shapes=[pltpu.VMEM((n_pre, 2 * w), BF16), pltpu.VMEM((tq, 2 * w), BF16),
                        pltpu.VMEM((2 * DF_HEADS, bq, LANES), BF16), pltpu.VMEM((2 * DF_HEADS, bq, LANES), F32),
                        pltpu.VMEM((2 * DF_HEADS, bq, 2 * LANES), F32)],
        compiler_params=_cparams(("parallel",)),
        name="df_attention",
    )(q, k_pre, v_pre, k_self, v_self, lq1, lk1, lq2, lk2, subln_g)


def _outproj_kernel(x_ref, sb_ref, df_ref, wsb_ref, wdf_ref, g_ref, wr_ref, br_ref,
                    h_ref, xn_ref, route_ref):
    h = (x_ref[...]
         + jnp.dot(sb_ref[...], wsb_ref[...], preferred_element_type=F32)
         + jnp.dot(df_ref[...], wdf_ref[...], preferred_element_type=F32))
    h_ref[...] = h
    xn = h * lax.rsqrt(jnp.mean(h * h, axis=-1, keepdims=True) + EPS) * g_ref[...]
    xn_ref[...] = xn
    logits = jnp.dot(xn.astype(BF16), wr_ref[...], preferred_element_type=F32) + br_ref[...]
    lane = lax.broadcasted_iota(jnp.int32, logits.shape, 1)
    big = jnp.int32(LANES)
    neg = -jnp.inf

    def first_argmax(v):
        mx = jnp.max(v, axis=1, keepdims=True)
        idx = jnp.min(jnp.where(v == mx, lane, big), axis=1, keepdims=True)
        return mx, idx

    gl = jnp.where((lane >= N_EXPERTS) & (lane < N_EXPERTS + N_GROUPS), logits, neg)
    gmax, gidx = first_argmax(gl)
    g_w = 1.0 / jnp.sum(jnp.exp(gl - gmax), axis=1, keepdims=True)
    grp = gidx - N_EXPERTS
    el = jnp.where((lane < N_EXPERTS) & (lane // EXPERTS_PER_GROUP == grp), logits, neg)
    m1, i1 = first_argmax(el)
    m2, i2 = first_argmax(jnp.where(lane == i1, neg, el))
    e21 = jnp.exp(m2 - m1)
    t1 = 1.0 / (1.0 + e21)
    t2 = e21 / (1.0 + e21)
    route = jnp.where(lane == 0, i1.astype(F32),
            jnp.where(lane == 1, i2.astype(F32),
            jnp.where(lane == 2, g_w * t1,
            jnp.where(lane == 3, g_w * t2, 0.0))))
    route_ref[...] = route


def _outproj_route(x2d, sb, df, wsb, wdf, g, wr, br, tm):
    n, d = x2d.shape
    row = lambda i: (i, 0)
    fixed = lambda i: (0, 0)
    return pl.pallas_call(
        _outproj_kernel,
        grid=(n // tm,),
        in_specs=[pl.BlockSpec((tm, d), row), pl.BlockSpec((tm, GROUP_W), row), pl.BlockSpec((tm, GROUP_W), row),
                  pl.BlockSpec(wsb.shape, fixed), pl.BlockSpec(wdf.shape, fixed), pl.BlockSpec((1, d), fixed),
                  pl.BlockSpec(wr.shape, fixed), pl.BlockSpec((1, LANES), fixed)],
        out_specs=[pl.BlockSpec((tm, d), row), pl.BlockSpec((tm, d), row), pl.BlockSpec((tm, LANES), row)],
        out_shape=[jax.ShapeDtypeStruct((n, d), F32), jax.ShapeDtypeStruct((n, d), F32),
                   jax.ShapeDtypeStruct((n, LANES), F32)],
        compiler_params=_cparams(("parallel",)),
        name="outproj_route",
    )(x2d, sb, df, wsb, wdf, g, wr, br)


def _row_copy(src, s, dst, d, sem):
    return pltpu.make_async_copy(src.at[pl.ds(s, 1)], dst.at[pl.ds(d, 1)], sem)


def _dispatch_kernel(d0_ref, d1_ref, x_ref, xs_in_hbm, xs_hbm, sem, *, tm):
    del xs_in_hbm

    def issue(r, c):
        _row_copy(x_ref, r, xs_hbm, d0_ref[0, 0, r], sem).start()
        _row_copy(x_ref, r, xs_hbm, d1_ref[0, 0, r], sem).start()
        return c

    lax.fori_loop(0, tm, issue, 0, unroll=8)
    for _ in range(2):
        pltpu.make_async_copy(x_ref, xs_hbm.at[pl.ds(0, tm)], sem).wait()


def _dispatch(dest0, dest1, x2d, xs, tm):
    n, d = x2d.shape
    idx = lambda i: (i, 0, 0)
    smem = functools.partial(pl.BlockSpec, memory_space=pltpu.SMEM)
    any_spec = pl.BlockSpec(memory_space=pl.ANY)
    return pl.pallas_call(
        functools.partial(_dispatch_kernel, tm=tm),
        grid=(n // tm,),
        in_specs=[smem((1, 1, tm), idx), smem((1, 1, tm), idx),
                  pl.BlockSpec((tm, d), lambda i: (i, 0)), any_spec],
        out_specs=any_spec,
        out_shape=jax.ShapeDtypeStruct(xs.shape, xs.dtype),
        scratch_shapes=[pltpu.SemaphoreType.DMA(())],
        input_output_aliases={3: 0},
        compiler_params=_cparams(("arbitrary",)),
        name="moe_dispatch",
    )(dest0.reshape(n // tm, 1, tm), dest1.reshape(n // tm, 1, tm), x2d, xs)


def _expert_kernel(te_ref, nt_ref, x_ref, wgu_ref, wd_ref, y_ref, *, de):
    del te_ref
    active = pl.program_id(0) < nt_ref[0]

    @pl.when(jnp.logical_not(active))
    def _():
        y_ref[...] = jnp.zeros_like(y_ref)

    @pl.when(active)
    def _():
        x = x_ref[...].astype(BF16)
        gu = jnp.dot(x, wgu_ref[0], preferred_element_type=F32)
        g = gu[:, :de]
        h = (g * (1.0 / (1.0 + jnp.exp(-g)))) * gu[:, de:]
        y_ref[...] = jnp.dot(h.astype(BF16), wd_ref[0], preferred_element_type=F32)


def _experts(tile_expert, n_tiles, xs, wgu, wd, tm):
    p, d = xs.shape
    de = wd.shape[1]
    last = lambda i, te, nt: jnp.minimum(i, nt[0] - 1)
    grid_spec = pltpu.PrefetchScalarGridSpec(
        num_scalar_prefetch=2,
        grid=(p // tm,),
        in_specs=[pl.BlockSpec((tm, d), lambda i, te, nt: (last(i, te, nt), 0)),
                  pl.BlockSpec((1, d, 2 * de), lambda i, te, nt: (te[last(i, te, nt)], 0, 0)),
                  pl.BlockSpec((1, de, d), lambda i, te, nt: (te[last(i, te, nt)], 0, 0))],
        out_specs=pl.BlockSpec((tm, d), lambda i, te, nt: (i, 0)),
    )
    return pl.pallas_call(
        functools.partial(_expert_kernel, de=de),
        grid_spec=grid_spec,
        out_shape=jax.ShapeDtypeStruct((p, d), F32),
        compiler_params=_cparams(("arbitrary",)),
        name="moe_experts",
    )(tile_expert, n_tiles, xs, wgu, wd)


def _combine_kernel(d0_ref, d1_ref, h_ref, gate_ref, g_ref, ys_hbm, o_ref, y0_ref, y1_ref, sem, *, tm):
    def issue(r, c):
        _row_copy(ys_hbm, d0_ref[0, 0, r], y0_ref, r, sem).start()
        _row_copy(ys_hbm, d1_ref[0, 0, r], y1_ref, r, sem).start()
        return c

    lax.fori_loop(0, tm, issue, 0, unroll=8)
    pltpu.make_async_copy(ys_hbm.at[pl.ds(0, tm)], y0_ref, sem).wait()
    pltpu.make_async_copy(ys_hbm.at[pl.ds(0, tm)], y1_ref, sem).wait()
    gate = gate_ref[...]
    h = h_ref[...] + (y0_ref[...] * gate[:, 2:3] + y1_ref[...] * gate[:, 3:4])
    o_ref[...] = h * lax.rsqrt(jnp.mean(h * h, axis=-1, keepdims=True) + EPS) * g_ref[...]


def _combine(dest0, dest1, h2d, route, g, ys, tm):
    n, d = h2d.shape
    idx = lambda i: (i, 0, 0)
    row = lambda i: (i, 0)
    smem = functools.partial(pl.BlockSpec, memory_space=pltpu.SMEM)
    return pl.pallas_call(
        functools.partial(_combine_kernel, tm=tm),
        grid=(n // tm,),
        in_specs=[smem((1, 1, tm), idx), smem((1, 1, tm), idx),
                  pl.BlockSpec((tm, d), row), pl.BlockSpec((tm, LANES), row),
                  pl.BlockSpec((1, d), lambda i: (0, 0)), pl.BlockSpec(memory_space=pl.ANY)],
        out_specs=pl.BlockSpec((tm, d), row),
        out_shape=jax.ShapeDtypeStruct((n, d), F32),
        scratch_shapes=[pltpu.VMEM((tm, d), F32), pltpu.VMEM((tm, d), F32), pltpu.SemaphoreType.DMA(())],
        compiler_params=_cparams(("arbitrary",)),
        name="moe_combine",
    )(dest0.reshape(n // tm, 1, tm), dest1.reshape(n // tm, 1, tm), h2d, route, g, ys)


def _routing_plan(route, tm):
    t = route.shape[0]
    e = route[:, :2].astype(jnp.int32).reshape(-1)
    onehot = (e[:, None] == jnp.arange(N_EXPERTS, dtype=jnp.int32)[None, :])
    blk = 512
    oh = onehot.astype(BF16).reshape(2 * t // blk, blk, N_EXPERTS)
    tri = (jnp.arange(blk)[:, None] > jnp.arange(blk)[None, :]).astype(BF16)
    within = jnp.einsum("ij,bjk->bik", tri, oh, preferred_element_type=F32)
    totals = jnp.sum(oh.astype(F32), axis=1)
    before = jnp.cumsum(totals, axis=0) - totals
    rank_all = (within + before[:, None, :]).reshape(2 * t, N_EXPERTS)
    rank = jnp.sum(jnp.where(onehot, rank_all, 0.0), axis=1).astype(jnp.int32)
    counts = jnp.sum(totals, axis=0).astype(jnp.int32)
    padded = ((counts + tm - 1) // tm) * tm
    ends = jnp.cumsum(padded)
    starts = ends - padded
    dest = (starts[e] + rank).reshape(t, 2)
    n_slots = 2 * t + N_EXPERTS * tm
    tile_start = jnp.arange(n_slots // tm, dtype=jnp.int32) * tm
    tile_expert = jnp.minimum(jnp.sum(tile_start[:, None] >= ends[None, :], axis=1), N_EXPERTS - 1)
    n_tiles = (ends[-1] // tm).reshape(1)
    return dest[:, 0], dest[:, 1], tile_expert.astype(jnp.int32), n_tiles.astype(jnp.int32), n_slots


def _pick(n, prefs):
    for p in prefs:
        if n % p == 0:
            return p
    return n


def kernel(x_prompt, x_sample, cache_sb_k, cache_sb_v, cache_diff_k, cache_diff_v, meta_tokens, norm_mix_g, w_in, lambda_q1, lambda_k1, lambda_q2, lambda_k2, subln_g, w_out, norm_ffn_g, w_group, b_group, w_router, b_router, w_gate, w_up, w_down, final_norm_g):
    depth = w_in.shape[0]
    assert depth == 1, "meta-token rows are only dropped after a single layer"
    b, s, d = x_prompt.shape
    bs, ss, _ = x_sample.shape
    past = cache_sb_k.shape[2]
    n_meta = meta_tokens.shape[0]
    lam_init = 0.8 - 0.6 * math.exp(-0.3 * 0)

    g_mix = norm_mix_g[0].reshape(1, d)
    w_in_bf = w_in[0].astype(BF16)
    pos_p = n_meta + jnp.arange(s, dtype=jnp.int32)
    pos_s = past + jnp.arange(ss, dtype=jnp.int32)
    pos_m = jnp.arange(n_meta, dtype=jnp.int32)
    xp2 = x_prompt.reshape(b * s, d)
    xs2 = x_sample.reshape(bs * ss, d)

    tm_p = _pick(s, (512, 256, 128))
    (qsb_p, qdf_p, ksb_p, vsb_p, kdf_p, vdf_p,
     ksb_pb, vsb_pb, kdf_pb, vdf_pb) = _inproj(xp2, g_mix, w_in_bf, pos_p, tm_p)
    tm_s = _pick(bs * ss, (512, 256, 128))
    (qsb_s, qdf_s, ksb_s, vsb_s, kdf_s, vdf_s,
     ksb_sb, vsb_sb, kdf_sb, vdf_sb) = _inproj(xs2, g_mix, w_in_bf, pos_s, tm_s)
    (_, _, ksb_m, vsb_m, kdf_m, vdf_m,
     ksb_mb, vsb_mb, kdf_mb, vdf_mb) = _inproj(meta_tokens, g_mix, w_in_bf, pos_m, n_meta)

    r3 = lambda a, bb, t: a.reshape(bb, t, GROUP_W)
    lam_vecs = [v[0].reshape(1, HEAD_DIM) for v in (lambda_q1, lambda_k1, lambda_q2, lambda_k2)]
    sub_g = subln_g[0].reshape(1, LANES)

    sb_p = _sb_attention(r3(qsb_p, b, s), ksb_mb[None], vsb_mb[None], r3(ksb_pb, b, s), r3(vsb_pb, b, s))
    df_p = _df_attention(r3(qdf_p, b, s), kdf_mb[None], vdf_mb[None], r3(kdf_pb, b, s), r3(vdf_pb, b, s),
                         *lam_vecs, sub_g, lam_init)
    sb_s = _sb_attention(r3(qsb_s, bs, ss), cache_sb_k[0].reshape(bs, past, GROUP_W),
                         cache_sb_v[0].reshape(bs, past, GROUP_W), r3(ksb_sb, bs, ss), r3(vsb_sb, bs, ss))
    df_s = _df_attention(r3(qdf_s, bs, ss), cache_diff_k[0].reshape(bs, past, GROUP_W),
                         cache_diff_v[0].reshape(bs, past, GROUP_W), r3(kdf_sb, bs, ss), r3(vdf_sb, bs, ss),
                         *lam_vecs, sub_g, lam_init)

    w_out_bf = w_out[0].astype(BF16)
    wsb, wdf = w_out_bf[:GROUP_W], w_out_bf[GROUP_W:]
    g_ffn = norm_ffn_g[0].reshape(1, d)
    pad_cols = LANES - N_EXPERTS - N_GROUPS
    wr = jnp.concatenate([w_router[0], w_group[0], jnp.zeros((d, pad_cols), F32)], axis=1).astype(BF16)
    br = jnp.concatenate([b_router[0], b_group[0], jnp.zeros((pad_cols,), F32)]).reshape(1, LANES)

    h_p, xn_p, route_p = _outproj_route(xp2, sb_p.reshape(b * s, GROUP_W), df_p.reshape(b * s, GROUP_W),
                                        wsb, wdf, g_ffn, wr, br, tm_p)
    h_s, xn_s, route_s = _outproj_route(xs2, sb_s.reshape(bs * ss, GROUP_W), df_s.reshape(bs * ss, GROUP_W),
                                        wsb, wdf, g_ffn, wr, br, tm_s)

    tm_e = 512
    n_p = b * s
    tm_d = _pick(math.gcd(n_p, bs * ss), (512, 256, 128))
    route = jnp.concatenate([route_p, route_s], axis=0)
    dest0, dest1, tile_expert, n_tiles, n_slots = _routing_plan(route, tm_e)
    xsorted = jnp.zeros((n_slots, d), F32)
    xsorted = _dispatch(dest0[:n_p], dest1[:n_p], xn_p, xsorted, tm_d)
    xsorted = _dispatch(dest0[n_p:], dest1[n_p:], xn_s, xsorted, tm_d)
    wgu = jnp.concatenate([w_gate[0], w_up[0]], axis=-1).astype(BF16)
    ysorted = _experts(tile_expert, n_tiles, xsorted, wgu, w_down[0].astype(BF16), tm_e)
    g_fin = final_norm_g.reshape(1, d)
    y_p = _combine(dest0[:n_p], dest1[:n_p], h_p, route_p, g_fin, ysorted, tm_d)
    y_s = _combine(dest0[n_p:], dest1[n_p:], h_s, route_s, g_fin, ysorted, tm_d)

    def with_meta(meta_rows, frames, heads, hd):
        m = jnp.broadcast_to(meta_rows[None], (b, n_meta, GROUP_W))
        return jnp.concatenate([m, frames.reshape(b, s, GROUP_W)], axis=1).reshape(1, b, n_meta + s, heads, hd)

    shp = lambda a, heads, hd: a.reshape(1, bs, ss, heads, hd)
    return (y_p.reshape(b, s, d), y_s.reshape(bs, ss, d),
            with_meta(ksb_m, ksb_p, SB_HEADS, HEAD_DIM), with_meta(vsb_m, vsb_p, SB_HEADS, HEAD_DIM),
            with_meta(kdf_m, kdf_p, 2 * DF_HEADS, HEAD_DIM), with_meta(vdf_m, vdf_p, DF_HEADS, 2 * HEAD_DIM),
            shp(ksb_s, SB_HEADS, HEAD_DIM), shp(vsb_s, SB_HEADS, HEAD_DIM),
            shp(kdf_s, 2 * DF_HEADS, HEAD_DIM), shp(vdf_s, DF_HEADS, 2 * HEAD_DIM))
```

```python
import functools
import math

import jax
import jax.numpy as jnp
from jax import lax
from jax.experimental import pallas as pl
from jax.experimental.pallas import tpu as pltpu

F32 = jnp.float32
BF16 = jnp.bfloat16

CHUNK = 64
SB_HEADS = 8
DF_HEADS = 4
HEAD_DIM = 64
GROUP_W = 512
ROPE_THETA = 500000.0
ROT_DIM = 16
N_GROUPS = 4
EXPERTS_PER_GROUP = 8
N_EXPERTS = N_GROUPS * EXPERTS_PER_GROUP
EPS = 1e-6
LANES = 128
SB_CUTOFF = -88.0
VMEM_LIMIT = 56 * 1024 * 1024


def _cparams(sem):
    return pltpu.CompilerParams(dimension_semantics=sem, vmem_limit_bytes=VMEM_LIMIT)


def _inproj_kernel(x_ref, g_ref, w_ref, cos_ref, sin_ref,
                   qsb_ref, qdf_ref, ksb_ref, vsb_ref, kdf_ref, vdf_ref,
                   ksb_bf_ref, vsb_bf_ref, kdf_bf_ref, vdf_bf_ref, *, feature_major):
    x = x_ref[...]
    ms = jnp.mean(x * x, axis=-1, keepdims=True)
    xn = (x * lax.rsqrt(ms + EPS) * g_ref[...]).astype(BF16)
    proj = jnp.dot(xn, w_ref[...], preferred_element_type=F32)
    cos = cos_ref[...]
    sin = sin_ref[...]
    lane = lax.broadcasted_iota(jnp.int32, cos.shape, 1)
    first_half = (lane % ROT_DIM) < (ROT_DIM // 2)

    def rope(a):
        cols = []
        for j in range(GROUP_W // LANES):
            aj = a[:, j * LANES:(j + 1) * LANES]
            partner = jnp.where(first_half,
                                pltpu.roll(aj, LANES - ROT_DIM // 2, 1),
                                pltpu.roll(aj, ROT_DIM // 2, 1))
            cols.append(aj * cos + partner * sin)
        return jnp.concatenate(cols, axis=1)

    scale = HEAD_DIM ** -0.5
    qsb_ref[...] = (proj[:, 0:GROUP_W] * scale).astype(BF16)
    qdf_ref[...] = (rope(proj[:, 3 * GROUP_W:4 * GROUP_W]) * scale).astype(BF16)
    kdf = rope(proj[:, 4 * GROUP_W:5 * GROUP_W])
    for f32_ref, bf_ref, val in ((ksb_ref, ksb_bf_ref, proj[:, GROUP_W:2 * GROUP_W]),
                                 (vsb_ref, vsb_bf_ref, proj[:, 2 * GROUP_W:3 * GROUP_W]),
                                 (kdf_ref, kdf_bf_ref, kdf),
                                 (vdf_ref, vdf_bf_ref, proj[:, 5 * GROUP_W:6 * GROUP_W])):
        if feature_major and f32_ref is not vdf_ref:
            f32_ref[0] = val.T
        else:
            f32_ref[...] = val
        bf_ref[...] = val.astype(BF16)


def _rope_tables(pos):
    half = ROT_DIM // 2
    inv = ROPE_THETA ** (-jnp.arange(half, dtype=F32) * 2.0 / ROT_DIM)
    ang = pos.astype(F32)[:, None] * inv[None, :]
    cos, sin = jnp.cos(ang), jnp.sin(ang)
    n = pos.shape[0]
    pad = HEAD_DIM - ROT_DIM
    cos_h = jnp.concatenate([cos, cos, jnp.ones((n, pad), F32)], axis=1)
    sin_h = jnp.concatenate([-sin, sin, jnp.zeros((n, pad), F32)], axis=1)
    reps = LANES // HEAD_DIM
    return jnp.tile(cos_h, (1, reps)), jnp.tile(sin_h, (1, reps))


def _inproj(x2d, g, w_bf, pos, tm, feature_major=False):
    n, d = x2d.shape
    cos_t, sin_t = _rope_tables(pos)
    period = pos.shape[0]
    if period < tm:
        cos_t = jnp.tile(cos_t, (tm // period, 1))
        sin_t = jnp.tile(sin_t, (tm // period, 1))
    ntab = cos_t.shape[0] // tm
    row = lambda i: (i, 0)
    fixed = lambda i: (0, 0)
    tab = lambda i: (i % ntab, 0)
    out_spec = pl.BlockSpec((tm, GROUP_W), row)
    f32_spec, f32_shape = out_spec, jax.ShapeDtypeStruct((n, GROUP_W), F32)
    if feature_major:
        f32_spec = pl.BlockSpec((1, GROUP_W, tm), lambda i: (i // ntab, 0, i % ntab))
        f32_shape = jax.ShapeDtypeStruct((n // period, GROUP_W, period), F32)
    return pl.pallas_call(
        functools.partial(_inproj_kernel, feature_major=feature_major),
        grid=(n // tm,),
        in_specs=[pl.BlockSpec((tm, d), row), pl.BlockSpec((1, d), fixed),
                  pl.BlockSpec(w_bf.shape, fixed),
                  pl.BlockSpec((tm, LANES), tab), pl.BlockSpec((tm, LANES), tab)],
        out_specs=[out_spec] * 2 + [f32_spec] * 3 + [out_spec] * 5,
        out_shape=[jax.ShapeDtypeStruct((n, GROUP_W), BF16)] * 2
                  + [f32_shape] * 3 + [jax.ShapeDtypeStruct((n, GROUP_W), F32)]
                  + [jax.ShapeDtypeStruct((n, GROUP_W), BF16)] * 4,
        compiler_params=_cparams(("parallel",)),
        name="inproj",
    )(x2d, g, w_bf, cos_t, sin_t)


def _sb_kernel(q_ref, kp_ref, vp_ref, ks_ref, vs_ref, o_ref, acc_ref, car_ref, *cache_scratch,
               tq, n_pre, bq, bp, cached):
    lane = lax.broadcasted_iota(jnp.int32, (bq, LANES), 1)
    lo = lane < HEAD_DIM
    contract_last = (((1,), (1,)), ((), ()))
    wn = 2 * bq if tq >= 2 * bq else bq
    step = pl.program_id(0)

    def neg_suffix_matrix(n):
        j = lax.broadcasted_iota(jnp.int32, (n, n + LANES), 0)
        s = lax.broadcasted_iota(jnp.int32, (n, n + LANES), 1)
        return jnp.where((j > s) | (s >= n), -1.0, 0.0).astype(BF16)

    suffixes = {n: neg_suffix_matrix(n) for n in {wn, bq, bp}}

    def row_major(k_ref, v_ref, k0, n):
        def load(hp):
            cols = slice(hp * LANES, (hp + 1) * LANES)
            return k_ref[0, pl.ds(k0, n), cols].astype(BF16), v_ref[0, pl.ds(k0, n), cols].astype(BF16)
        return load

    def sweep(load, feature_major, q0, koff, n, causal, first):
        suffix = suffixes[n]
        if causal:
            r = lax.broadcasted_iota(jnp.int32, (bq, n), 0)
            c = lax.broadcasted_iota(jnp.int32, (bq, n), 1)
            vis = (c + koff) < r
        worst = None
        for hp in range(SB_HEADS // 2):
            cols = slice(hp * LANES, (hp + 1) * LANES)
            k2, v2 = load(hp)
            q2 = q_ref[0, pl.ds(q0, bq), cols]
            pv = []
            for m in range(2):
                qm = jnp.where(lo if m == 0 else ~lo, q2, jnp.zeros_like(q2))
                if feature_major:
                    s = jnp.dot(qm, k2, preferred_element_type=F32)
                else:
                    s = lax.dot_general(qm, k2, contract_last, preferred_element_type=F32)
                sp = jnp.maximum(s, 0.0) + jnp.log(1.0 + jnp.exp(-jnp.abs(s)))
                spm = jnp.where(vis, sp, 0.0) if causal else sp
                cr = jnp.dot(spm.astype(BF16), suffix, preferred_element_type=F32)
                between = cr[:, :n]
                car = cr[:, n:]
                if not first:
                    old = car_ref[2 * hp + m]
                    between = between + (jnp.tile(old, (1, n // LANES)) if n % LANES == 0 else old[:, :n])
                    car = car + old
                p = jnp.exp(s - sp + between)
                if causal:
                    p = jnp.where(vis, p, 0.0)
                if feature_major:
                    pv.append(lax.dot_general(p.astype(BF16), v2, contract_last, preferred_element_type=F32))
                else:
                    pv.append(jnp.dot(p.astype(BF16), v2, preferred_element_type=F32))
                car_ref[2 * hp + m] = car
                worst = car if worst is None else jnp.maximum(worst, car)
            both = jnp.where(lo, pv[0], pv[1])
            if first:
                acc_ref[:, cols] = both
            else:
                acc_ref[:, cols] += both
        return jnp.max(worst) > SB_CUTOFF

    if cached:
        kwin_ref, vwin_ref, kblk_ref, vblk_ref, win_sem, blk_sem = cache_scratch
        pw = min(2 * bp, n_pre)

        def window_copies(bi, slot):
            return [pltpu.make_async_copy(src.at[bi, :, pl.ds(n_pre - pw, pw)], dst.at[slot], win_sem.at[a, slot])
                    for a, (src, dst) in enumerate(((kp_ref, kwin_ref), (vp_ref, vwin_ref)))]

        def block_copies(k0):
            return [pltpu.make_async_copy(src.at[step, :, pl.ds(k0, bp)], dst, blk_sem.at[a])
                    for a, (src, dst) in enumerate(((kp_ref, kblk_ref), (vp_ref, vblk_ref)))]

        slot = step % 2

        @pl.when(step == 0)
        def _():
            for cp in window_copies(step, slot):
                cp.start()

        @pl.when(step + 1 < pl.num_programs(0))
        def _():
            for cp in window_copies(step + 1, 1 - slot):
                cp.start()

        for cp in window_copies(step, slot):
            cp.wait()

    def cond(st):
        return (st[0] >= 0) & st[1]

    def q_block(qi, carry):
        q0 = pl.multiple_of(qi * bq, bq)
        first_blk = jnp.maximum(qi + 1 - wn // bq, 0)
        k0 = pl.multiple_of(first_blk * bq, bq)
        go = sweep(row_major(ks_ref, vs_ref, k0, wn), False, q0, k0 - q0, wn, True, True)

        def self_body(st):
            kj = pl.multiple_of(st[0] * bq, bq)
            return st[0] - 1, sweep(row_major(ks_ref, vs_ref, kj, bq), False, q0, 0, bq, False, False)

        _, go = lax.while_loop(cond, self_body, (first_blk - 1, go))

        if not cached:
            def pre_body(st):
                kj = pl.multiple_of(st[0] * bp, bp)
                return st[0] - 1, sweep(row_major(kp_ref, vp_ref, kj, bp), False, q0, 0, bp, False, False)

            lax.while_loop(cond, pre_body, (jnp.int32(n_pre // bp - 1), go))
        else:
            def window_block(off):
                def load(hp):
                    rows = slice(hp * LANES, (hp + 1) * LANES)
                    return (kwin_ref[slot, rows, off:off + bp].astype(BF16),
                            vwin_ref[slot, rows, off:off + bp].astype(BF16))
                return load

            for off in range(pw - bp, -1, -bp):
                go = lax.cond(go, lambda off=off: sweep(window_block(off), True, q0, 0, bp, False, False),
                              lambda: jnp.zeros((), jnp.bool_))

            def older_body(st):
                for cp in block_copies(pl.multiple_of(st[0] * bp, bp)):
                    cp.start()
                for cp in block_copies(pl.multiple_of(st[0] * bp, bp)):
                    cp.wait()

                def load(hp):
                    rows = slice(hp * LANES, (hp + 1) * LANES)
                    return kblk_ref[rows, :].astype(BF16), vblk_ref[rows, :].astype(BF16)

                return st[0] - 1, sweep(load, True, q0, 0, bp, False, False)

            lax.while_loop(cond, older_body, (jnp.int32((n_pre - pw) // bp - 1), go))
        o_ref[0, pl.ds(q0, bq), :] = acc_ref[...].astype(o_ref.dtype)
        return carry

    lax.fori_loop(0, tq // bq, q_block, 0)


def _sb_attention(q, k_pre, v_pre, k_self, v_self, cached=False):
    b, tq, w = q.shape
    n_pre = k_pre.shape[2] if cached else k_pre.shape[1]
    bq = min(128, tq)
    bp = min(128, n_pre)
    bat = lambda i: (i, 0, 0)
    scratch = [pltpu.VMEM((bq, w), F32), pltpu.VMEM((SB_HEADS, bq, LANES), F32)]
    if cached:
        pre_spec = pl.BlockSpec(memory_space=pl.ANY)
        pw = min(2 * bp, n_pre)
        scratch += [pltpu.VMEM((2, w, pw), F32), pltpu.VMEM((2, w, pw), F32),
                    pltpu.VMEM((w, bp), F32), pltpu.VMEM((w, bp), F32),
                    pltpu.SemaphoreType.DMA((2, 2)), pltpu.SemaphoreType.DMA((2,))]
    else:
        pre_map = (lambda i: (i, 0, 0)) if k_pre.shape[0] == b else (lambda i: (0, 0, 0))
        pre_spec = pl.BlockSpec((1, n_pre, w), pre_map)
    kern = functools.partial(_sb_kernel, tq=tq, n_pre=n_pre, bq=bq, bp=bp, cached=cached)
    return pl.pallas_call(
        kern,
        grid=(b,),
        in_specs=[pl.BlockSpec((1, tq, w), bat), pre_spec, pre_spec,
                  pl.BlockSpec((1, tq, w), bat), pl.BlockSpec((1, tq, w), bat)],
        out_specs=pl.BlockSpec((1, tq, w), bat),
        out_shape=jax.ShapeDtypeStruct((b, tq, w), BF16),
        scratch_shapes=scratch,
        compiler_params=_cparams(("arbitrary",) if cached else ("parallel",)),
        name="sb_attention",
    )(q, k_pre, v_pre, k_self, v_self)


def _df_kernel(q_ref, kp_ref, vp_ref, ks_ref, vs_ref, lq1_ref, lk1_ref, lq2_ref, lk2_ref, g_ref,
               o_ref, vxp_ref, vxs_ref, qm_ref, mx_ref, acc_ref, *, tq, n_pre, bq, bp, lam_init, cached):
    n_maps = 2 * DF_HEADS
    lane = lax.broadcasted_iota(jnp.int32, (bq, LANES), 1)
    lo = lane < HEAD_DIM
    contract_last = (((1,), (1,)), ((), ()))
    lam = (jnp.exp(jnp.sum(lq1_ref[...] * lk1_ref[...], axis=-1, keepdims=True))
           - jnp.exp(jnp.sum(lq2_ref[...] * lk2_ref[...], axis=-1, keepdims=True)) + lam_init)

    def extend_values(v_ref, vx_ref, head_rows):
        n = vx_ref.shape[0]
        ones = jnp.ones((n, LANES), BF16)
        for h in range(DF_HEADS):
            if head_rows:
                v_h = v_ref[0, pl.ds(h, n, stride=DF_HEADS), :]
            else:
                v_h = v_ref[0, :, h * LANES:(h + 1) * LANES]
            vx_ref[:, 2 * h * LANES:(2 * h + 1) * LANES] = v_h.astype(BF16)
            vx_ref[:, (2 * h + 1) * LANES:(2 * h + 2) * LANES] = ones

    extend_values(vp_ref, vxp_ref, cached)
    extend_values(vs_ref, vxs_ref, False)

    def scores(c, k_ref, k0, n, chunk_mask):
        h = c // 2
        if cached and k_ref is kp_ref:
            k2 = k_ref[0, h * LANES:(h + 1) * LANES, pl.ds(k0, n)].astype(BF16)
            s = jnp.dot(qm_ref[c], k2, preferred_element_type=F32)
        else:
            k2 = k_ref[0, pl.ds(k0, n), h * LANES:(h + 1) * LANES].astype(BF16)
            s = lax.dot_general(qm_ref[c], k2, contract_last, preferred_element_type=F32)
        if chunk_mask:
            r = lax.broadcasted_iota(jnp.int32, (bq, n), 0)
            col = lax.broadcasted_iota(jnp.int32, (bq, n), 1)
            s = jnp.where((col // CHUNK) <= (r // CHUNK), s, -jnp.inf)
        return s

    def max_tile(k_ref, k0, n, chunk_mask):
        for c in range(n_maps):
            s = scores(c, k_ref, k0, n, chunk_mask)
            if n % LANES == 0:
                m = s[:, :LANES]
                for j in range(1, n // LANES):
                    m = jnp.maximum(m, s[:, j * LANES:(j + 1) * LANES])
            else:
                m = jnp.broadcast_to(jnp.max(s, axis=1, keepdims=True), (bq, LANES))
            mx_ref[c] = jnp.maximum(mx_ref[c], m)

    def value_tile(k_ref, vx_ref, k0, n, chunk_mask):
        for c in range(n_maps):
            h = c // 2
            s = scores(c, k_ref, k0, n, chunk_mask)
            row_max = mx_ref[c]
            if n % LANES == 0:
                p = jnp.concatenate([jnp.exp(s[:, j * LANES:(j + 1) * LANES] - row_max)
                                     for j in range(n // LANES)], axis=1)
            else:
                p = jnp.exp(s - row_max[:, :n])
            vx = vx_ref[pl.ds(k0, n), 2 * h * LANES:(2 * h + 2) * LANES]
            acc_ref[c] += jnp.dot(p.astype(BF16), vx, preferred_element_type=F32)

    def q_block(qi, carry):
        q0 = pl.multiple_of(qi * bq, bq)
        for h in range(DF_HEADS):
            q2 = q_ref[0, pl.ds(q0, bq), h * LANES:(h + 1) * LANES]
            zero = jnp.zeros_like(q2)
            qm_ref[2 * h] = jnp.where(lo, q2, zero)
            qm_ref[2 * h + 1] = jnp.where(lo, zero, q2)
        mx_ref[...] = jnp.full_like(mx_ref, -jnp.inf)
        acc_ref[...] = jnp.zeros_like(acc_ref)

        def sweep(pre_fn, self_fn):
            def pre_body(j, c):
                pre_fn(pl.multiple_of(j * bp, bp))
                return c

            def self_body(j, c):
                self_fn(pl.multiple_of(j * bq, bq), False)
                return c

            lax.fori_loop(0, n_pre // bp, pre_body, 0)
            lax.fori_loop(0, qi, self_body, 0)
            self_fn(q0, True)

        sweep(lambda k0: max_tile(kp_ref, k0, bp, False),
              lambda k0, msk: max_tile(ks_ref, k0, bq, msk))
        for c in range(n_maps):
            mx_ref[c] = jnp.broadcast_to(jnp.max(mx_ref[c], axis=1, keepdims=True), (bq, LANES))
        sweep(lambda k0: value_tile(kp_ref, vxp_ref, k0, bp, False),
              lambda k0, msk: value_tile(ks_ref, vxs_ref, k0, bq, msk))

        for h in range(DF_HEADS):
            a0 = acc_ref[2 * h]
            a1 = acc_ref[2 * h + 1]
            o = a0[:, :LANES] / a0[:, LANES:] - lam * (a1[:, :LANES] / a1[:, LANES:])
            y = o * lax.rsqrt(jnp.mean(o * o, axis=-1, keepdims=True) + EPS) * g_ref[...]
            o_ref[0, pl.ds(q0, bq), h * LANES:(h + 1) * LANES] = (y * (1.0 - lam_init)).astype(o_ref.dtype)
        return carry

    lax.fori_loop(0, tq // bq, q_block, 0)


def _df_attention(q, k_pre, v_pre, k_self, v_self, lq1, lk1, lq2, lk2, subln_g, lam_init, cached=False):
    b, tq, w = q.shape
    n_pre = k_pre.shape[2] if cached else k_pre.shape[1]
    bq = min(256, tq)
    bp = min(256, n_pre)
    pre_map = (lambda i: (i, 0, 0)) if k_pre.shape[0] == b else (lambda i: (0, 0, 0))
    bat = lambda i: (i, 0, 0)
    fixed = lambda i: (0, 0)
    kern = functools.partial(_df_kernel, tq=tq, n_pre=n_pre, bq=bq, bp=bp, lam_init=lam_init, cached=cached)
    kp_spec = pl.BlockSpec((1,) + k_pre.shape[1:], pre_map)
    vp_spec = pl.BlockSpec((1,) + v_pre.shape[1:], pre_map)
    vec = pl.BlockSpec((1, HEAD_DIM), fixed)
    return pl.pallas_call(
        kern,
        grid=(b,),
        in_specs=[pl.BlockSpec((1, tq, w), bat),
                  kp_spec, vp_spec,
                  pl.BlockSpec((1, tq, w), bat), pl.BlockSpec((1, tq, w), bat),
                  vec, vec, vec, vec, pl.BlockSpec((1, LANES), fixed)],
        out_specs=pl.BlockSpec((1, tq, w), bat),
        out_shape=jax.ShapeDtypeStruct((b, tq, w), BF16),
        scratch_shapes=[pltpu.VMEM((n_pre, 2 * w), BF16), pltpu.VMEM((tq, 2 * w), BF16),
                        pltpu.VMEM((2 * DF_HEADS, bq, LANES), BF16), pltpu.VMEM((2 * DF_HEADS, bq, LANES), F32),
                        pltpu.VMEM((2 * DF_HEADS, bq, 2 * LANES), F32)],
        compiler_params=_cparams(("parallel",)),
        name="df_attention",
    )(q, k_pre, v_pre, k_self, v_self, lq1, lk1, lq2, lk2, subln_g)


def _outproj_kernel(x_ref, sb_ref, df_ref, wsb_ref, wdf_ref, g_ref, wr_ref, br_ref,
                    h_ref, xn_ref, route_ref):
    h = (x_ref[...]
         + jnp.dot(sb_ref[...], wsb_ref[...], preferred_element_type=F32)
         + jnp.dot(df_ref[...], wdf_ref[...], preferred_element_type=F32))
    h_ref[...] = h
    xn = h * lax.rsqrt(jnp.mean(h * h, axis=-1, keepdims=True) + EPS) * g_ref[...]
    xn_ref[...] = xn
    logits = jnp.dot(xn.astype(BF16), wr_ref[...], preferred_element_type=F32) + br_ref[...]
    lane = lax.broadcasted_iota(jnp.int32, logits.shape, 1)
    big = jnp.int32(LANES)
    neg = -jnp.inf

    def first_argmax(v):
        mx = jnp.max(v, axis=1, keepdims=True)
        idx = jnp.min(jnp.where(v == mx, lane, big), axis=1, keepdims=True)
        return mx, idx

    gl = jnp.where((lane >= N_EXPERTS) & (lane < N_EXPERTS + N_GROUPS), logits, neg)
    gmax, gidx = first_argmax(gl)
    g_w = 1.0 / jnp.sum(jnp.exp(gl - gmax), axis=1, keepdims=True)
    grp = gidx - N_EXPERTS
    el = jnp.where((lane < N_EXPERTS) & (lane // EXPERTS_PER_GROUP == grp), logits, neg)
    m1, i1 = first_argmax(el)
    m2, i2 = first_argmax(jnp.where(lane == i1, neg, el))
    e21 = jnp.exp(m2 - m1)
    t1 = 1.0 / (1.0 + e21)
    t2 = e21 / (1.0 + e21)
    route = jnp.where(lane == 0, i1.astype(F32),
            jnp.where(lane == 1, i2.astype(F32),
            jnp.where(lane == 2, g_w * t1,
            jnp.where(lane == 3, g_w * t2, 0.0))))
    route_ref[...] = route


def _outproj_route(x2d, sb, df, wsb, wdf, g, wr, br, tm):
    n, d = x2d.shape
    row = lambda i: (i, 0)
    fixed = lambda i: (0, 0)
    return pl.pallas_call(
        _outproj_kernel,
        grid=(n // tm,),
        in_specs=[pl.BlockSpec((tm, d), row), pl.BlockSpec((tm, GROUP_W), row), pl.BlockSpec((tm, GROUP_W), row),
                  pl.BlockSpec(wsb.shape, fixed), pl.BlockSpec(wdf.shape, fixed), pl.BlockSpec((1, d), fixed),
                  pl.BlockSpec(wr.shape, fixed), pl.BlockSpec((1, LANES), fixed)],
        out_specs=[pl.BlockSpec((tm, d), row), pl.BlockSpec((tm, d), row), pl.BlockSpec((tm, LANES), row)],
        out_shape=[jax.ShapeDtypeStruct((n, d), F32), jax.ShapeDtypeStruct((n, d), F32),
                   jax.ShapeDtypeStruct((n, LANES), F32)],
        compiler_params=_cparams(("parallel",)),
        name="outproj_route",
    )(x2d, sb, df, wsb, wdf, g, wr, br)


def _row_copy(src, s, dst, d, sem):
    return pltpu.make_async_copy(src.at[pl.ds(s, 1)], dst.at[pl.ds(d, 1)], sem)


def _dispatch_kernel(d0_ref, d1_ref, x_ref, xs_in_hbm, xs_hbm, sem, *, tm):
    del xs_in_hbm

    def issue(r, c):
        _row_copy(x_ref, r, xs_hbm, d0_ref[0, 0, r], sem).start(priority=0)
        _row_copy(x_ref, r, xs_hbm, d1_ref[0, 0, r], sem).start(priority=1)
        return c

    lax.fori_loop(0, tm, issue, 0, unroll=8)
    for _ in range(2):
        pltpu.make_async_copy(x_ref, xs_hbm.at[pl.ds(0, tm)], sem).wait()


def _dispatch(dest0, dest1, x2d, xs, tm):
    n, d = x2d.shape
    idx = lambda i: (i, 0, 0)
    smem = functools.partial(pl.BlockSpec, memory_space=pltpu.SMEM)
    any_spec = pl.BlockSpec(memory_space=pl.ANY)
    return pl.pallas_call(
        functools.partial(_dispatch_kernel, tm=tm),
        grid=(n // tm,),
        in_specs=[smem((1, 1, tm), idx), smem((1, 1, tm), idx),
                  pl.BlockSpec((tm, d), lambda i: (i, 0)), any_spec],
        out_specs=any_spec,
        out_shape=jax.ShapeDtypeStruct(xs.shape, xs.dtype),
        scratch_shapes=[pltpu.SemaphoreType.DMA(())],
        input_output_aliases={3: 0},
        compiler_params=_cparams(("arbitrary",)),
        name="moe_dispatch",
    )(dest0.reshape(n // tm, 1, tm), dest1.reshape(n // tm, 1, tm), x2d, xs)


def _expert_kernel(te_ref, nt_ref, x_ref, wgu_ref, wd_ref, y_ref, *, de):
    del te_ref
    active = pl.program_id(0) < nt_ref[0]

    @pl.when(jnp.logical_not(active))
    def _():
        y_ref[...] = jnp.zeros_like(y_ref)

    @pl.when(active)
    def _():
        x = x_ref[...].astype(BF16)
        gu = jnp.dot(x, wgu_ref[0], preferred_element_type=F32)
        g = gu[:, :de]
        h = (g * (1.0 / (1.0 + jnp.exp(-g)))) * gu[:, de:]
        y_ref[...] = jnp.dot(h.astype(BF16), wd_ref[0], preferred_element_type=F32)


def _experts(tile_expert, n_tiles, xs, wgu, wd, tm):
    p, d = xs.shape
    de = wd.shape[1]
    last = lambda i, te, nt: jnp.minimum(i, nt[0] - 1)
    grid_spec = pltpu.PrefetchScalarGridSpec(
        num_scalar_prefetch=2,
        grid=(p // tm,),
        in_specs=[pl.BlockSpec((tm, d), lambda i, te, nt: (last(i, te, nt), 0)),
                  pl.BlockSpec((1, d, 2 * de), lambda i, te, nt: (te[last(i, te, nt)], 0, 0)),
                  pl.BlockSpec((1, de, d), lambda i, te, nt: (te[last(i, te, nt)], 0, 0))],
        out_specs=pl.BlockSpec((tm, d), lambda i, te, nt: (i, 0)),
    )
    return pl.pallas_call(
        functools.partial(_expert_kernel, de=de),
        grid_spec=grid_spec,
        out_shape=jax.ShapeDtypeStruct((p, d), F32),
        compiler_params=_cparams(("arbitrary",)),
        name="moe_experts",
    )(tile_expert, n_tiles, xs, wgu, wd)


def _combine_kernel(d0_ref, d1_ref, h_ref, gate_ref, g_ref, ys_hbm, o_ref, y0_ref, y1_ref, sem, *, tm):
    def issue(r, c):
        _row_copy(ys_hbm, d0_ref[0, 0, r], y0_ref, r, sem).start(priority=0)
        _row_copy(ys_hbm, d1_ref[0, 0, r], y1_ref, r, sem).start(priority=1)
        return c

    lax.fori_loop(0, tm, issue, 0, unroll=8)
    pltpu.make_async_copy(ys_hbm.at[pl.ds(0, tm)], y0_ref, sem).wait()
    pltpu.make_async_copy(ys_hbm.at[pl.ds(0, tm)], y1_ref, sem).wait()
    gate = gate_ref[...]
    h = h_ref[...] + (y0_ref[...] * gate[:, 2:3] + y1_ref[...] * gate[:, 3:4])
    o_ref[...] = h * lax.rsqrt(jnp.mean(h * h, axis=-1, keepdims=True) + EPS) * g_ref[...]


def _combine(dest0, dest1, h2d, route, g, ys, tm):
    n, d = h2d.shape
    idx = lambda i: (i, 0, 0)
    row = lambda i: (i, 0)
    smem = functools.partial(pl.BlockSpec, memory_space=pltpu.SMEM)
    return pl.pallas_call(
        functools.partial(_combine_kernel, tm=tm),
        grid=(n // tm,),
        in_specs=[smem((1, 1, tm), idx), smem((1, 1, tm), idx),
                  pl.BlockSpec((tm, d), row), pl.BlockSpec((tm, LANES), row),
                  pl.BlockSpec((1, d), lambda i: (0, 0)), pl.BlockSpec(memory_space=pl.ANY)],
        out_specs=pl.BlockSpec((tm, d), row),
        out_shape=jax.ShapeDtypeStruct((n, d), F32),
        scratch_shapes=[pltpu.VMEM((tm, d), F32), pltpu.VMEM((tm, d), F32), pltpu.SemaphoreType.DMA(())],
        compiler_params=_cparams(("arbitrary",)),
        name="moe_combine",
    )(dest0.reshape(n // tm, 1, tm), dest1.reshape(n // tm, 1, tm), h2d, route, g, ys)


def _routing_plan(route, tm):
    t = route.shape[0]
    e = route[:, :2].astype(jnp.int32).reshape(-1)
    onehot = (e[:, None] == jnp.arange(N_EXPERTS, dtype=jnp.int32)[None, :])
    blk = 512
    oh = onehot.astype(BF16).reshape(2 * t // blk, blk, N_EXPERTS)
    tri = (jnp.arange(blk)[:, None] > jnp.arange(blk)[None, :]).astype(BF16)
    within = jnp.einsum("ij,bjk->bik", tri, oh, preferred_element_type=F32)
    totals = jnp.sum(oh.astype(F32), axis=1)
    before = jnp.cumsum(totals, axis=0) - totals
    rank_all = (within + before[:, None, :]).reshape(2 * t, N_EXPERTS)
    rank = jnp.sum(jnp.where(onehot, rank_all, 0.0), axis=1).astype(jnp.int32)
    counts = jnp.sum(totals, axis=0).astype(jnp.int32)
    padded = ((counts + tm - 1) // tm) * tm
    ends = jnp.cumsum(padded)
    starts = ends - padded
    dest = (starts[e] + rank).reshape(t, 2)
    n_slots = 2 * t + N_EXPERTS * tm
    tile_start = jnp.arange(n_slots // tm, dtype=jnp.int32) * tm
    tile_expert = jnp.minimum(jnp.sum(tile_start[:, None] >= ends[None, :], axis=1), N_EXPERTS - 1)
    n_tiles = (ends[-1] // tm).reshape(1)
    return dest[:, 0], dest[:, 1], tile_expert.astype(jnp.int32), n_tiles.astype(jnp.int32), n_slots


def _pick(n, prefs):
    for p in prefs:
        if n % p == 0:
            return p
    return n


def kernel(x_prompt, x_sample, cache_sb_k, cache_sb_v, cache_diff_k, cache_diff_v, meta_tokens, norm_mix_g, w_in, lambda_q1, lambda_k1, lambda_q2, lambda_k2, subln_g, w_out, norm_ffn_g, w_group, b_group, w_router, b_router, w_gate, w_up, w_down, final_norm_g):
    depth = w_in.shape[0]
    assert depth == 1, "meta-token rows are only dropped after a single layer"
    b, s, d = x_prompt.shape
    bs, ss, _ = x_sample.shape
    past = cache_sb_k.shape[2]
    n_meta = meta_tokens.shape[0]
    lam_init = 0.8 - 0.6 * math.exp(-0.3 * 0)

    g_mix = norm_mix_g[0].reshape(1, d)
    w_in_bf = w_in[0].astype(BF16)
    pos_p = n_meta + jnp.arange(s, dtype=jnp.int32)
    pos_s = past + jnp.arange(ss, dtype=jnp.int32)
    pos_m = jnp.arange(n_meta, dtype=jnp.int32)
    xp2 = x_prompt.reshape(b * s, d)
    xs2 = x_sample.reshape(bs * ss, d)

    tm_p = _pick(s, (512, 256, 128))
    (qsb_p, qdf_p, ksb_p, vsb_p, kdf_p, vdf_p,
     ksb_pb, vsb_pb, kdf_pb, vdf_pb) = _inproj(xp2, g_mix, w_in_bf, pos_p, tm_p, feature_major=True)
    tm_s = _pick(bs * ss, (512, 256, 128))
    (qsb_s, qdf_s, ksb_s, vsb_s, kdf_s, vdf_s,
     ksb_sb, vsb_sb, kdf_sb, vdf_sb) = _inproj(xs2, g_mix, w_in_bf, pos_s, tm_s)
    (_, _, ksb_m, vsb_m, kdf_m, vdf_m,
     ksb_mb, vsb_mb, kdf_mb, vdf_mb) = _inproj(meta_tokens, g_mix, w_in_bf, pos_m, n_meta)

    r3 = lambda a, bb, t: a.reshape(bb, t, GROUP_W)
    lam_vecs = [v[0].reshape(1, HEAD_DIM) for v in (lambda_q1, lambda_k1, lambda_q2, lambda_k2)]
    sub_g = subln_g[0].reshape(1, LANES)

    sb_p = _sb_attention(r3(qsb_p, b, s), ksb_mb[None], vsb_mb[None], r3(ksb_pb, b, s), r3(vsb_pb, b, s))
    df_p = _df_attention(r3(qdf_p, b, s), kdf_mb[None], vdf_mb[None], r3(kdf_pb, b, s), r3(vdf_pb, b, s),
                         *lam_vecs, sub_g, lam_init)
    feature_major = lambda c: jnp.transpose(c[0], (0, 2, 3, 1)).reshape(bs, GROUP_W, past)
    sb_s = _sb_attention(r3(qsb_s, bs, ss), feature_major(cache_sb_k), feature_major(cache_sb_v),
                         r3(ksb_sb, bs, ss), r3(vsb_sb, bs, ss), cached=True)
    df_s = _df_attention(r3(qdf_s, bs, ss), feature_major(cache_diff_k),
                         cache_diff_v[0].reshape(bs, past * DF_HEADS, LANES), r3(kdf_sb, bs, ss), r3(vdf_sb, bs, ss),
                         *lam_vecs, sub_g, lam_init, cached=True)

    w_out_bf = w_out[0].astype(BF16)
    wsb, wdf = w_out_bf[:GROUP_W], w_out_bf[GROUP_W:]
    g_ffn = norm_ffn_g[0].reshape(1, d)
    pad_cols = LANES - N_EXPERTS - N_GROUPS
    wr = jnp.concatenate([w_router[0], w_group[0], jnp.zeros((d, pad_cols), F32)], axis=1).astype(BF16)
    br = jnp.concatenate([b_router[0], b_group[0], jnp.zeros((pad_cols,), F32)]).reshape(1, LANES)

    h_p, xn_p, route_p = _outproj_route(xp2, sb_p.reshape(b * s, GROUP_W), df_p.reshape(b * s, GROUP_W),
                                        wsb, wdf, g_ffn, wr, br, tm_p)
    h_s, xn_s, route_s = _outproj_route(xs2, sb_s.reshape(bs * ss, GROUP_W), df_s.reshape(bs * ss, GROUP_W),
                                        wsb, wdf, g_ffn, wr, br, tm_s)

    tm_e = 512
    n_p = b * s
    tm_d = _pick(math.gcd(n_p, bs * ss), (512, 256, 128))
    route = jnp.concatenate([route_p, route_s], axis=0)
    dest0, dest1, tile_expert, n_tiles, n_slots = _routing_plan(route, tm_e)
    xsorted = jnp.zeros((n_slots, d), F32)
    xsorted = _dispatch(dest0[:n_p], dest1[:n_p], xn_p, xsorted, tm_d)
    xsorted = _dispatch(dest0[n_p:], dest1[n_p:], xn_s, xsorted, tm_d)
    wgu = jnp.concatenate([w_gate[0], w_up[0]], axis=-1).astype(BF16)
    ysorted = _experts(tile_expert, n_tiles, xsorted, wgu, w_down[0].astype(BF16), tm_e)
    g_fin = final_norm_g.reshape(1, d)
    y_p = _combine(dest0[:n_p], dest1[:n_p], h_p, route_p, g_fin, ysorted, tm_d)
    y_s = _combine(dest0[n_p:], dest1[n_p:], h_s, route_s, g_fin, ysorted, tm_d)

    def with_meta(meta_rows, frames, heads, hd):
        m = jnp.broadcast_to(meta_rows[None], (b, n_meta, GROUP_W))
        return jnp.concatenate([m, frames.reshape(b, s, GROUP_W)], axis=1).reshape(1, b, n_meta + s, heads, hd)

    def with_meta_t(meta_rows, frames_t, heads, hd):
        m = jnp.broadcast_to(meta_rows.T[None], (b, GROUP_W, n_meta))
        full = jnp.concatenate([m, frames_t], axis=2).reshape(b, heads, hd, n_meta + s)
        return jnp.transpose(full, (0, 3, 1, 2))[None]

    shp = lambda a, heads, hd: a.reshape(1, bs, ss, heads, hd)
    return (y_p.reshape(b, s, d), y_s.reshape(bs, ss, d),
            with_meta_t(ksb_m, ksb_p, SB_HEADS, HEAD_DIM), with_meta_t(vsb_m, vsb_p, SB_HEADS, HEAD_DIM),
            with_meta_t(kdf_m, kdf_p, 2 * DF_HEADS, HEAD_DIM), with_meta(vdf_m, vdf_p, DF_HEADS, 2 * HEAD_DIM),
            shp(ksb_s, SB_HEADS, HEAD_DIM), shp(vsb_s, SB_HEADS, HEAD_DIM),
            shp(kdf_s, 2 * DF_HEADS, HEAD_DIM), shp(vdf_s, DF_HEADS, 2 * HEAD_DIM))
```

```python
import functools
import math

import jax
import jax.numpy as jnp
from jax import lax
from jax.experimental import pallas as pl
from jax.experimental.pallas import tpu as pltpu

F32 = jnp.float32
BF16 = jnp.bfloat16

CHUNK = 64
SB_HEADS = 8
DF_HEADS = 4
HEAD_DIM = 64
GROUP_W = 512
ROPE_THETA = 500000.0
ROT_DIM = 16
N_GROUPS = 4
EXPERTS_PER_GROUP = 8
N_EXPERTS = N_GROUPS * EXPERTS_PER_GROUP
EPS = 1e-6
LANES = 128
SB_CUTOFF = -88.0
VMEM_LIMIT = 56 * 1024 * 1024
SOFTMAX_SHIFT_SLACK = 80.0


def _cparams(sem):
    return pltpu.CompilerParams(dimension_semantics=sem, vmem_limit_bytes=VMEM_LIMIT)


def _inproj_kernel(x_ref, g_ref, w_ref, cos_ref, sin_ref,
                   qsb_ref, qdf_ref, ksb_ref, vsb_ref, kdf_ref, vdf_ref,
                   ksb_bf_ref, vsb_bf_ref, kdf_bf_ref, vdf_bf_ref, *, feature_major):
    x = x_ref[...]
    ms = jnp.mean(x * x, axis=-1, keepdims=True)
    xn = (x * lax.rsqrt(ms + EPS) * g_ref[...]).astype(BF16)
    proj = jnp.dot(xn, w_ref[...], preferred_element_type=F32)
    cos = cos_ref[...]
    sin = sin_ref[...]
    lane = lax.broadcasted_iota(jnp.int32, cos.shape, 1)
    first_half = (lane % ROT_DIM) < (ROT_DIM // 2)

    def rope(a):
        cols = []
        for j in range(GROUP_W // LANES):
            aj = a[:, j * LANES:(j + 1) * LANES]
            partner = jnp.where(first_half,
                                pltpu.roll(aj, LANES - ROT_DIM // 2, 1),
                                pltpu.roll(aj, ROT_DIM // 2, 1))
            cols.append(aj * cos + partner * sin)
        return jnp.concatenate(cols, axis=1)

    scale = HEAD_DIM ** -0.5
    qsb_ref[...] = (proj[:, 0:GROUP_W] * scale).astype(BF16)
    qdf_ref[...] = (rope(proj[:, 3 * GROUP_W:4 * GROUP_W]) * scale).astype(BF16)
    kdf = rope(proj[:, 4 * GROUP_W:5 * GROUP_W])
    for f32_ref, bf_ref, val in ((ksb_ref, ksb_bf_ref, proj[:, GROUP_W:2 * GROUP_W]),
                                 (vsb_ref, vsb_bf_ref, proj[:, 2 * GROUP_W:3 * GROUP_W]),
                                 (kdf_ref, kdf_bf_ref, kdf),
                                 (vdf_ref, vdf_bf_ref, proj[:, 5 * GROUP_W:6 * GROUP_W])):
        if feature_major and f32_ref is not vdf_ref:
            f32_ref[0] = val.T
        else:
            f32_ref[...] = val
        bf_ref[...] = val.astype(BF16)


def _rope_tables(pos):
    half = ROT_DIM // 2
    inv = ROPE_THETA ** (-jnp.arange(half, dtype=F32) * 2.0 / ROT_DIM)
    ang = pos.astype(F32)[:, None] * inv[None, :]
    cos, sin = jnp.cos(ang), jnp.sin(ang)
    n = pos.shape[0]
    pad = HEAD_DIM - ROT_DIM
    cos_h = jnp.concatenate([cos, cos, jnp.ones((n, pad), F32)], axis=1)
    sin_h = jnp.concatenate([-sin, sin, jnp.zeros((n, pad), F32)], axis=1)
    reps = LANES // HEAD_DIM
    return jnp.tile(cos_h, (1, reps)), jnp.tile(sin_h, (1, reps))


def _inproj(x2d, g, w_bf, pos, tm, feature_major=False):
    n, d = x2d.shape
    cos_t, sin_t = _rope_tables(pos)
    period = pos.shape[0]
    if period < tm:
        cos_t = jnp.tile(cos_t, (tm // period, 1))
        sin_t = jnp.tile(sin_t, (tm // period, 1))
    ntab = cos_t.shape[0] // tm
    row = lambda i: (i, 0)
    fixed = lambda i: (0, 0)
    tab = lambda i: (i % ntab, 0)
    out_spec = pl.BlockSpec((tm, GROUP_W), row)
    f32_spec, f32_shape = out_spec, jax.ShapeDtypeStruct((n, GROUP_W), F32)
    if feature_major:
        f32_spec = pl.BlockSpec((1, GROUP_W, tm), lambda i: (i // ntab, 0, i % ntab))
        f32_shape = jax.ShapeDtypeStruct((n // period, GROUP_W, period), F32)
    return pl.pallas_call(
        functools.partial(_inproj_kernel, feature_major=feature_major),
        grid=(n // tm,),
        in_specs=[pl.BlockSpec((tm, d), row), pl.BlockSpec((1, d), fixed),
                  pl.BlockSpec(w_bf.shape, fixed),
                  pl.BlockSpec((tm, LANES), tab), pl.BlockSpec((tm, LANES), tab)],
        out_specs=[out_spec] * 2 + [f32_spec] * 3 + [out_spec] * 5,
        out_shape=[jax.ShapeDtypeStruct((n, GROUP_W), BF16)] * 2
                  + [f32_shape] * 3 + [jax.ShapeDtypeStruct((n, GROUP_W), F32)]
                  + [jax.ShapeDtypeStruct((n, GROUP_W), BF16)] * 4,
        compiler_params=_cparams(("parallel",)),
        name="inproj",
    )(x2d, g, w_bf, cos_t, sin_t)


def _sb_kernel(q_ref, kp_ref, vp_ref, ks_ref, vs_ref, o_ref, acc_ref, car_ref, *cache_scratch,
               tq, n_pre, bq, bp, cached):
    lane = lax.broadcasted_iota(jnp.int32, (bq, LANES), 1)
    lo = lane < HEAD_DIM
    contract_last = (((1,), (1,)), ((), ()))
    wn = 2 * bq if tq >= 2 * bq else bq
    step = pl.program_id(0)

    def neg_suffix_matrix(n):
        j = lax.broadcasted_iota(jnp.int32, (n, n + LANES), 0)
        s = lax.broadcasted_iota(jnp.int32, (n, n + LANES), 1)
        return jnp.where((j > s) | (s >= n), -1.0, 0.0).astype(BF16)

    suffixes = {n: neg_suffix_matrix(n) for n in {wn, bq, bp}}

    def row_major(k_ref, v_ref, k0, n):
        def load(hp):
            cols = slice(hp * LANES, (hp + 1) * LANES)
            return k_ref[0, pl.ds(k0, n), cols].astype(BF16), v_ref[0, pl.ds(k0, n), cols].astype(BF16)
        return load

    def sweep(load, feature_major, q0, koff, n, causal, first):
        suffix = suffixes[n]
        if causal:
            r = lax.broadcasted_iota(jnp.int32, (bq, n), 0)
            c = lax.broadcasted_iota(jnp.int32, (bq, n), 1)
            vis = (c + koff) < r
        worst = None
        for hp in range(SB_HEADS // 2):
            cols = slice(hp * LANES, (hp + 1) * LANES)
            k2, v2 = load(hp)
            q2 = q_ref[0, pl.ds(q0, bq), cols]
            pv = []
            for m in range(2):
                qm = jnp.where(lo if m == 0 else ~lo, q2, jnp.zeros_like(q2))
                if feature_major:
                    s = jnp.dot(qm, k2, preferred_element_type=F32)
                else:
                    s = lax.dot_general(qm, k2, contract_last, preferred_element_type=F32)
                sp = jnp.maximum(s, 0.0) + jnp.log(1.0 + jnp.exp(-jnp.abs(s)))
                spm = jnp.where(vis, sp, 0.0) if causal else sp
                cr = jnp.dot(spm.astype(BF16), suffix, preferred_element_type=F32)
                between = cr[:, :n]
                car = cr[:, n:]
                if not first:
                    old = car_ref[2 * hp + m]
                    between = between + (jnp.tile(old, (1, n // LANES)) if n % LANES == 0 else old[:, :n])
                    car = car + old
                p = jnp.exp(s - sp + between)
                if causal:
                    p = jnp.where(vis, p, 0.0)
                if feature_major:
                    pv.append(lax.dot_general(p.astype(BF16), v2, contract_last, preferred_element_type=F32))
                else:
                    pv.append(jnp.dot(p.astype(BF16), v2, preferred_element_type=F32))
                car_ref[2 * hp + m] = car
                worst = car if worst is None else jnp.maximum(worst, car)
            both = jnp.where(lo, pv[0], pv[1])
            if first:
                acc_ref[:, cols] = both
            else:
                acc_ref[:, cols] += both
        return jnp.max(worst) > SB_CUTOFF

    if cached:
        kwin_ref, vwin_ref, kblk_ref, vblk_ref, win_sem, blk_sem = cache_scratch
        pw = min(2 * bp, n_pre)

        def window_copies(bi, slot):
            return [pltpu.make_async_copy(src.at[bi, :, pl.ds(n_pre - pw, pw)], dst.at[slot], win_sem.at[a, slot])
                    for a, (src, dst) in enumerate(((kp_ref, kwin_ref), (vp_ref, vwin_ref)))]

        def block_copies(k0):
            return [pltpu.make_async_copy(src.at[step, :, pl.ds(k0, bp)], dst, blk_sem.at[a])
                    for a, (src, dst) in enumerate(((kp_ref, kblk_ref), (vp_ref, vblk_ref)))]

        slot = step % 2

        @pl.when(step == 0)
        def _():
            for cp in window_copies(step, slot):
                cp.start()

        @pl.when(step + 1 < pl.num_programs(0))
        def _():
            for cp in window_copies(step + 1, 1 - slot):
                cp.start()

        for cp in window_copies(step, slot):
            cp.wait()

    def cond(st):
        return (st[0] >= 0) & st[1]

    def q_block(qi, carry):
        q0 = pl.multiple_of(qi * bq, bq)
        first_blk = jnp.maximum(qi + 1 - wn // bq, 0)
        k0 = pl.multiple_of(first_blk * bq, bq)
        go = sweep(row_major(ks_ref, vs_ref, k0, wn), False, q0, k0 - q0, wn, True, True)

        def self_body(st):
            kj = pl.multiple_of(st[0] * bq, bq)
            return st[0] - 1, sweep(row_major(ks_ref, vs_ref, kj, bq), False, q0, 0, bq, False, False)

        _, go = lax.while_loop(cond, self_body, (first_blk - 1, go))

        if not cached:
            def pre_body(st):
                kj = pl.multiple_of(st[0] * bp, bp)
                return st[0] - 1, sweep(row_major(kp_ref, vp_ref, kj, bp), False, q0, 0, bp, False, False)

            lax.while_loop(cond, pre_body, (jnp.int32(n_pre // bp - 1), go))
        else:
            def window_block(off):
                def load(hp):
                    rows = slice(hp * LANES, (hp + 1) * LANES)
                    return (kwin_ref[slot, rows, off:off + bp].astype(BF16),
                            vwin_ref[slot, rows, off:off + bp].astype(BF16))
                return load

            for off in range(pw - bp, -1, -bp):
                go = lax.cond(go, lambda off=off: sweep(window_block(off), True, q0, 0, bp, False, False),
                              lambda: jnp.zeros((), jnp.bool_))

            def older_body(st):
                for cp in block_copies(pl.multiple_of(st[0] * bp, bp)):
                    cp.start()
                for cp in block_copies(pl.multiple_of(st[0] * bp, bp)):
                    cp.wait()

                def load(hp):
                    rows = slice(hp * LANES, (hp + 1) * LANES)
                    return kblk_ref[rows, :].astype(BF16), vblk_ref[rows, :].astype(BF16)

                return st[0] - 1, sweep(load, True, q0, 0, bp, False, False)

            lax.while_loop(cond, older_body, (jnp.int32((n_pre - pw) // bp - 1), go))
        o_ref[0, pl.ds(q0, bq), :] = acc_ref[...].astype(o_ref.dtype)
        return carry

    lax.fori_loop(0, tq // bq, q_block, 0)


def _sb_attention(q, k_pre, v_pre, k_self, v_self, cached=False):
    b, tq, w = q.shape
    n_pre = k_pre.shape[2] if cached else k_pre.shape[1]
    bq = min(128, tq)
    bp = min(128, n_pre)
    bat = lambda i: (i, 0, 0)
    scratch = [pltpu.VMEM((bq, w), F32), pltpu.VMEM((SB_HEADS, bq, LANES), F32)]
    if cached:
        pre_spec = pl.BlockSpec(memory_space=pl.ANY)
        pw = min(2 * bp, n_pre)
        scratch += [pltpu.VMEM((2, w, pw), F32), pltpu.VMEM((2, w, pw), F32),
                    pltpu.VMEM((w, bp), F32), pltpu.VMEM((w, bp), F32),
                    pltpu.SemaphoreType.DMA((2, 2)), pltpu.SemaphoreType.DMA((2,))]
    else:
        pre_map = (lambda i: (i, 0, 0)) if k_pre.shape[0] == b else (lambda i: (0, 0, 0))
        pre_spec = pl.BlockSpec((1, n_pre, w), pre_map)
    kern = functools.partial(_sb_kernel, tq=tq, n_pre=n_pre, bq=bq, bp=bp, cached=cached)
    return pl.pallas_call(
        kern,
        grid=(b,),
        in_specs=[pl.BlockSpec((1, tq, w), bat), pre_spec, pre_spec,
                  pl.BlockSpec((1, tq, w), bat), pl.BlockSpec((1, tq, w), bat)],
        out_specs=pl.BlockSpec((1, tq, w), bat),
        out_shape=jax.ShapeDtypeStruct((b, tq, w), BF16),
        scratch_shapes=scratch,
        compiler_params=_cparams(("arbitrary",) if cached else ("parallel",)),
        name="sb_attention",
    )(q, k_pre, v_pre, k_self, v_self)


def _df_kernel(q_ref, kp_ref, vp_ref, ks_ref, vs_ref, lq1_ref, lk1_ref, lq2_ref, lk2_ref, g_ref,
               o_ref, vxp_ref, vxs_ref, qm_ref, mx_ref, acc_ref, *, tq, n_pre, bq, bp, lam_init, cached):
    n_maps = 2 * DF_HEADS
    lane = lax.broadcasted_iota(jnp.int32, (bq, LANES), 1)
    lo = lane < HEAD_DIM
    contract_last = (((1,), (1,)), ((), ()))
    lam = (jnp.exp(jnp.sum(lq1_ref[...] * lk1_ref[...], axis=-1, keepdims=True))
           - jnp.exp(jnp.sum(lq2_ref[...] * lk2_ref[...], axis=-1, keepdims=True)) + lam_init)

    def extend_values(v_ref, vx_ref, head_rows):
        n = vx_ref.shape[0]
        ones = jnp.ones((n, LANES), BF16)
        for h in range(DF_HEADS):
            if head_rows:
                v_h = v_ref[0, pl.ds(h, n, stride=DF_HEADS), :]
            else:
                v_h = v_ref[0, :, h * LANES:(h + 1) * LANES]
            vx_ref[:, 2 * h * LANES:(2 * h + 1) * LANES] = v_h.astype(BF16)
            vx_ref[:, (2 * h + 1) * LANES:(2 * h + 2) * LANES] = ones

    extend_values(vp_ref, vxp_ref, cached)
    extend_values(vs_ref, vxs_ref, False)

    def half_sums(x):
        in_lo = lax.broadcasted_iota(jnp.int32, x.shape, 1) < HEAD_DIM
        return (jnp.sum(jnp.where(in_lo, x, 0.0), axis=1, keepdims=True),
                jnp.sum(jnp.where(in_lo, 0.0, x), axis=1, keepdims=True))

    kmax2 = []
    for h in range(DF_HEADS):
        ks = ks_ref[0, :, h * LANES:(h + 1) * LANES].astype(BF16).astype(F32)
        halves = [jnp.max(v, axis=0, keepdims=True) for v in half_sums(ks * ks)]
        if cached:
            kp = kp_ref[0, h * LANES:(h + 1) * LANES, :].astype(BF16).astype(F32)
            sq = kp * kp
            pre = [jnp.max(jnp.sum(sq[m * HEAD_DIM:(m + 1) * HEAD_DIM], axis=0, keepdims=True), axis=1, keepdims=True)
                   for m in range(2)]
        else:
            kp = kp_ref[0, :, h * LANES:(h + 1) * LANES].astype(BF16).astype(F32)
            pre = [jnp.max(v, axis=0, keepdims=True) for v in half_sums(kp * kp)]
        kmax2 += [jnp.maximum(halves[m], pre[m]) for m in range(2)]

    def scores(c, k_ref, k0, n, chunk_mask):
        h = c // 2
        if cached and k_ref is kp_ref:
            k2 = k_ref[0, h * LANES:(h + 1) * LANES, pl.ds(k0, n)].astype(BF16)
            s = jnp.dot(qm_ref[c], k2, preferred_element_type=F32)
        else:
            k2 = k_ref[0, pl.ds(k0, n), h * LANES:(h + 1) * LANES].astype(BF16)
            s = lax.dot_general(qm_ref[c], k2, contract_last, preferred_element_type=F32)
        if chunk_mask:
            r = lax.broadcasted_iota(jnp.int32, (bq, n), 0)
            col = lax.broadcasted_iota(jnp.int32, (bq, n), 1)
            s = jnp.where((col // CHUNK) <= (r // CHUNK), s, -jnp.inf)
        return s

    def max_tile(k_ref, k0, n, chunk_mask):
        for c in range(n_maps):
            s = scores(c, k_ref, k0, n, chunk_mask)
            if n % LANES == 0:
                m = s[:, :LANES]
                for j in range(1, n // LANES):
                    m = jnp.maximum(m, s[:, j * LANES:(j + 1) * LANES])
            else:
                m = jnp.broadcast_to(jnp.max(s, axis=1, keepdims=True), (bq, LANES))
            mx_ref[c] = jnp.maximum(mx_ref[c], m)

    def value_tile(k_ref, vx_ref, k0, n, chunk_mask):
        for c in range(n_maps):
            h = c // 2
            s = scores(c, k_ref, k0, n, chunk_mask)
            row_max = mx_ref[c]
            if n % LANES == 0:
                p = jnp.concatenate([jnp.exp(s[:, j * LANES:(j + 1) * LANES] - row_max)
                                     for j in range(n // LANES)], axis=1)
            else:
                p = jnp.exp(s - row_max[:, :n])
            vx = vx_ref[pl.ds(k0, n), 2 * h * LANES:(2 * h + 2) * LANES]
            acc_ref[c] += jnp.dot(p.astype(BF16), vx, preferred_element_type=F32)

    def q_block(qi, carry):
        q0 = pl.multiple_of(qi * bq, bq)
        for h in range(DF_HEADS):
            q2 = q_ref[0, pl.ds(q0, bq), h * LANES:(h + 1) * LANES]
            zero = jnp.zeros_like(q2)
            qm_ref[2 * h] = jnp.where(lo, q2, zero)
            qm_ref[2 * h + 1] = jnp.where(lo, zero, q2)
        acc_ref[...] = jnp.zeros_like(acc_ref)

        gap = None
        for h in range(DF_HEADS):
            qf = q_ref[0, pl.ds(q0, bq), h * LANES:(h + 1) * LANES].astype(F32)
            kf = ks_ref[0, pl.ds(q0, bq), h * LANES:(h + 1) * LANES].astype(BF16).astype(F32)
            qn2 = half_sums(qf * qf)
            own = half_sums(qf * kf)
            for m in range(2):
                bound = jnp.sqrt(qn2[m] * kmax2[2 * h + m]) * 1.001 + 1e-6
                mx_ref[2 * h + m] = jnp.broadcast_to(bound, (bq, LANES))
                g = jnp.max(bound - own[m])
                gap = g if gap is None else jnp.maximum(gap, g)
        bound_is_tight = gap < SOFTMAX_SHIFT_SLACK

        def sweep(pre_fn, self_fn):
            def pre_body(j, c):
                pre_fn(pl.multiple_of(j * bp, bp))
                return c

            def self_body(j, c):
                self_fn(pl.multiple_of(j * bq, bq), False)
                return c

            lax.fori_loop(0, n_pre // bp, pre_body, 0)
            lax.fori_loop(0, qi, self_body, 0)
            self_fn(q0, True)

        @pl.when(jnp.logical_not(bound_is_tight))
        def _():
            mx_ref[...] = jnp.full_like(mx_ref, -jnp.inf)
            sweep(lambda k0: max_tile(kp_ref, k0, bp, False),
                  lambda k0, msk: max_tile(ks_ref, k0, bq, msk))
            for c in range(n_maps):
                mx_ref[c] = jnp.broadcast_to(jnp.max(mx_ref[c], axis=1, keepdims=True), (bq, LANES))

        sweep(lambda k0: value_tile(kp_ref, vxp_ref, k0, bp, False),
              lambda k0, msk: value_tile(ks_ref, vxs_ref, k0, bq, msk))

        for h in range(DF_HEADS):
            a0 = acc_ref[2 * h]
            a1 = acc_ref[2 * h + 1]
            o = a0[:, :LANES] / a0[:, LANES:] - lam * (a1[:, :LANES] / a1[:, LANES:])
            y = o * lax.rsqrt(jnp.mean(o * o, axis=-1, keepdims=True) + EPS) * g_ref[...]
            o_ref[0, pl.ds(q0, bq), h * LANES:(h + 1) * LANES] = (y * (1.0 - lam_init)).astype(o_ref.dtype)
        return carry

    lax.fori_loop(0, tq // bq, q_block, 0)


def _df_attention(q, k_pre, v_pre, k_self, v_self, lq1, lk1, lq2, lk2, subln_g, lam_init, cached=False):
    b, tq, w = q.shape
    n_pre = k_pre.shape[2] if cached else k_pre.shape[1]
    bq = min(256, tq)
    bp = min(256, n_pre)
    pre_map = (lambda i: (i, 0, 0)) if k_pre.shape[0] == b else (lambda i: (0, 0, 0))
    bat = lambda i: (i, 0, 0)
    fixed = lambda i: (0, 0)
    kern = functools.partial(_df_kernel, tq=tq, n_pre=n_pre, bq=bq, bp=bp, lam_init=lam_init, cached=cached)
    kp_spec = pl.BlockSpec((1,) + k_pre.shape[1:], pre_map)
    vp_spec = pl.BlockSpec((1,) + v_pre.shape[1:], pre_map)
    vec = pl.BlockSpec((1, HEAD_DIM), fixed)
    return pl.pallas_call(
        kern,
        grid=(b,),
        in_specs=[pl.BlockSpec((1, tq, w), bat),
                  kp_spec, vp_spec,
                  pl.BlockSpec((1, tq, w), bat), pl.BlockSpec((1, tq, w), bat),
                  vec, vec, vec, vec, pl.BlockSpec((1, LANES), fixed)],
        out_specs=pl.BlockSpec((1, tq, w), bat),
        out_shape=jax.ShapeDtypeStruct((b, tq, w), BF16),
        scratch_shapes=[pltpu.VMEM((n_pre, 2 * w), BF16), pltpu.VMEM((tq, 2 * w), BF16),
                        pltpu.VMEM((2 * DF_HEADS, bq, LANES), BF16), pltpu.VMEM((2 * DF_HEADS, bq, LANES), F32),
                        pltpu.VMEM((2 * DF_HEADS, bq, 2 * LANES), F32)],
        compiler_params=_cparams(("parallel",)),
        name="df_attention",
    )(q, k_pre, v_pre, k_self, v_self, lq1, lk1, lq2, lk2, subln_g)


def _outproj_kernel(x_ref, sb_ref, df_ref, wsb_ref, wdf_ref, g_ref, wr_ref, br_ref,
                    h_ref, route_ref):
    h = (x_ref[...]
         + jnp.dot(sb_ref[...], wsb_ref[...], preferred_element_type=F32)
         + jnp.dot(df_ref[...], wdf_ref[...], preferred_element_type=F32))
    h_ref[...] = h
    xn = h * lax.rsqrt(jnp.mean(h * h, axis=-1, keepdims=True) + EPS) * g_ref[...]
    logits = jnp.dot(xn.astype(BF16), wr_ref[...], preferred_element_type=F32) + br_ref[...]
    lane = lax.broadcasted_iota(jnp.int32, logits.shape, 1)
    big = jnp.int32(LANES)
    neg = -jnp.inf

    def first_argmax(v):
        mx = jnp.max(v, axis=1, keepdims=True)
        idx = jnp.min(jnp.where(v == mx, lane, big), axis=1, keepdims=True)
        return mx, idx

    gl = jnp.where((lane >= N_EXPERTS) & (lane < N_EXPERTS + N_GROUPS), logits, neg)
    gmax, gidx = first_argmax(gl)
    g_w = 1.0 / jnp.sum(jnp.exp(gl - gmax), axis=1, keepdims=True)
    grp = gidx - N_EXPERTS
    el = jnp.where((lane < N_EXPERTS) & (lane // EXPERTS_PER_GROUP == grp), logits, neg)
    m1, i1 = first_argmax(el)
    m2, i2 = first_argmax(jnp.where(lane == i1, neg, el))
    e21 = jnp.exp(m2 - m1)
    t1 = 1.0 / (1.0 + e21)
    t2 = e21 / (1.0 + e21)
    route = jnp.where(lane == 0, i1.astype(F32),
            jnp.where(lane == 1, i2.astype(F32),
            jnp.where(lane == 2, g_w * t1,
            jnp.where(lane == 3, g_w * t2, 0.0))))
    route_ref[...] = route


def _outproj_route(x2d, sb, df, wsb, wdf, g, wr, br, tm):
    n, d = x2d.shape
    row = lambda i: (i, 0)
    fixed = lambda i: (0, 0)
    return pl.pallas_call(
        _outproj_kernel,
        grid=(n // tm,),
        in_specs=[pl.BlockSpec((tm, d), row), pl.BlockSpec((tm, GROUP_W), row), pl.BlockSpec((tm, GROUP_W), row),
                  pl.BlockSpec(wsb.shape, fixed), pl.BlockSpec(wdf.shape, fixed), pl.BlockSpec((1, d), fixed),
                  pl.BlockSpec(wr.shape, fixed), pl.BlockSpec((1, LANES), fixed)],
        out_specs=[pl.BlockSpec((tm, d), row), pl.BlockSpec((tm, LANES), row)],
        out_shape=[jax.ShapeDtypeStruct((n, d), F32), jax.ShapeDtypeStruct((n, LANES), F32)],
        compiler_params=_cparams(("parallel",)),
        name="outproj_route",
    )(x2d, sb, df, wsb, wdf, g, wr, br)


def _row_copy(src, s, dst, d, sem):
    return pltpu.make_async_copy(src.at[pl.ds(s, 1)], dst.at[pl.ds(d, 1)], sem)


def _dispatch_kernel(d0_ref, d1_ref, hp_ref, hs_ref, xs_hbm, sem, *, tm, n_prompt_tiles):
    def issue_from(src_ref):
        def issue(r, c):
            _row_copy(src_ref, r, xs_hbm, d0_ref[0, 0, r], sem).start(priority=0)
            _row_copy(src_ref, r, xs_hbm, d1_ref[0, 0, r], sem).start(priority=1)
            return c

        lax.fori_loop(0, tm, issue, 0, unroll=8)

    is_prompt = pl.program_id(0) < n_prompt_tiles
    pl.when(is_prompt)(lambda: issue_from(hp_ref))
    pl.when(jnp.logical_not(is_prompt))(lambda: issue_from(hs_ref))
    for _ in range(2):
        pltpu.make_async_copy(hp_ref, xs_hbm.at[pl.ds(0, tm)], sem).wait()


def _dispatch(dest0, dest1, h_prompt, h_sample, tm):
    n_p, d = h_prompt.shape
    n = n_p + h_sample.shape[0]
    npt = n_p // tm
    idx = lambda i: (i, 0, 0)
    smem = functools.partial(pl.BlockSpec, memory_space=pltpu.SMEM)
    return pl.pallas_call(
        functools.partial(_dispatch_kernel, tm=tm, n_prompt_tiles=npt),
        grid=(n // tm,),
        in_specs=[smem((1, 1, tm), idx), smem((1, 1, tm), idx),
                  pl.BlockSpec((tm, d), lambda i: (jnp.minimum(i, npt - 1), 0)),
                  pl.BlockSpec((tm, d), lambda i: (jnp.maximum(i - npt, 0), 0))],
        out_specs=pl.BlockSpec(memory_space=pl.ANY),
        out_shape=jax.ShapeDtypeStruct((2 * n, d), F32),
        scratch_shapes=[pltpu.SemaphoreType.DMA(())],
        compiler_params=_cparams(("arbitrary",)),
        name="moe_dispatch",
    )(dest0.reshape(n // tm, 1, tm), dest1.reshape(n // tm, 1, tm), h_prompt, h_sample)


def _expert_kernel(vt_ref, ve_ref, seg_ref, nv_ref, h_ref, g_ref, wgu_ref, wd_ref, y_ref, *, de, tm):
    v = pl.program_id(0)

    @pl.when(v < nv_ref[0])
    def _():
        e = ve_ref[v]
        t = vt_ref[v]
        h_in = h_ref[...]
        x = (h_in * lax.rsqrt(jnp.mean(h_in * h_in, axis=-1, keepdims=True) + EPS) * g_ref[...]).astype(BF16)
        gu = jnp.dot(x, wgu_ref[0], preferred_element_type=F32)
        g = gu[:, :de]
        h = (g * (1.0 / (1.0 + jnp.exp(-g)))) * gu[:, de:]
        y = jnp.dot(h.astype(BF16), wd_ref[0], preferred_element_type=F32)
        row = t * tm + lax.broadcasted_iota(jnp.int32, (tm, 1), 0)
        mine = (row >= seg_ref[e]) & (row < seg_ref[N_EXPERTS + e])
        first_visit = (v == 0) | (vt_ref[jnp.maximum(v - 1, 0)] != t)

        @pl.when(first_visit)
        def _():
            y_ref[...] = jnp.where(mine, y, 0.0)

        @pl.when(jnp.logical_not(first_visit))
        def _():
            y_ref[...] = jnp.where(mine, y, y_ref[...])


def _experts(visit_tile, visit_expert, segments, n_visits, xs, g, wgu, wd, tm):
    p, d = xs.shape
    de = wd.shape[1]
    grid_spec = pltpu.PrefetchScalarGridSpec(
        num_scalar_prefetch=4,
        grid=(visit_tile.shape[0],),
        in_specs=[pl.BlockSpec((tm, d), lambda v, vt, ve, sg, nv: (vt[v], 0)),
                  pl.BlockSpec((1, d), lambda v, vt, ve, sg, nv: (0, 0)),
                  pl.BlockSpec((1, d, 2 * de), lambda v, vt, ve, sg, nv: (ve[v], 0, 0)),
                  pl.BlockSpec((1, de, d), lambda v, vt, ve, sg, nv: (ve[v], 0, 0))],
        out_specs=pl.BlockSpec((tm, d), lambda v, vt, ve, sg, nv: (vt[v], 0)),
    )
    return pl.pallas_call(
        functools.partial(_expert_kernel, de=de, tm=tm),
        grid_spec=grid_spec,
        out_shape=jax.ShapeDtypeStruct((p, d), F32),
        compiler_params=_cparams(("arbitrary",)),
        name="moe_experts",
    )(visit_tile, visit_expert, segments, n_visits, xs, g, wgu, wd)


def _combine_kernel(d0_ref, d1_ref, h_ref, gate_ref, g_ref, ys_hbm, o_ref, y0_ref, y1_ref, sem, *, tm):
    def issue(r, c):
        _row_copy(ys_hbm, d0_ref[0, 0, r], y0_ref, r, sem).start(priority=0)
        _row_copy(ys_hbm, d1_ref[0, 0, r], y1_ref, r, sem).start(priority=1)
        return c

    lax.fori_loop(0, tm, issue, 0, unroll=8)
    pltpu.make_async_copy(ys_hbm.at[pl.ds(0, tm)], y0_ref, sem).wait()
    pltpu.make_async_copy(ys_hbm.at[pl.ds(0, tm)], y1_ref, sem).wait()
    gate = gate_ref[...]
    h = h_ref[...] + (y0_ref[...] * gate[:, 2:3] + y1_ref[...] * gate[:, 3:4])
    o_ref[...] = h * lax.rsqrt(jnp.mean(h * h, axis=-1, keepdims=True) + EPS) * g_ref[...]


def _combine(dest0, dest1, h2d, route, g, ys, tm):
    n, d = h2d.shape
    idx = lambda i: (i, 0, 0)
    row = lambda i: (i, 0)
    smem = functools.partial(pl.BlockSpec, memory_space=pltpu.SMEM)
    return pl.pallas_call(
        functools.partial(_combine_kernel, tm=tm),
        grid=(n // tm,),
        in_specs=[smem((1, 1, tm), idx), smem((1, 1, tm), idx),
                  pl.BlockSpec((tm, d), row), pl.BlockSpec((tm, LANES), row),
                  pl.BlockSpec((1, d), lambda i: (0, 0)), pl.BlockSpec(memory_space=pl.ANY)],
        out_specs=pl.BlockSpec((tm, d), row),
        out_shape=jax.ShapeDtypeStruct((n, d), F32),
        scratch_shapes=[pltpu.VMEM((tm, d), F32), pltpu.VMEM((tm, d), F32), pltpu.SemaphoreType.DMA(())],
        compiler_params=_cparams(("arbitrary",)),
        name="moe_combine",
    )(dest0.reshape(n // tm, 1, tm), dest1.reshape(n // tm, 1, tm), h2d, route, g, ys)


def _routing_plan(route, tm):
    t = route.shape[0]
    e = route[:, :2].astype(jnp.int32).reshape(-1)
    onehot = (e[:, None] == jnp.arange(N_EXPERTS, dtype=jnp.int32)[None, :])
    blk = 512
    oh = onehot.astype(BF16).reshape(2 * t // blk, blk, N_EXPERTS)
    tri = (jnp.arange(blk)[:, None] > jnp.arange(blk)[None, :]).astype(BF16)
    within = jnp.einsum("ij,bjk->bik", tri, oh, preferred_element_type=F32)
    totals = jnp.sum(oh.astype(F32), axis=1)
    before = jnp.cumsum(totals, axis=0) - totals
    rank_all = (within + before[:, None, :]).reshape(2 * t, N_EXPERTS)
    rank = jnp.sum(jnp.where(onehot, rank_all, 0.0), axis=1).astype(jnp.int32)
    counts = jnp.sum(totals, axis=0).astype(jnp.int32)
    ends = jnp.cumsum(counts)
    starts = ends - counts
    dest = (starts[e] + rank).reshape(t, 2)
    first_tile = starts // tm
    n_vis = jnp.where(counts > 0, (ends - 1) // tm - first_tile + 1, 0)
    vis_end = jnp.cumsum(n_vis)
    vis_start = vis_end - n_vis
    n_visits = vis_end[-1]
    max_visits = 2 * t // tm + N_EXPERTS - 1
    v = jnp.minimum(jnp.arange(max_visits, dtype=jnp.int32), n_visits - 1)
    visit_expert = jnp.minimum(jnp.sum(v[:, None] >= vis_end[None, :], axis=1), N_EXPERTS - 1).astype(jnp.int32)
    visit_tile = (first_tile[visit_expert] + v - vis_start[visit_expert]).astype(jnp.int32)
    segments = jnp.concatenate([starts, ends]).astype(jnp.int32)
    return dest[:, 0], dest[:, 1], visit_tile, visit_expert, segments, n_visits.reshape(1).astype(jnp.int32)


def _pick(n, prefs):
    for p in prefs:
        if n % p == 0:
            return p
    return n


def kernel(x_prompt, x_sample, cache_sb_k, cache_sb_v, cache_diff_k, cache_diff_v, meta_tokens, norm_mix_g, w_in, lambda_q1, lambda_k1, lambda_q2, lambda_k2, subln_g, w_out, norm_ffn_g, w_group, b_group, w_router, b_router, w_gate, w_up, w_down, final_norm_g):
    depth = w_in.shape[0]
    assert depth == 1, "meta-token rows are only dropped after a single layer"
    b, s, d = x_prompt.shape
    bs, ss, _ = x_sample.shape
    past = cache_sb_k.shape[2]
    n_meta = meta_tokens.shape[0]
    lam_init = 0.8 - 0.6 * math.exp(-0.3 * 0)

    g_mix = norm_mix_g[0].reshape(1, d)
    w_in_bf = w_in[0].astype(BF16)
    pos_p = n_meta + jnp.arange(s, dtype=jnp.int32)
    pos_s = past + jnp.arange(ss, dtype=jnp.int32)
    pos_m = jnp.arange(n_meta, dtype=jnp.int32)
    xp2 = x_prompt.reshape(b * s, d)
    xs2 = x_sample.reshape(bs * ss, d)

    tm_p = _pick(s, (512, 256, 128))
    (qsb_p, qdf_p, ksb_p, vsb_p, kdf_p, vdf_p,
     ksb_pb, vsb_pb, kdf_pb, vdf_pb) = _inproj(xp2, g_mix, w_in_bf, pos_p, tm_p, feature_major=True)
    tm_s = _pick(bs * ss, (512, 256, 128))
    (qsb_s, qdf_s, ksb_s, vsb_s, kdf_s, vdf_s,
     ksb_sb, vsb_sb, kdf_sb, vdf_sb) = _inproj(xs2, g_mix, w_in_bf, pos_s, tm_s)
    (_, _, ksb_m, vsb_m, kdf_m, vdf_m,
     ksb_mb, vsb_mb, kdf_mb, vdf_mb) = _inproj(meta_tokens, g_mix, w_in_bf, pos_m, n_meta)

    r3 = lambda a, bb, t: a.reshape(bb, t, GROUP_W)
    lam_vecs = [v[0].reshape(1, HEAD_DIM) for v in (lambda_q1, lambda_k1, lambda_q2, lambda_k2)]
    sub_g = subln_g[0].reshape(1, LANES)

    sb_p = _sb_attention(r3(qsb_p, b, s), ksb_mb[None], vsb_mb[None], r3(ksb_pb, b, s), r3(vsb_pb, b, s))
    df_p = _df_attention(r3(qdf_p, b, s), kdf_mb[None], vdf_mb[None], r3(kdf_pb, b, s), r3(vdf_pb, b, s),
                         *lam_vecs, sub_g, lam_init)
    feature_major = lambda c: jnp.transpose(c[0], (0, 2, 3, 1)).reshape(bs, GROUP_W, past)
    sb_s = _sb_attention(r3(qsb_s, bs, ss), feature_major(cache_sb_k), feature_major(cache_sb_v),
                         r3(ksb_sb, bs, ss), r3(vsb_sb, bs, ss), cached=True)
    df_s = _df_attention(r3(qdf_s, bs, ss), feature_major(cache_diff_k),
                         cache_diff_v[0].reshape(bs, past * DF_HEADS, LANES), r3(kdf_sb, bs, ss), r3(vdf_sb, bs, ss),
                         *lam_vecs, sub_g, lam_init, cached=True)

    w_out_bf = w_out[0].astype(BF16)
    wsb, wdf = w_out_bf[:GROUP_W], w_out_bf[GROUP_W:]
    g_ffn = norm_ffn_g[0].reshape(1, d)
    pad_cols = LANES - N_EXPERTS - N_GROUPS
    wr = jnp.concatenate([w_router[0], w_group[0], jnp.zeros((d, pad_cols), F32)], axis=1).astype(BF16)
    br = jnp.concatenate([b_router[0], b_group[0], jnp.zeros((pad_cols,), F32)]).reshape(1, LANES)

    h_p, route_p = _outproj_route(xp2, sb_p.reshape(b * s, GROUP_W), df_p.reshape(b * s, GROUP_W),
                                        wsb, wdf, g_ffn, wr, br, tm_p)
    h_s, route_s = _outproj_route(xs2, sb_s.reshape(bs * ss, GROUP_W), df_s.reshape(bs * ss, GROUP_W),
                                        wsb, wdf, g_ffn, wr, br, tm_s)

    tm_e = 512
    n_p = b * s
    tm_d = _pick(math.gcd(n_p, bs * ss), (512, 256, 128))
    route = jnp.concatenate([route_p, route_s], axis=0)
    dest0, dest1, visit_tile, visit_expert, segments, n_visits = _routing_plan(route, tm_e)
    xsorted = _dispatch(dest0, dest1, h_p, h_s, tm_d)
    wgu = jnp.concatenate([w_gate[0], w_up[0]], axis=-1).astype(BF16)
    ysorted = _experts(visit_tile, visit_expert, segments, n_visits, xsorted, g_ffn, wgu,
                       w_down[0].astype(BF16), tm_e)
    g_fin = final_norm_g.reshape(1, d)
    y_p = _combine(dest0[:n_p], dest1[:n_p], h_p, route_p, g_fin, ysorted, tm_d)
    y_s = _combine(dest0[n_p:], dest1[n_p:], h_s, route_s, g_fin, ysorted, tm_d)

    def with_meta(meta_rows, frames, heads, hd):
        m = jnp.broadcast_to(meta_rows[None], (b, n_meta, GROUP_W))
        return jnp.concatenate([m, frames.reshape(b, s, GROUP_W)], axis=1).reshape(1, b, n_meta + s, heads, hd)

    def with_meta_t(meta_rows, frames_t, heads, hd):
        m = jnp.broadcast_to(meta_rows.T[None], (b, GROUP_W, n_meta))
        full = jnp.concatenate([m, frames_t], axis=2).reshape(b, heads, hd, n_meta + s)
        return jnp.transpose(full, (0, 3, 1, 2))[None]

    shp = lambda a, heads, hd: a.reshape(1, bs, ss, heads, hd)
    return (y_p.reshape(b, s, d), y_s.reshape(bs, ss, d),
            with_meta_t(ksb_m, ksb_p, SB_HEADS, HEAD_DIM), with_meta_t(vsb_m, vsb_p, SB_HEADS, HEAD_DIM),
            with_meta_t(kdf_m, kdf_p, 2 * DF_HEADS, HEAD_DIM), with_meta(vdf_m, vdf_p, DF_HEADS, 2 * HEAD_DIM),
            shp(ksb_s, SB_HEADS, HEAD_DIM), shp(vsb_s, SB_HEADS, HEAD_DIM),
            shp(kdf_s, 2 * DF_HEADS, HEAD_DIM), shp(vdf_s, DF_HEADS, 2 * HEAD_DIM))
```

```python
import functools
import math

import jax
import jax.numpy as jnp
from jax import lax
from jax.experimental import pallas as pl
from jax.experimental.pallas import tpu as pltpu

F32 = jnp.float32
BF16 = jnp.bfloat16

CHUNK = 64
SB_HEADS = 8
DF_HEADS = 4
HEAD_DIM = 64
GROUP_W = 512
ROPE_THETA = 500000.0
ROT_DIM = 16
N_GROUPS = 4
EXPERTS_PER_GROUP = 8
N_EXPERTS = N_GROUPS * EXPERTS_PER_GROUP
EPS = 1e-6
LANES = 128
SB_CUTOFF = -88.0
VMEM_LIMIT = 56 * 1024 * 1024
SOFTMAX_SHIFT_SLACK = 80.0


def _cparams(sem):
    return pltpu.CompilerParams(dimension_semantics=sem, vmem_limit_bytes=VMEM_LIMIT)


def _inproj_kernel(x_ref, g_ref, w_ref, cos_ref, sin_ref,
                   qsb_ref, qdf_ref, ksb_ref, vsb_ref, kdf_ref, vdf_ref,
                   ksb_bf_ref, vsb_bf_ref, kdf_bf_ref, vdf_bf_ref, *, feature_major):
    x = x_ref[...]
    ms = jnp.mean(x * x, axis=-1, keepdims=True)
    xn = (x * lax.rsqrt(ms + EPS) * g_ref[...]).astype(BF16)
    proj = jnp.dot(xn, w_ref[...], preferred_element_type=F32)
    cos = cos_ref[...]
    sin = sin_ref[...]
    lane = lax.broadcasted_iota(jnp.int32, cos.shape, 1)
    first_half = (lane % ROT_DIM) < (ROT_DIM // 2)

    def rope(a):
        cols = []
        for j in range(GROUP_W // LANES):
            aj = a[:, j * LANES:(j + 1) * LANES]
            partner = jnp.where(first_half,
                                pltpu.roll(aj, LANES - ROT_DIM // 2, 1),
                                pltpu.roll(aj, ROT_DIM // 2, 1))
            cols.append(aj * cos + partner * sin)
        return jnp.concatenate(cols, axis=1)

    scale = HEAD_DIM ** -0.5
    qsb_ref[...] = (proj[:, 0:GROUP_W] * scale).astype(BF16)
    qdf_ref[...] = (rope(proj[:, 3 * GROUP_W:4 * GROUP_W]) * scale).astype(BF16)
    kdf = rope(proj[:, 4 * GROUP_W:5 * GROUP_W])
    for f32_ref, bf_ref, val in ((ksb_ref, ksb_bf_ref, proj[:, GROUP_W:2 * GROUP_W]),
                                 (vsb_ref, vsb_bf_ref, proj[:, 2 * GROUP_W:3 * GROUP_W]),
                                 (kdf_ref, kdf_bf_ref, kdf),
                                 (vdf_ref, vdf_bf_ref, proj[:, 5 * GROUP_W:6 * GROUP_W])):
        if feature_major and f32_ref is not vdf_ref:
            f32_ref[0] = val.T
        else:
            f32_ref[...] = val
        bf_ref[...] = val.astype(BF16)


def _rope_tables(pos):
    half = ROT_DIM // 2
    inv = ROPE_THETA ** (-jnp.arange(half, dtype=F32) * 2.0 / ROT_DIM)
    ang = pos.astype(F32)[:, None] * inv[None, :]
    cos, sin = jnp.cos(ang), jnp.sin(ang)
    n = pos.shape[0]
    pad = HEAD_DIM - ROT_DIM
    cos_h = jnp.concatenate([cos, cos, jnp.ones((n, pad), F32)], axis=1)
    sin_h = jnp.concatenate([-sin, sin, jnp.zeros((n, pad), F32)], axis=1)
    reps = LANES // HEAD_DIM
    return jnp.tile(cos_h, (1, reps)), jnp.tile(sin_h, (1, reps))


def _inproj(x2d, g, w_bf, pos, tm, feature_major=False):
    n, d = x2d.shape
    cos_t, sin_t = _rope_tables(pos)
    period = pos.shape[0]
    if period < tm:
        cos_t = jnp.tile(cos_t, (tm // period, 1))
        sin_t = jnp.tile(sin_t, (tm // period, 1))
    ntab = cos_t.shape[0] // tm
    row = lambda i: (i, 0)
    fixed = lambda i: (0, 0)
    tab = lambda i: (i % ntab, 0)
    out_spec = pl.BlockSpec((tm, GROUP_W), row)
    f32_spec, f32_shape = out_spec, jax.ShapeDtypeStruct((n, GROUP_W), F32)
    if feature_major:
        f32_spec = pl.BlockSpec((1, GROUP_W, tm), lambda i: (i // ntab, 0, i % ntab))
        f32_shape = jax.ShapeDtypeStruct((n // period, GROUP_W, period), F32)
    return pl.pallas_call(
        functools.partial(_inproj_kernel, feature_major=feature_major),
        grid=(n // tm,),
        in_specs=[pl.BlockSpec((tm, d), row), pl.BlockSpec((1, d), fixed),
                  pl.BlockSpec(w_bf.shape, fixed),
                  pl.BlockSpec((tm, LANES), tab), pl.BlockSpec((tm, LANES), tab)],
        out_specs=[out_spec] * 2 + [f32_spec] * 3 + [out_spec] * 5,
        out_shape=[jax.ShapeDtypeStruct((n, GROUP_W), BF16)] * 2
                  + [f32_shape] * 3 + [jax.ShapeDtypeStruct((n, GROUP_W), F32)]
                  + [jax.ShapeDtypeStruct((n, GROUP_W), BF16)] * 4,
        compiler_params=_cparams(("parallel",)),
        name="inproj",
    )(x2d, g, w_bf, cos_t, sin_t)


def _sb_kernel(q_ref, kp_ref, vp_ref, ks_ref, vs_ref, o_ref, acc_ref, car_ref, *cache_scratch,
               tq, n_pre, bq, bp, ways, cached):
    lane = lax.broadcasted_iota(jnp.int32, (bq, LANES), 1)
    lo = lane < HEAD_DIM
    contract_last = (((1,), (1,)), ((), ()))
    wn = 2 * bq if tq >= 2 * bq else bq
    step = pl.program_id(0)

    def neg_suffix_matrix(n):
        j = lax.broadcasted_iota(jnp.int32, (n, n + LANES), 0)
        s = lax.broadcasted_iota(jnp.int32, (n, n + LANES), 1)
        return jnp.where((j > s) | (s >= n), -1.0, 0.0).astype(BF16)

    suffixes = {n: neg_suffix_matrix(n) for n in {wn, bq, bp}}

    def row_major(k_ref, v_ref, k0, n):
        def load(hp):
            cols = slice(hp * LANES, (hp + 1) * LANES)
            return k_ref[0, pl.ds(k0, n), cols].astype(BF16), v_ref[0, pl.ds(k0, n), cols].astype(BF16)
        return load

    def sweep(load, feature_major, q0, koff, n, causal, first, w):
        suffix = suffixes[n]
        n_pairs = SB_HEADS // 2
        if causal:
            r = lax.broadcasted_iota(jnp.int32, (2 * bq, n), 0)
            c = lax.broadcasted_iota(jnp.int32, (2 * bq, n), 1)
            vis = (c + koff) < jnp.where(r >= bq, r - bq, r)
        log_sig, masked_sp, values = [], [], []
        for hp in range(n_pairs):
            cols = slice(hp * LANES, (hp + 1) * LANES)
            k2, v2 = load(hp)
            q2 = q_ref[0, pl.ds(q0, bq), cols]
            zero = jnp.zeros_like(q2)
            qs = jnp.concatenate([jnp.where(lo, q2, zero), jnp.where(lo, zero, q2)], axis=0)
            if feature_major:
                s = jnp.dot(qs, k2, preferred_element_type=F32)
            else:
                s = lax.dot_general(qs, k2, contract_last, preferred_element_type=F32)
            sp = jnp.maximum(s, 0.0) + jnp.log(1.0 + jnp.exp(-jnp.abs(s)))
            log_sig.append(s - sp)
            masked_sp.append((jnp.where(vis, sp, 0.0) if causal else sp).astype(BF16))
            values.append(v2)
        cr_all = jnp.dot(jnp.concatenate(masked_sp, axis=0), suffix, preferred_element_type=F32)
        worst = None
        for hp in range(n_pairs):
            cols = slice(hp * LANES, (hp + 1) * LANES)
            cr = cr_all[2 * bq * hp:2 * bq * (hp + 1)]
            between = cr[:, :n]
            car = cr[:, n:]
            if not first:
                old = car_ref[w, hp]
                between = between + (jnp.tile(old, (1, n // LANES)) if n % LANES == 0 else old[:, :n])
                car = car + old
            p = jnp.exp(log_sig[hp] + between)
            if causal:
                p = jnp.where(vis, p, 0.0)
            if feature_major:
                pv = lax.dot_general(p.astype(BF16), values[hp], contract_last, preferred_element_type=F32)
            else:
                pv = jnp.dot(p.astype(BF16), values[hp], preferred_element_type=F32)
            car_ref[w, hp] = car
            worst = car if worst is None else jnp.maximum(worst, car)
            both = jnp.where(lo, pv[:bq], pv[bq:])
            if first:
                acc_ref[w, :, cols] = both
            else:
                acc_ref[w, :, cols] += both
        return jnp.max(worst) > SB_CUTOFF

    if cached:
        kwin_ref, vwin_ref, kblk_ref, vblk_ref, win_sem, blk_sem = cache_scratch
        pw = min(2 * bp, n_pre)

        def window_copies(bi, slot):
            return [pltpu.make_async_copy(src.at[bi, :, pl.ds(n_pre - pw, pw)], dst.at[slot], win_sem.at[a, slot])
                    for a, (src, dst) in enumerate(((kp_ref, kwin_ref), (vp_ref, vwin_ref)))]

        def block_copies(k0):
            return [pltpu.make_async_copy(src.at[step, :, pl.ds(k0, bp)], dst, blk_sem.at[a])
                    for a, (src, dst) in enumerate(((kp_ref, kblk_ref), (vp_ref, vblk_ref)))]

        slot = step % 2

        @pl.when(step == 0)
        def _():
            for cp in window_copies(step, slot):
                cp.start()

        @pl.when(step + 1 < pl.num_programs(0))
        def _():
            for cp in window_copies(step + 1, 1 - slot):
                cp.start()

        for cp in window_copies(step, slot):
            cp.wait()

    def cond(st):
        return (st[0] >= 0) & st[1]

    def first_sweep(qi, w):
        q0 = pl.multiple_of(qi * bq, bq)
        first_blk = jnp.maximum(qi + 1 - wn // bq, 0)
        k0 = pl.multiple_of(first_blk * bq, bq)
        return q0, first_blk, sweep(row_major(ks_ref, vs_ref, k0, wn), False, q0, k0 - q0, wn, True, True, w)

    def finish(w, q0, first_blk, go):
        def self_body(st):
            kj = pl.multiple_of(st[0] * bq, bq)
            return st[0] - 1, sweep(row_major(ks_ref, vs_ref, kj, bq), False, q0, 0, bq, False, False, w)

        _, go = lax.while_loop(cond, self_body, (first_blk - 1, go))

        if not cached:
            def pre_body(st):
                kj = pl.multiple_of(st[0] * bp, bp)
                return st[0] - 1, sweep(row_major(kp_ref, vp_ref, kj, bp), False, q0, 0, bp, False, False, w)

            lax.while_loop(cond, pre_body, (jnp.int32(n_pre // bp - 1), go))
        else:
            def window_block(off):
                def load(hp):
                    rows = slice(hp * LANES, (hp + 1) * LANES)
                    return (kwin_ref[slot, rows, off:off + bp].astype(BF16),
                            vwin_ref[slot, rows, off:off + bp].astype(BF16))
                return load

            for off in range(pw - bp, -1, -bp):
                go = lax.cond(go, lambda off=off: sweep(window_block(off), True, q0, 0, bp, False, False, w),
                              lambda: jnp.zeros((), jnp.bool_))

            def older_body(st):
                for cp in block_copies(pl.multiple_of(st[0] * bp, bp)):
                    cp.start()
                for cp in block_copies(pl.multiple_of(st[0] * bp, bp)):
                    cp.wait()

                def load(hp):
                    rows = slice(hp * LANES, (hp + 1) * LANES)
                    return kblk_ref[rows, :].astype(BF16), vblk_ref[rows, :].astype(BF16)

                return st[0] - 1, sweep(load, True, q0, 0, bp, False, False, w)

            lax.while_loop(cond, older_body, (jnp.int32((n_pre - pw) // bp - 1), go))
        o_ref[0, pl.ds(q0, bq), :] = acc_ref[w].astype(o_ref.dtype)

    def q_group(g, carry):
        started = [first_sweep(g * ways + w, w) for w in range(ways)]
        for w in range(ways):
            finish(w, *started[w])
        return carry

    lax.fori_loop(0, tq // bq // ways, q_group, 0)


def _sb_attention(q, k_pre, v_pre, k_self, v_self, cached=False):
    b, tq, w = q.shape
    n_pre = k_pre.shape[2] if cached else k_pre.shape[1]
    bq = min(128, tq)
    bp = min(128, n_pre)
    ways = 2 if (tq // bq) % 2 == 0 else 1
    bat = lambda i: (i, 0, 0)
    scratch = [pltpu.VMEM((ways, bq, w), F32), pltpu.VMEM((ways, SB_HEADS // 2, 2 * bq, LANES), F32)]
    if cached:
        pre_spec = pl.BlockSpec(memory_space=pl.ANY)
        pw = min(2 * bp, n_pre)
        scratch += [pltpu.VMEM((2, w, pw), F32), pltpu.VMEM((2, w, pw), F32),
                    pltpu.VMEM((w, bp), F32), pltpu.VMEM((w, bp), F32),
                    pltpu.SemaphoreType.DMA((2, 2)), pltpu.SemaphoreType.DMA((2,))]
    else:
        pre_map = (lambda i: (i, 0, 0)) if k_pre.shape[0] == b else (lambda i: (0, 0, 0))
        pre_spec = pl.BlockSpec((1, n_pre, w), pre_map)
    kern = functools.partial(_sb_kernel, tq=tq, n_pre=n_pre, bq=bq, bp=bp, ways=ways, cached=cached)
    return pl.pallas_call(
        kern,
        grid=(b,),
        in_specs=[pl.BlockSpec((1, tq, w), bat), pre_spec, pre_spec,
                  pl.BlockSpec((1, tq, w), bat), pl.BlockSpec((1, tq, w), bat)],
        out_specs=pl.BlockSpec((1, tq, w), bat),
        out_shape=jax.ShapeDtypeStruct((b, tq, w), BF16),
        scratch_shapes=scratch,
        compiler_params=_cparams(("arbitrary",) if cached else ("parallel",)),
        name="sb_attention",
    )(q, k_pre, v_pre, k_self, v_self)


def _df_kernel(q_ref, kp_ref, vp_ref, ks_ref, vs_ref, lq1_ref, lk1_ref, lq2_ref, lk2_ref, g_ref,
               o_ref, vxp_ref, vxs_ref, qm_ref, mx_ref, acc_ref, *, tq, n_pre, bq, bp, lam_init, cached):
    lane = lax.broadcasted_iota(jnp.int32, (bq, LANES), 1)
    lo = lane < HEAD_DIM
    contract_last = (((1,), (1,)), ((), ()))
    lam = (jnp.exp(jnp.sum(lq1_ref[...] * lk1_ref[...], axis=-1, keepdims=True))
           - jnp.exp(jnp.sum(lq2_ref[...] * lk2_ref[...], axis=-1, keepdims=True)) + lam_init)

    def extend_values(v_ref, vx_ref, head_rows):
        n = vx_ref.shape[0]
        ones = jnp.ones((n, LANES), BF16)
        for h in range(DF_HEADS):
            if head_rows:
                v_h = v_ref[0, pl.ds(h, n, stride=DF_HEADS), :]
            else:
                v_h = v_ref[0, :, h * LANES:(h + 1) * LANES]
            vx_ref[:, 2 * h * LANES:(2 * h + 1) * LANES] = v_h.astype(BF16)
            vx_ref[:, (2 * h + 1) * LANES:(2 * h + 2) * LANES] = ones

    extend_values(vp_ref, vxp_ref, cached)
    extend_values(vs_ref, vxs_ref, False)

    def half_sums(x):
        in_lo = lax.broadcasted_iota(jnp.int32, x.shape, 1) < HEAD_DIM
        return (jnp.sum(jnp.where(in_lo, x, 0.0), axis=1, keepdims=True),
                jnp.sum(jnp.where(in_lo, 0.0, x), axis=1, keepdims=True))

    kmax2 = []
    for h in range(DF_HEADS):
        ks = ks_ref[0, :, h * LANES:(h + 1) * LANES].astype(BF16).astype(F32)
        halves = [jnp.max(v, axis=0, keepdims=True) for v in half_sums(ks * ks)]
        if cached:
            kp = kp_ref[0, h * LANES:(h + 1) * LANES, :].astype(BF16).astype(F32)
            sq = kp * kp
            pre = [jnp.max(jnp.sum(sq[m * HEAD_DIM:(m + 1) * HEAD_DIM], axis=0, keepdims=True), axis=1, keepdims=True)
                   for m in range(2)]
        else:
            kp = kp_ref[0, :, h * LANES:(h + 1) * LANES].astype(BF16).astype(F32)
            pre = [jnp.max(v, axis=0, keepdims=True) for v in half_sums(kp * kp)]
        kmax2 += [jnp.maximum(halves[m], pre[m]) for m in range(2)]

    def scores(h, k_ref, k0, n, chunk_mask):
        if cached and k_ref is kp_ref:
            k2 = k_ref[0, h * LANES:(h + 1) * LANES, pl.ds(k0, n)].astype(BF16)
            s = jnp.dot(qm_ref[h], k2, preferred_element_type=F32)
        else:
            k2 = k_ref[0, pl.ds(k0, n), h * LANES:(h + 1) * LANES].astype(BF16)
            s = lax.dot_general(qm_ref[h], k2, contract_last, preferred_element_type=F32)
        if chunk_mask:
            r = lax.broadcasted_iota(jnp.int32, (2 * bq, n), 0)
            r = jnp.where(r >= bq, r - bq, r)
            col = lax.broadcasted_iota(jnp.int32, (2 * bq, n), 1)
            s = jnp.where((col // CHUNK) <= (r // CHUNK), s, -jnp.inf)
        return s

    def max_tile(k_ref, k0, n, chunk_mask):
        for h in range(DF_HEADS):
            s = scores(h, k_ref, k0, n, chunk_mask)
            if n % LANES == 0:
                m = s[:, :LANES]
                for j in range(1, n // LANES):
                    m = jnp.maximum(m, s[:, j * LANES:(j + 1) * LANES])
            else:
                m = jnp.broadcast_to(jnp.max(s, axis=1, keepdims=True), (2 * bq, LANES))
            mx_ref[h] = jnp.maximum(mx_ref[h], m)

    def value_tile(k_ref, vx_ref, k0, n, chunk_mask):
        for h in range(DF_HEADS):
            s = scores(h, k_ref, k0, n, chunk_mask)
            row_max = mx_ref[h]
            if n % LANES == 0:
                p = jnp.concatenate([jnp.exp(s[:, j * LANES:(j + 1) * LANES] - row_max)
                                     for j in range(n // LANES)], axis=1)
            else:
                p = jnp.exp(s - row_max[:, :n])
            vx = vx_ref[pl.ds(k0, n), 2 * h * LANES:(2 * h + 2) * LANES]
            acc_ref[h] += jnp.dot(p.astype(BF16), vx, preferred_element_type=F32)

    def q_block(qi, carry):
        q0 = pl.multiple_of(qi * bq, bq)
        for h in range(DF_HEADS):
            q2 = q_ref[0, pl.ds(q0, bq), h * LANES:(h + 1) * LANES]
            zero = jnp.zeros_like(q2)
            qm_ref[h] = jnp.concatenate([jnp.where(lo, q2, zero), jnp.where(lo, zero, q2)], axis=0)
        acc_ref[...] = jnp.zeros_like(acc_ref)

        gap = None
        for h in range(DF_HEADS):
            qf = q_ref[0, pl.ds(q0, bq), h * LANES:(h + 1) * LANES].astype(F32)
            kf = ks_ref[0, pl.ds(q0, bq), h * LANES:(h + 1) * LANES].astype(BF16).astype(F32)
            qn2 = half_sums(qf * qf)
            own = half_sums(qf * kf)
            for m in range(2):
                bound = jnp.sqrt(qn2[m] * kmax2[2 * h + m]) * 1.001 + 1e-6
                mx_ref[h, m * bq:(m + 1) * bq, :] = jnp.broadcast_to(bound, (bq, LANES))
                g = jnp.max(bound - own[m])
                gap = g if gap is None else jnp.maximum(gap, g)
        bound_is_tight = gap < SOFTMAX_SHIFT_SLACK

        def sweep(pre_fn, self_fn):
            def pre_body(j, c):
                pre_fn(pl.multiple_of(j * bp, bp))
                return c

            def self_body(j, c):
                self_fn(pl.multiple_of(j * bq, bq), False)
                return c

            lax.fori_loop(0, n_pre // bp, pre_body, 0)
            lax.fori_loop(0, qi, self_body, 0)
            self_fn(q0, True)

        @pl.when(jnp.logical_not(bound_is_tight))
        def _():
            mx_ref[...] = jnp.full_like(mx_ref, -jnp.inf)
            sweep(lambda k0: max_tile(kp_ref, k0, bp, False),
                  lambda k0, msk: max_tile(ks_ref, k0, bq, msk))
            for h in range(DF_HEADS):
                mx_ref[h] = jnp.broadcast_to(jnp.max(mx_ref[h], axis=1, keepdims=True), (2 * bq, LANES))

        sweep(lambda k0: value_tile(kp_ref, vxp_ref, k0, bp, False),
              lambda k0, msk: value_tile(ks_ref, vxs_ref, k0, bq, msk))

        for h in range(DF_HEADS):
            a0 = acc_ref[h, :bq, :]
            a1 = acc_ref[h, bq:, :]
            o = a0[:, :LANES] / a0[:, LANES:] - lam * (a1[:, :LANES] / a1[:, LANES:])
            y = o * lax.rsqrt(jnp.mean(o * o, axis=-1, keepdims=True) + EPS) * g_ref[...]
            o_ref[0, pl.ds(q0, bq), h * LANES:(h + 1) * LANES] = (y * (1.0 - lam_init)).astype(o_ref.dtype)
        return carry

    lax.fori_loop(0, tq // bq, q_block, 0)


def _df_attention(q, k_pre, v_pre, k_self, v_self, lq1, lk1, lq2, lk2, subln_g, lam_init, cached=False):
    b, tq, w = q.shape
    n_pre = k_pre.shape[2] if cached else k_pre.shape[1]
    bq = min(256, tq)
    bp = min(256, n_pre)
    pre_map = (lambda i: (i, 0, 0)) if k_pre.shape[0] == b else (lambda i: (0, 0, 0))
    bat = lambda i: (i, 0, 0)
    fixed = lambda i: (0, 0)
    kern = functools.partial(_df_kernel, tq=tq, n_pre=n_pre, bq=bq, bp=bp, lam_init=lam_init, cached=cached)
    kp_spec = pl.BlockSpec((1,) + k_pre.shape[1:], pre_map)
    vp_spec = pl.BlockSpec((1,) + v_pre.shape[1:], pre_map)
    vec = pl.BlockSpec((1, HEAD_DIM), fixed)
    return pl.pallas_call(
        kern,
        grid=(b,),
        in_specs=[pl.BlockSpec((1, tq, w), bat),
                  kp_spec, vp_spec,
                  pl.BlockSpec((1, tq, w), bat), pl.BlockSpec((1, tq, w), bat),
                  vec, vec, vec, vec, pl.BlockSpec((1, LANES), fixed)],
        out_specs=pl.BlockSpec((1, tq, w), bat),
        out_shape=jax.ShapeDtypeStruct((b, tq, w), BF16),
        scratch_shapes=[pltpu.VMEM((n_pre, 2 * w), BF16), pltpu.VMEM((tq, 2 * w), BF16),
                        pltpu.VMEM((DF_HEADS, 2 * bq, LANES), BF16), pltpu.VMEM((DF_HEADS, 2 * bq, LANES), F32),
                        pltpu.VMEM((DF_HEADS, 2 * bq, 2 * LANES), F32)],
        compiler_params=_cparams(("parallel",)),
        name="df_attention",
    )(q, k_pre, v_pre, k_self, v_self, lq1, lk1, lq2, lk2, subln_g)


def _outproj_kernel(x_ref, sb_ref, df_ref, wsb_ref, wdf_ref, g_ref, wr_ref, br_ref,
                    h_ref, route_ref):
    h = (x_ref[...]
         + jnp.dot(sb_ref[...], wsb_ref[...], preferred_element_type=F32)
         + jnp.dot(df_ref[...], wdf_ref[...], preferred_element_type=F32))
    h_ref[...] = h
    xn = h * lax.rsqrt(jnp.mean(h * h, axis=-1, keepdims=True) + EPS) * g_ref[...]
    logits = jnp.dot(xn.astype(BF16), wr_ref[...], preferred_element_type=F32) + br_ref[...]
    lane = lax.broadcasted_iota(jnp.int32, logits.shape, 1)
    big = jnp.int32(LANES)
    neg = -jnp.inf

    def first_argmax(v):
        mx = jnp.max(v, axis=1, keepdims=True)
        idx = jnp.min(jnp.where(v == mx, lane, big), axis=1, keepdims=True)
        return mx, idx

    gl = jnp.where((lane >= N_EXPERTS) & (lane < N_EXPERTS + N_GROUPS), logits, neg)
    gmax, gidx = first_argmax(gl)
    g_w = 1.0 / jnp.sum(jnp.exp(gl - gmax), axis=1, keepdims=True)
    grp = gidx - N_EXPERTS
    el = jnp.where((lane < N_EXPERTS) & (lane // EXPERTS_PER_GROUP == grp), logits, neg)
    m1, i1 = first_argmax(el)
    m2, i2 = first_argmax(jnp.where(lane == i1, neg, el))
    e21 = jnp.exp(m2 - m1)
    t1 = 1.0 / (1.0 + e21)
    t2 = e21 / (1.0 + e21)
    route = jnp.where(lane == 0, i1.astype(F32),
            jnp.where(lane == 1, i2.astype(F32),
            jnp.where(lane == 2, g_w * t1,
            jnp.where(lane == 3, g_w * t2, 0.0))))
    route_ref[...] = route


def _outproj_route(x2d, sb, df, wsb, wdf, g, wr, br, tm):
    n, d = x2d.shape
    row = lambda i: (i, 0)
    fixed = lambda i: (0, 0)
    return pl.pallas_call(
        _outproj_kernel,
        grid=(n // tm,),
        in_specs=[pl.BlockSpec((tm, d), row), pl.BlockSpec((tm, GROUP_W), row), pl.BlockSpec((tm, GROUP_W), row),
                  pl.BlockSpec(wsb.shape, fixed), pl.BlockSpec(wdf.shape, fixed), pl.BlockSpec((1, d), fixed),
                  pl.BlockSpec(wr.shape, fixed), pl.BlockSpec((1, LANES), fixed)],
        out_specs=[pl.BlockSpec((tm, d), row), pl.BlockSpec((tm, LANES), row)],
        out_shape=[jax.ShapeDtypeStruct((n, d), F32), jax.ShapeDtypeStruct((n, LANES), F32)],
        compiler_params=_cparams(("parallel",)),
        name="outproj_route",
    )(x2d, sb, df, wsb, wdf, g, wr, br)


def _row_copy(src, s, dst, d, sem):
    return pltpu.make_async_copy(src.at[pl.ds(s, 1)], dst.at[pl.ds(d, 1)], sem)


def _dispatch_kernel(d0_ref, d1_ref, hp_ref, hs_ref, xs_hbm, sem, *, tm, n_prompt_tiles):
    def issue_from(src_ref):
        def issue(r, c):
            _row_copy(src_ref, r, xs_hbm, d0_ref[0, 0, r], sem).start(priority=0)
            _row_copy(src_ref, r, xs_hbm, d1_ref[0, 0, r], sem).start(priority=1)
            return c

        lax.fori_loop(0, tm, issue, 0, unroll=8)

    is_prompt = pl.program_id(0) < n_prompt_tiles
    pl.when(is_prompt)(lambda: issue_from(hp_ref))
    pl.when(jnp.logical_not(is_prompt))(lambda: issue_from(hs_ref))
    for _ in range(2):
        pltpu.make_async_copy(hp_ref, xs_hbm.at[pl.ds(0, tm)], sem).wait()


def _dispatch(dest0, dest1, h_prompt, h_sample, tm):
    n_p, d = h_prompt.shape
    n = n_p + h_sample.shape[0]
    npt = n_p // tm
    idx = lambda i: (i, 0, 0)
    smem = functools.partial(pl.BlockSpec, memory_space=pltpu.SMEM)
    return pl.pallas_call(
        functools.partial(_dispatch_kernel, tm=tm, n_prompt_tiles=npt),
        grid=(n // tm,),
        in_specs=[smem((1, 1, tm), idx), smem((1, 1, tm), idx),
                  pl.BlockSpec((tm, d), lambda i: (jnp.minimum(i, npt - 1), 0)),
                  pl.BlockSpec((tm, d), lambda i: (jnp.maximum(i - npt, 0), 0))],
        out_specs=pl.BlockSpec(memory_space=pl.ANY),
        out_shape=jax.ShapeDtypeStruct((2 * n, d), F32),
        scratch_shapes=[pltpu.SemaphoreType.DMA(())],
        compiler_params=_cparams(("arbitrary",)),
        name="moe_dispatch",
    )(dest0.reshape(n // tm, 1, tm), dest1.reshape(n // tm, 1, tm), h_prompt, h_sample)


def _expert_kernel(vt_ref, ve_ref, seg_ref, nv_ref, h_ref, g_ref, wgu_ref, wd_ref, y_ref, *, de, tm):
    v = pl.program_id(0)

    @pl.when(v < nv_ref[0])
    def _():
        e = ve_ref[v]
        t = vt_ref[v]
        h_in = h_ref[...]
        x = (h_in * lax.rsqrt(jnp.mean(h_in * h_in, axis=-1, keepdims=True) + EPS) * g_ref[...]).astype(BF16)
        gu = jnp.dot(x, wgu_ref[0], preferred_element_type=F32)
        g = gu[:, :de]
        h = (g * (1.0 / (1.0 + jnp.exp(-g)))) * gu[:, de:]
        y = jnp.dot(h.astype(BF16), wd_ref[0], preferred_element_type=F32)
        row = t * tm + lax.broadcasted_iota(jnp.int32, (tm, 1), 0)
        mine = (row >= seg_ref[e]) & (row < seg_ref[N_EXPERTS + e])
        first_visit = (v == 0) | (vt_ref[jnp.maximum(v - 1, 0)] != t)

        @pl.when(first_visit)
        def _():
            y_ref[...] = jnp.where(mine, y, 0.0)

        @pl.when(jnp.logical_not(first_visit))
        def _():
            y_ref[...] = jnp.where(mine, y, y_ref[...])


def _experts(visit_tile, visit_expert, segments, n_visits, xs, g, wgu, wd, tm):
    p, d = xs.shape
    de = wd.shape[1]
    grid_spec = pltpu.PrefetchScalarGridSpec(
        num_scalar_prefetch=4,
        grid=(visit_tile.shape[0],),
        in_specs=[pl.BlockSpec((tm, d), lambda v, vt, ve, sg, nv: (vt[v], 0)),
                  pl.BlockSpec((1, d), lambda v, vt, ve, sg, nv: (0, 0)),
                  pl.BlockSpec((1, d, 2 * de), lambda v, vt, ve, sg, nv: (ve[v], 0, 0)),
                  pl.BlockSpec((1, de, d), lambda v, vt, ve, sg, nv: (ve[v], 0, 0))],
        out_specs=pl.BlockSpec((tm, d), lambda v, vt, ve, sg, nv: (vt[v], 0)),
    )
    return pl.pallas_call(
        functools.partial(_expert_kernel, de=de, tm=tm),
        grid_spec=grid_spec,
        out_shape=jax.ShapeDtypeStruct((p, d), F32),
        compiler_params=_cparams(("arbitrary",)),
        name="moe_experts",
    )(visit_tile, visit_expert, segments, n_visits, xs, g, wgu, wd)


def _combine_kernel(d0_ref, d1_ref, h_ref, gate_ref, g_ref, ys_hbm, o_ref, y0_ref, y1_ref, sem, *, tm):
    def issue(r, c):
        _row_copy(ys_hbm, d0_ref[0, 0, r], y0_ref, r, sem).start(priority=0)
        _row_copy(ys_hbm, d1_ref[0, 0, r], y1_ref, r, sem).start(priority=1)
        return c

    lax.fori_loop(0, tm, issue, 0, unroll=8)
    pltpu.make_async_copy(ys_hbm.at[pl.ds(0, tm)], y0_ref, sem).wait()
    pltpu.make_async_copy(ys_hbm.at[pl.ds(0, tm)], y1_ref, sem).wait()
    gate = gate_ref[...]
    h = h_ref[...] + (y0_ref[...] * gate[:, 2:3] + y1_ref[...] * gate[:, 3:4])
    o_ref[...] = h * lax.rsqrt(jnp.mean(h * h, axis=-1, keepdims=True) + EPS) * g_ref[...]


def _combine(dest0, dest1, h2d, route, g, ys, tm):
    n, d = h2d.shape
    idx = lambda i: (i, 0, 0)
    row = lambda i: (i, 0)
    smem = functools.partial(pl.BlockSpec, memory_space=pltpu.SMEM)
    return pl.pallas_call(
        functools.partial(_combine_kernel, tm=tm),
        grid=(n // tm,),
        in_specs=[smem((1, 1, tm), idx), smem((1, 1, tm), idx),
                  pl.BlockSpec((tm, d), row), pl.BlockSpec((tm, LANES), row),
                  pl.BlockSpec((1, d), lambda i: (0, 0)), pl.BlockSpec(memory_space=pl.ANY)],
        out_specs=pl.BlockSpec((tm, d), row),
        out_shape=jax.ShapeDtypeStruct((n, d), F32),
        scratch_shapes=[pltpu.VMEM((tm, d), F32), pltpu.VMEM((tm, d), F32), pltpu.SemaphoreType.DMA(())],
        compiler_params=_cparams(("arbitrary",)),
        name="moe_combine",
    )(dest0.reshape(n // tm, 1, tm), dest1.reshape(n // tm, 1, tm), h2d, route, g, ys)


def _routing_plan(route, tm):
    t = route.shape[0]
    e = route[:, :2].astype(jnp.int32).reshape(-1)
    onehot = (e[:, None] == jnp.arange(N_EXPERTS, dtype=jnp.int32)[None, :])
    blk = 512
    oh = onehot.astype(BF16).reshape(2 * t // blk, blk, N_EXPERTS)
    tri = (jnp.arange(blk)[:, None] > jnp.arange(blk)[None, :]).astype(BF16)
    within = jnp.einsum("ij,bjk->bik", tri, oh, preferred_element_type=F32)
    totals = jnp.sum(oh.astype(F32), axis=1)
    before = jnp.cumsum(totals, axis=0) - totals
    rank_all = (within + before[:, None, :]).reshape(2 * t, N_EXPERTS)
    rank = jnp.sum(jnp.where(onehot, rank_all, 0.0), axis=1).astype(jnp.int32)
    counts = jnp.sum(totals, axis=0).astype(jnp.int32)
    ends = jnp.cumsum(counts)
    starts = ends - counts
    dest = (starts[e] + rank).reshape(t, 2)
    first_tile = starts // tm
    n_vis = jnp.where(counts > 0, (ends - 1) // tm - first_tile + 1, 0)
    vis_end = jnp.cumsum(n_vis)
    vis_start = vis_end - n_vis
    n_visits = vis_end[-1]
    max_visits = 2 * t // tm + N_EXPERTS - 1
    v = jnp.minimum(jnp.arange(max_visits, dtype=jnp.int32), n_visits - 1)
    visit_expert = jnp.minimum(jnp.sum(v[:, None] >= vis_end[None, :], axis=1), N_EXPERTS - 1).astype(jnp.int32)
    visit_tile = (first_tile[visit_expert] + v - vis_start[visit_expert]).astype(jnp.int32)
    segments = jnp.concatenate([starts, ends]).astype(jnp.int32)
    return dest[:, 0], dest[:, 1], visit_tile, visit_expert, segments, n_visits.reshape(1).astype(jnp.int32)


def _pick(n, prefs):
    for p in prefs:
        if n % p == 0:
            return p
    return n


def kernel(x_prompt, x_sample, cache_sb_k, cache_sb_v, cache_diff_k, cache_diff_v, meta_tokens, norm_mix_g, w_in, lambda_q1, lambda_k1, lambda_q2, lambda_k2, subln_g, w_out, norm_ffn_g, w_group, b_group, w_router, b_router, w_gate, w_up, w_down, final_norm_g):
    depth = w_in.shape[0]
    assert depth == 1, "meta-token rows are only dropped after a single layer"
    b, s, d = x_prompt.shape
    bs, ss, _ = x_sample.shape
    past = cache_sb_k.shape[2]
    n_meta = meta_tokens.shape[0]
    lam_init = 0.8 - 0.6 * math.exp(-0.3 * 0)

    g_mix = norm_mix_g[0].reshape(1, d)
    w_in_bf = w_in[0].astype(BF16)
    pos_p = n_meta + jnp.arange(s, dtype=jnp.int32)
    pos_s = past + jnp.arange(ss, dtype=jnp.int32)
    pos_m = jnp.arange(n_meta, dtype=jnp.int32)
    xp2 = x_prompt.reshape(b * s, d)
    xs2 = x_sample.reshape(bs * ss, d)

    tm_p = _pick(s, (512, 256, 128))
    (qsb_p, qdf_p, ksb_p, vsb_p, kdf_p, vdf_p,
     ksb_pb, vsb_pb, kdf_pb, vdf_pb) = _inproj(xp2, g_mix, w_in_bf, pos_p, tm_p, feature_major=True)
    tm_s = _pick(bs * ss, (512, 256, 128))
    (qsb_s, qdf_s, ksb_s, vsb_s, kdf_s, vdf_s,
     ksb_sb, vsb_sb, kdf_sb, vdf_sb) = _inproj(xs2, g_mix, w_in_bf, pos_s, tm_s)
    (_, _, ksb_m, vsb_m, kdf_m, vdf_m,
     ksb_mb, vsb_mb, kdf_mb, vdf_mb) = _inproj(meta_tokens, g_mix, w_in_bf, pos_m, n_meta)

    r3 = lambda a, bb, t: a.reshape(bb, t, GROUP_W)
    lam_vecs = [v[0].reshape(1, HEAD_DIM) for v in (lambda_q1, lambda_k1, lambda_q2, lambda_k2)]
    sub_g = subln_g[0].reshape(1, LANES)

    sb_p = _sb_attention(r3(qsb_p, b, s), ksb_mb[None], vsb_mb[None], r3(ksb_pb, b, s), r3(vsb_pb, b, s))
    df_p = _df_attention(r3(qdf_p, b, s), kdf_mb[None], vdf_mb[None], r3(kdf_pb, b, s), r3(vdf_pb, b, s),
                         *lam_vecs, sub_g, lam_init)
    feature_major = lambda c: jnp.transpose(c[0], (0, 2, 3, 1)).reshape(bs, GROUP_W, past)
    sb_s = _sb_attention(r3(qsb_s, bs, ss), feature_major(cache_sb_k), feature_major(cache_sb_v),
                         r3(ksb_sb, bs, ss), r3(vsb_sb, bs, ss), cached=True)
    df_s = _df_attention(r3(qdf_s, bs, ss), feature_major(cache_diff_k),
                         cache_diff_v[0].reshape(bs, past * DF_HEADS, LANES), r3(kdf_sb, bs, ss), r3(vdf_sb, bs, ss),
                         *lam_vecs, sub_g, lam_init, cached=True)

    w_out_bf = w_out[0].astype(BF16)
    wsb, wdf = w_out_bf[:GROUP_W], w_out_bf[GROUP_W:]
    g_ffn = norm_ffn_g[0].reshape(1, d)
    pad_cols = LANES - N_EXPERTS - N_GROUPS
    wr = jnp.concatenate([w_router[0], w_group[0], jnp.zeros((d, pad_cols), F32)], axis=1).astype(BF16)
    br = jnp.concatenate([b_router[0], b_group[0], jnp.zeros((pad_cols,), F32)]).reshape(1, LANES)

    h_p, route_p = _outproj_route(xp2, sb_p.reshape(b * s, GROUP_W), df_p.reshape(b * s, GROUP_W),
                                        wsb, wdf, g_ffn, wr, br, tm_p)
    h_s, route_s = _outproj_route(xs2, sb_s.reshape(bs * ss, GROUP_W), df_s.reshape(bs * ss, GROUP_W),
                                        wsb, wdf, g_ffn, wr, br, tm_s)

    tm_e = 512
    n_p = b * s
    tm_d = _pick(math.gcd(n_p, bs * ss), (512, 256, 128))
    route = jnp.concatenate([route_p, route_s], axis=0)
    dest0, dest1, visit_tile, visit_expert, segments, n_visits = _routing_plan(route, tm_e)
    xsorted = _dispatch(dest0, dest1, h_p, h_s, tm_d)
    wgu = jnp.concatenate([w_gate[0], w_up[0]], axis=-1).astype(BF16)
    ysorted = _experts(visit_tile, visit_expert, segments, n_visits, xsorted, g_ffn, wgu,
                       w_down[0].astype(BF16), tm_e)
    g_fin = final_norm_g.reshape(1, d)
    y_p = _combine(dest0[:n_p], dest1[:n_p], h_p, route_p, g_fin, ysorted, tm_d)
    y_s = _combine(dest0[n_p:], dest1[n_p:], h_s, route_s, g_fin, ysorted, tm_d)

    def with_meta(meta_rows, frames, heads, hd):
        m = jnp.broadcast_to(meta_rows[None], (b, n_meta, GROUP_W))
        return jnp.concatenate([m, frames.reshape(b, s, GROUP_W)], axis=1).reshape(1, b, n_meta + s, heads, hd)

    def with_meta_t(meta_rows, frames_t, heads, hd):
        m = jnp.broadcast_to(meta_rows.T[None], (b, GROUP_W, n_meta))
        full = jnp.concatenate([m, frames_t], axis=2).reshape(b, heads, hd, n_meta + s)
        return jnp.transpose(full, (0, 3, 1, 2))[None]

    shp = lambda a, heads, hd: a.reshape(1, bs, ss, heads, hd)
    return (y_p.reshape(b, s, d), y_s.reshape(bs, ss, d),
            with_meta_t(ksb_m, ksb_p, SB_HEADS, HEAD_DIM), with_meta_t(vsb_m, vsb_p, SB_HEADS, HEAD_DIM),
            with_meta_t(kdf_m, kdf_p, 2 * DF_HEADS, HEAD_DIM), with_meta(vdf_m, vdf_p, DF_HEADS, 2 * HEAD_DIM),
            shp(ksb_s, SB_HEADS, HEAD_DIM), shp(vsb_s, SB_HEADS, HEAD_DIM),
            shp(kdf_s, 2 * DF_HEADS, HEAD_DIM), shp(vdf_s, DF_HEADS, 2 * HEAD_DIM))
```

```python
import functools
import math

import jax
import jax.numpy as jnp
from jax import lax
from jax.experimental import pallas as pl
from jax.experimental.pallas import tpu as pltpu

F32 = jnp.float32
BF16 = jnp.bfloat16

CHUNK = 64
SB_HEADS = 8
DF_HEADS = 4
HEAD_DIM = 64
GROUP_W = 512
ROPE_THETA = 500000.0
ROT_DIM = 16
N_GROUPS = 4
EXPERTS_PER_GROUP = 8
N_EXPERTS = N_GROUPS * EXPERTS_PER_GROUP
EPS = 1e-6
LANES = 128
SB_CUTOFF = -88.0
VMEM_LIMIT = 56 * 1024 * 1024
SOFTMAX_SHIFT_SLACK = 80.0
FOLDED_PREFIX_ROWS = 128


def _cparams(sem):
    return pltpu.CompilerParams(dimension_semantics=sem, vmem_limit_bytes=VMEM_LIMIT)


def _inproj_kernel(x_ref, g_ref, w_ref, cos_ref, sin_ref,
                   qsb_ref, qdf_ref, ksb_ref, vsb_ref, kdf_ref, vdf_ref,
                   ksb_bf_ref, vsb_bf_ref, kdf_bf_ref, vdf_bf_ref, *, feature_major):
    x = x_ref[...]
    ms = jnp.mean(x * x, axis=-1, keepdims=True)
    xn = (x * lax.rsqrt(ms + EPS) * g_ref[...]).astype(BF16)
    proj = jnp.dot(xn, w_ref[...], preferred_element_type=F32)
    cos = cos_ref[...]
    sin = sin_ref[...]
    lane = lax.broadcasted_iota(jnp.int32, cos.shape, 1)
    first_half = (lane % ROT_DIM) < (ROT_DIM // 2)

    def rope(a):
        cols = []
        for j in range(GROUP_W // LANES):
            aj = a[:, j * LANES:(j + 1) * LANES]
            partner = jnp.where(first_half,
                                pltpu.roll(aj, LANES - ROT_DIM // 2, 1),
                                pltpu.roll(aj, ROT_DIM // 2, 1))
            cols.append(aj * cos + partner * sin)
        return jnp.concatenate(cols, axis=1)

    scale = HEAD_DIM ** -0.5
    qsb_ref[...] = (proj[:, 0:GROUP_W] * scale).astype(BF16)
    qdf_ref[...] = (rope(proj[:, 3 * GROUP_W:4 * GROUP_W]) * scale).astype(BF16)
    kdf = rope(proj[:, 4 * GROUP_W:5 * GROUP_W])
    for f32_ref, bf_ref, val in ((ksb_ref, ksb_bf_ref, proj[:, GROUP_W:2 * GROUP_W]),
                                 (vsb_ref, vsb_bf_ref, proj[:, 2 * GROUP_W:3 * GROUP_W]),
                                 (kdf_ref, kdf_bf_ref, kdf),
                                 (vdf_ref, vdf_bf_ref, proj[:, 5 * GROUP_W:6 * GROUP_W])):
        if feature_major and f32_ref is not vdf_ref:
            f32_ref[0] = val.T
        else:
            f32_ref[...] = val
        bf_ref[...] = val.astype(BF16)


def _rope_tables(pos):
    half = ROT_DIM // 2
    inv = ROPE_THETA ** (-jnp.arange(half, dtype=F32) * 2.0 / ROT_DIM)
    ang = pos.astype(F32)[:, None] * inv[None, :]
    cos, sin = jnp.cos(ang), jnp.sin(ang)
    n = pos.shape[0]
    pad = HEAD_DIM - ROT_DIM
    cos_h = jnp.concatenate([cos, cos, jnp.ones((n, pad), F32)], axis=1)
    sin_h = jnp.concatenate([-sin, sin, jnp.zeros((n, pad), F32)], axis=1)
    reps = LANES // HEAD_DIM
    return jnp.tile(cos_h, (1, reps)), jnp.tile(sin_h, (1, reps))


def _inproj(x2d, g, w_bf, pos, tm, feature_major=False):
    n, d = x2d.shape
    cos_t, sin_t = _rope_tables(pos)
    period = pos.shape[0]
    if period < tm:
        cos_t = jnp.tile(cos_t, (tm // period, 1))
        sin_t = jnp.tile(sin_t, (tm // period, 1))
    ntab = cos_t.shape[0] // tm
    row = lambda i: (i, 0)
    fixed = lambda i: (0, 0)
    tab = lambda i: (i % ntab, 0)
    out_spec = pl.BlockSpec((tm, GROUP_W), row)
    f32_spec, f32_shape = out_spec, jax.ShapeDtypeStruct((n, GROUP_W), F32)
    if feature_major:
        f32_spec = pl.BlockSpec((1, GROUP_W, tm), lambda i: (i // ntab, 0, i % ntab))
        f32_shape = jax.ShapeDtypeStruct((n // period, GROUP_W, period), F32)
    return pl.pallas_call(
        functools.partial(_inproj_kernel, feature_major=feature_major),
        grid=(n // tm,),
        in_specs=[pl.BlockSpec((tm, d), row), pl.BlockSpec((1, d), fixed),
                  pl.BlockSpec(w_bf.shape, fixed),
                  pl.BlockSpec((tm, LANES), tab), pl.BlockSpec((tm, LANES), tab)],
        out_specs=[out_spec] * 2 + [f32_spec] * 3 + [out_spec] * 5,
        out_shape=[jax.ShapeDtypeStruct((n, GROUP_W), BF16)] * 2
                  + [f32_shape] * 3 + [jax.ShapeDtypeStruct((n, GROUP_W), F32)]
                  + [jax.ShapeDtypeStruct((n, GROUP_W), BF16)] * 4,
        compiler_params=_cparams(("parallel",)),
        name="inproj",
    )(x2d, g, w_bf, cos_t, sin_t)


def _sb_kernel(q_ref, kp_ref, vp_ref, ks_ref, vs_ref, o_ref, acc_ref, car_ref, *cache_scratch,
               tq, n_pre, bq, bp, ways, cached):
    lane = lax.broadcasted_iota(jnp.int32, (bq, LANES), 1)
    lo = lane < HEAD_DIM
    contract_last = (((1,), (1,)), ((), ()))
    wn = 2 * bq if tq >= 2 * bq else bq
    step = pl.program_id(0)

    def neg_suffix_matrix(n):
        j = lax.broadcasted_iota(jnp.int32, (n, n + LANES), 0)
        s = lax.broadcasted_iota(jnp.int32, (n, n + LANES), 1)
        return jnp.where((j > s) | (s >= n), -1.0, 0.0).astype(BF16)

    suffixes = {n: neg_suffix_matrix(n) for n in {wn, bq, bp}}

    def row_major(k_ref, v_ref, k0, n):
        def load(hp):
            cols = slice(hp * LANES, (hp + 1) * LANES)
            return k_ref[0, pl.ds(k0, n), cols].astype(BF16), v_ref[0, pl.ds(k0, n), cols].astype(BF16)
        return load

    def sweep(load, feature_major, q0, koff, n, causal, first, w):
        suffix = suffixes[n]
        n_pairs = SB_HEADS // 2
        if causal:
            r = lax.broadcasted_iota(jnp.int32, (2 * bq, n), 0)
            c = lax.broadcasted_iota(jnp.int32, (2 * bq, n), 1)
            vis = (c + koff) < jnp.where(r >= bq, r - bq, r)
        log_sig, masked_sp, values = [], [], []
        for hp in range(n_pairs):
            cols = slice(hp * LANES, (hp + 1) * LANES)
            k2, v2 = load(hp)
            q2 = q_ref[0, pl.ds(q0, bq), cols]
            zero = jnp.zeros_like(q2)
            qs = jnp.concatenate([jnp.where(lo, q2, zero), jnp.where(lo, zero, q2)], axis=0)
            if feature_major:
                s = jnp.dot(qs, k2, preferred_element_type=F32)
            else:
                s = lax.dot_general(qs, k2, contract_last, preferred_element_type=F32)
            sp = jnp.maximum(s, 0.0) + jnp.log(1.0 + jnp.exp(-jnp.abs(s)))
            log_sig.append(s - sp)
            masked_sp.append((jnp.where(vis, sp, 0.0) if causal else sp).astype(BF16))
            values.append(v2)
        cr_all = jnp.dot(jnp.concatenate(masked_sp, axis=0), suffix, preferred_element_type=F32)
        worst = None
        for hp in range(n_pairs):
            cols = slice(hp * LANES, (hp + 1) * LANES)
            cr = cr_all[2 * bq * hp:2 * bq * (hp + 1)]
            between = cr[:, :n]
            car = cr[:, n:]
            if not first:
                old = car_ref[w, hp]
                between = between + (jnp.tile(old, (1, n // LANES)) if n % LANES == 0 else old[:, :n])
                car = car + old
            p = jnp.exp(log_sig[hp] + between)
            if causal:
                p = jnp.where(vis, p, 0.0)
            if feature_major:
                pv = lax.dot_general(p.astype(BF16), values[hp], contract_last, preferred_element_type=F32)
            else:
                pv = jnp.dot(p.astype(BF16), values[hp], preferred_element_type=F32)
            car_ref[w, hp] = car
            worst = car if worst is None else jnp.maximum(worst, car)
            both = jnp.where(lo, pv[:bq], pv[bq:])
            if first:
                acc_ref[w, :, cols] = both
            else:
                acc_ref[w, :, cols] += both
        return jnp.max(worst) > SB_CUTOFF

    if cached:
        kwin_ref, vwin_ref, kblk_ref, vblk_ref, win_sem, blk_sem = cache_scratch
        pw = min(2 * bp, n_pre)

        def window_copies(bi, slot):
            return [pltpu.make_async_copy(src.at[bi, :, pl.ds(n_pre - pw, pw)], dst.at[slot], win_sem.at[a, slot])
                    for a, (src, dst) in enumerate(((kp_ref, kwin_ref), (vp_ref, vwin_ref)))]

        def block_copies(k0):
            return [pltpu.make_async_copy(src.at[step, :, pl.ds(k0, bp)], dst, blk_sem.at[a])
                    for a, (src, dst) in enumerate(((kp_ref, kblk_ref), (vp_ref, vblk_ref)))]

        slot = step % 2

        @pl.when(step == 0)
        def _():
            for cp in window_copies(step, slot):
                cp.start()

        @pl.when(step + 1 < pl.num_programs(0))
        def _():
            for cp in window_copies(step + 1, 1 - slot):
                cp.start()

        for cp in window_copies(step, slot):
            cp.wait()

    def cond(st):
        return (st[0] >= 0) & st[1]

    def first_sweep(qi, w):
        q0 = pl.multiple_of(qi * bq, bq)
        first_blk = jnp.maximum(qi + 1 - wn // bq, 0)
        k0 = pl.multiple_of(first_blk * bq, bq)
        return q0, first_blk, sweep(row_major(ks_ref, vs_ref, k0, wn), False, q0, k0 - q0, wn, True, True, w)

    def finish(w, q0, first_blk, go):
        def self_body(st):
            kj = pl.multiple_of(st[0] * bq, bq)
            return st[0] - 1, sweep(row_major(ks_ref, vs_ref, kj, bq), False, q0, 0, bq, False, False, w)

        _, go = lax.while_loop(cond, self_body, (first_blk - 1, go))

        if not cached:
            def pre_body(st):
                kj = pl.multiple_of(st[0] * bp, bp)
                return st[0] - 1, sweep(row_major(kp_ref, vp_ref, kj, bp), False, q0, 0, bp, False, False, w)

            lax.while_loop(cond, pre_body, (jnp.int32(n_pre // bp - 1), go))
        else:
            def window_block(off):
                def load(hp):
                    rows = slice(hp * LANES, (hp + 1) * LANES)
                    return (kwin_ref[slot, rows, off:off + bp].astype(BF16),
                            vwin_ref[slot, rows, off:off + bp].astype(BF16))
                return load

            for off in range(pw - bp, -1, -bp):
                go = lax.cond(go, lambda off=off: sweep(window_block(off), True, q0, 0, bp, False, False, w),
                              lambda: jnp.zeros((), jnp.bool_))

            def older_body(st):
                for cp in block_copies(pl.multiple_of(st[0] * bp, bp)):
                    cp.start()
                for cp in block_copies(pl.multiple_of(st[0] * bp, bp)):
                    cp.wait()

                def load(hp):
                    rows = slice(hp * LANES, (hp + 1) * LANES)
                    return kblk_ref[rows, :].astype(BF16), vblk_ref[rows, :].astype(BF16)

                return st[0] - 1, sweep(load, True, q0, 0, bp, False, False, w)

            lax.while_loop(cond, older_body, (jnp.int32((n_pre - pw) // bp - 1), go))
        o_ref[0, pl.ds(q0, bq), :] = acc_ref[w].astype(o_ref.dtype)

    def q_group(g, carry):
        started = [first_sweep(g * ways + w, w) for w in range(ways)]
        for w in range(ways):
            finish(w, *started[w])
        return carry

    lax.fori_loop(0, tq // bq // ways, q_group, 0)


def _sb_attention(q, k_pre, v_pre, k_self, v_self, cached=False):
    b, tq, w = q.shape
    n_pre = k_pre.shape[2] if cached else k_pre.shape[1]
    bq = min(128, tq)
    bp = min(128, n_pre)
    ways = 2 if (tq // bq) % 2 == 0 else 1
    bat = lambda i: (i, 0, 0)
    scratch = [pltpu.VMEM((ways, bq, w), F32), pltpu.VMEM((ways, SB_HEADS // 2, 2 * bq, LANES), F32)]
    if cached:
        pre_spec = pl.BlockSpec(memory_space=pl.ANY)
        pw = min(2 * bp, n_pre)
        scratch += [pltpu.VMEM((2, w, pw), F32), pltpu.VMEM((2, w, pw), F32),
                    pltpu.VMEM((w, bp), F32), pltpu.VMEM((w, bp), F32),
                    pltpu.SemaphoreType.DMA((2, 2)), pltpu.SemaphoreType.DMA((2,))]
    else:
        pre_map = (lambda i: (i, 0, 0)) if k_pre.shape[0] == b else (lambda i: (0, 0, 0))
        pre_spec = pl.BlockSpec((1, n_pre, w), pre_map)
    kern = functools.partial(_sb_kernel, tq=tq, n_pre=n_pre, bq=bq, bp=bp, ways=ways, cached=cached)
    return pl.pallas_call(
        kern,
        grid=(b,),
        in_specs=[pl.BlockSpec((1, tq, w), bat), pre_spec, pre_spec,
                  pl.BlockSpec((1, tq, w), bat), pl.BlockSpec((1, tq, w), bat)],
        out_specs=pl.BlockSpec((1, tq, w), bat),
        out_shape=jax.ShapeDtypeStruct((b, tq, w), BF16),
        scratch_shapes=scratch,
        compiler_params=_cparams(("arbitrary",) if cached else ("parallel",)),
        name="sb_attention",
    )(q, k_pre, v_pre, k_self, v_self)


def _df_kernel(q_ref, kp_ref, vp_ref, ks_ref, vs_ref, lq1_ref, lk1_ref, lq2_ref, lk2_ref, g_ref,
               o_ref, vxp_ref, vxs_ref, qm_ref, mx_ref, acc_ref, *fold_scratch,
               tq, n_pre, bq, bp, lam_init, cached, fold):
    lane = lax.broadcasted_iota(jnp.int32, (bq, LANES), 1)
    lo = lane < HEAD_DIM
    contract_last = (((1,), (1,)), ((), ()))
    lam = (jnp.exp(jnp.sum(lq1_ref[...] * lk1_ref[...], axis=-1, keepdims=True))
           - jnp.exp(jnp.sum(lq2_ref[...] * lk2_ref[...], axis=-1, keepdims=True)) + lam_init)

    def extend_values(v_ref, vx_ref, head_rows):
        n = vx_ref.shape[0]
        ones = jnp.ones((n, LANES), BF16)
        for h in range(DF_HEADS):
            if head_rows:
                v_h = v_ref[0, pl.ds(h, n, stride=DF_HEADS), :]
            else:
                v_h = v_ref[0, :, h * LANES:(h + 1) * LANES]
            vx_ref[:, 2 * h * LANES:(2 * h + 1) * LANES] = v_h.astype(BF16)
            vx_ref[:, (2 * h + 1) * LANES:(2 * h + 2) * LANES] = ones

    extend_values(vp_ref, vxp_ref, cached)
    extend_values(vs_ref, vxs_ref, False)
    if fold:
        kpad_ref, vxpad_ref = fold_scratch
        kpad_ref[...] = jnp.zeros_like(kpad_ref)
        vxpad_ref[...] = jnp.zeros_like(vxpad_ref)
        kpad_ref[0:n_pre, :] = kp_ref[0].astype(BF16)
        vxpad_ref[0:n_pre, :] = vxp_ref[...]

    sel_row = lax.broadcasted_iota(jnp.int32, (LANES, 2 * LANES), 0) // HEAD_DIM
    sel_col = lax.broadcasted_iota(jnp.int32, (LANES, 2 * LANES), 1) // LANES
    half_selector = jnp.where(sel_row == sel_col, 1.0, 0.0).astype(BF16)

    def max_half_norm2(x):
        rep = jnp.dot((x * x).astype(BF16), half_selector, preferred_element_type=F32)
        col_max = jnp.max(rep, axis=0, keepdims=True)
        return [jnp.max(col_max[:, m * LANES:(m + 1) * LANES], axis=1, keepdims=True) for m in range(2)]

    kmax2 = []
    for h in range(DF_HEADS):
        halves = max_half_norm2(ks_ref[0, :, h * LANES:(h + 1) * LANES].astype(BF16).astype(F32))
        if cached:
            kp = kp_ref[0, h * LANES:(h + 1) * LANES, :].astype(BF16).astype(F32)
            sq = kp * kp
            pre = [jnp.max(jnp.sum(sq[m * HEAD_DIM:(m + 1) * HEAD_DIM], axis=0, keepdims=True), axis=1, keepdims=True)
                   for m in range(2)]
        else:
            pre = max_half_norm2(kp_ref[0, :, h * LANES:(h + 1) * LANES].astype(BF16).astype(F32))
        kmax2 += [jnp.maximum(halves[m], pre[m]) for m in range(2)]

    shift = []
    for h in range(DF_HEADS):
        qmax2 = max_half_norm2(q_ref[0, :, h * LANES:(h + 1) * LANES].astype(F32))
        shift += [jnp.sqrt(qmax2[m] * kmax2[2 * h + m]) * 1.004 + 1e-6 for m in range(2)]
    worst_shift = shift[0]
    for u in shift[1:]:
        worst_shift = jnp.maximum(worst_shift, u)
    bound_is_tight = jnp.max(worst_shift) * 2.0 < SOFTMAX_SHIFT_SLACK

    def scores(h, k_ref, k0, n, chunk_mask):
        if chunk_mask and fold:
            k2 = jnp.concatenate([k_ref[0, pl.ds(k0, n), h * LANES:(h + 1) * LANES].astype(BF16),
                                  kpad_ref[:, h * LANES:(h + 1) * LANES]], axis=0)
            s = lax.dot_general(qm_ref[h], k2, contract_last, preferred_element_type=F32)
            r = lax.broadcasted_iota(jnp.int32, s.shape, 0)
            r = jnp.where(r >= bq, r - bq, r)
            col = lax.broadcasted_iota(jnp.int32, s.shape, 1)
            vis = ((col < n) & ((col // CHUNK) <= (r // CHUNK))) | ((col >= n) & (col < n + n_pre))
            return jnp.where(vis, s, -jnp.inf)
        if cached and k_ref is kp_ref:
            k2 = k_ref[0, h * LANES:(h + 1) * LANES, pl.ds(k0, n)].astype(BF16)
            s = jnp.dot(qm_ref[h], k2, preferred_element_type=F32)
        else:
            k2 = k_ref[0, pl.ds(k0, n), h * LANES:(h + 1) * LANES].astype(BF16)
            s = lax.dot_general(qm_ref[h], k2, contract_last, preferred_element_type=F32)
        if chunk_mask:
            r = lax.broadcasted_iota(jnp.int32, (2 * bq, n), 0)
            r = jnp.where(r >= bq, r - bq, r)
            col = lax.broadcasted_iota(jnp.int32, (2 * bq, n), 1)
            s = jnp.where((col // CHUNK) <= (r // CHUNK), s, -jnp.inf)
        return s

    def max_tile(k_ref, k0, n, chunk_mask):
        for h in range(DF_HEADS):
            s = scores(h, k_ref, k0, n, chunk_mask)
            if s.shape[1] % LANES == 0:
                m = s[:, :LANES]
                for j in range(1, s.shape[1] // LANES):
                    m = jnp.maximum(m, s[:, j * LANES:(j + 1) * LANES])
            else:
                m = jnp.broadcast_to(jnp.max(s, axis=1, keepdims=True), (2 * bq, LANES))
            mx_ref[h] = jnp.maximum(mx_ref[h], m)

    def value_tile(k_ref, vx_ref, k0, n, chunk_mask):
        for h in range(DF_HEADS):
            s = scores(h, k_ref, k0, n, chunk_mask)
            row_max = mx_ref[h]
            width = s.shape[1]
            if width % LANES == 0:
                p = jnp.concatenate([jnp.exp(s[:, j * LANES:(j + 1) * LANES] - row_max)
                                     for j in range(width // LANES)], axis=1)
            else:
                p = jnp.exp(s - row_max[:, :width])
            vx = vx_ref[pl.ds(k0, n), 2 * h * LANES:(2 * h + 2) * LANES]
            if chunk_mask and fold:
                vx = jnp.concatenate([vx, vxpad_ref[:, 2 * h * LANES:(2 * h + 2) * LANES]], axis=0)
            acc_ref[h] += jnp.dot(p.astype(BF16), vx, preferred_element_type=F32)

    def q_block(qi, carry):
        q0 = pl.multiple_of(qi * bq, bq)
        for h in range(DF_HEADS):
            q2 = q_ref[0, pl.ds(q0, bq), h * LANES:(h + 1) * LANES]
            zero = jnp.zeros_like(q2)
            qm_ref[h] = jnp.concatenate([jnp.where(lo, q2, zero), jnp.where(lo, zero, q2)], axis=0)
        acc_ref[...] = jnp.zeros_like(acc_ref)

        for h in range(DF_HEADS):
            for m in range(2):
                mx_ref[h, m * bq:(m + 1) * bq, :] = jnp.broadcast_to(shift[2 * h + m], (bq, LANES))

        def sweep(pre_fn, self_fn):
            def pre_body(j, c):
                pre_fn(pl.multiple_of(j * bp, bp))
                return c

            def self_body(j, c):
                self_fn(pl.multiple_of(j * bq, bq), False)
                return c

            if not fold:
                lax.fori_loop(0, n_pre // bp, pre_body, 0)
            lax.fori_loop(0, qi, self_body, 0)
            self_fn(q0, True)

        @pl.when(jnp.logical_not(bound_is_tight))
        def _():
            mx_ref[...] = jnp.full_like(mx_ref, -jnp.inf)
            sweep(lambda k0: max_tile(kp_ref, k0, bp, False),
                  lambda k0, msk: max_tile(ks_ref, k0, bq, msk))
            for h in range(DF_HEADS):
                mx_ref[h] = jnp.broadcast_to(jnp.max(mx_ref[h], axis=1, keepdims=True), (2 * bq, LANES))

        sweep(lambda k0: value_tile(kp_ref, vxp_ref, k0, bp, False),
              lambda k0, msk: value_tile(ks_ref, vxs_ref, k0, bq, msk))

        for h in range(DF_HEADS):
            a0 = acc_ref[h, :bq, :]
            a1 = acc_ref[h, bq:, :]
            o = a0[:, :LANES] / a0[:, LANES:] - lam * (a1[:, :LANES] / a1[:, LANES:])
            y = o * lax.rsqrt(jnp.mean(o * o, axis=-1, keepdims=True) + EPS) * g_ref[...]
            o_ref[0, pl.ds(q0, bq), h * LANES:(h + 1) * LANES] = (y * (1.0 - lam_init)).astype(o_ref.dtype)
        return carry

    lax.fori_loop(0, tq // bq, q_block, 0)


def _df_attention(q, k_pre, v_pre, k_self, v_self, lq1, lk1, lq2, lk2, subln_g, lam_init, cached=False):
    b, tq, w = q.shape
    n_pre = k_pre.shape[2] if cached else k_pre.shape[1]
    bq = min(256, tq)
    bp = min(256, n_pre)
    pre_map = (lambda i: (i, 0, 0)) if k_pre.shape[0] == b else (lambda i: (0, 0, 0))
    bat = lambda i: (i, 0, 0)
    fixed = lambda i: (0, 0)
    fold = (not cached) and n_pre <= FOLDED_PREFIX_ROWS and bq % LANES == 0
    kern = functools.partial(_df_kernel, tq=tq, n_pre=n_pre, bq=bq, bp=bp, lam_init=lam_init, cached=cached,
                             fold=fold)
    fold_scratch = [pltpu.VMEM((FOLDED_PREFIX_ROWS, w), BF16), pltpu.VMEM((FOLDED_PREFIX_ROWS, 2 * w), BF16)]
    kp_spec = pl.BlockSpec((1,) + k_pre.shape[1:], pre_map)
    vp_spec = pl.BlockSpec((1,) + v_pre.shape[1:], pre_map)
    vec = pl.BlockSpec((1, HEAD_DIM), fixed)
    return pl.pallas_call(
        kern,
        grid=(b,),
        in_specs=[pl.BlockSpec((1, tq, w), bat),
                  kp_spec, vp_spec,
                  pl.BlockSpec((1, tq, w), bat), pl.BlockSpec((1, tq, w), bat),
                  vec, vec, vec, vec, pl.BlockSpec((1, LANES), fixed)],
        out_specs=pl.BlockSpec((1, tq, w), bat),
        out_shape=jax.ShapeDtypeStruct((b, tq, w), BF16),
        scratch_shapes=[pltpu.VMEM((n_pre, 2 * w), BF16), pltpu.VMEM((tq, 2 * w), BF16),
                        pltpu.VMEM((DF_HEADS, 2 * bq, LANES), BF16), pltpu.VMEM((DF_HEADS, 2 * bq, LANES), F32),
                        pltpu.VMEM((DF_HEADS, 2 * bq, 2 * LANES), F32)] + (fold_scratch if fold else []),
        compiler_params=_cparams(("parallel",)),
        name="df_attention",
    )(q, k_pre, v_pre, k_self, v_self, lq1, lk1, lq2, lk2, subln_g)


def _outproj_kernel(x_ref, sb_ref, df_ref, wsb_ref, wdf_ref, g_ref, wr_ref, br_ref,
                    h_ref, route_ref):
    h = (x_ref[...]
         + jnp.dot(sb_ref[...], wsb_ref[...], preferred_element_type=F32)
         + jnp.dot(df_ref[...], wdf_ref[...], preferred_element_type=F32))
    h_ref[...] = h
    xn = h * lax.rsqrt(jnp.mean(h * h, axis=-1, keepdims=True) + EPS) * g_ref[...]
    logits = jnp.dot(xn.astype(BF16), wr_ref[...], preferred_element_type=F32) + br_ref[...]
    lane = lax.broadcasted_iota(jnp.int32, logits.shape, 1)
    big = jnp.int32(LANES)
    neg = -jnp.inf

    def first_argmax(v):
        mx = jnp.max(v, axis=1, keepdims=True)
        idx = jnp.min(jnp.where(v == mx, lane, big), axis=1, keepdims=True)
        return mx, idx

    gl = jnp.where((lane >= N_EXPERTS) & (lane < N_EXPERTS + N_GROUPS), logits, neg)
    gmax, gidx = first_argmax(gl)
    g_w = 1.0 / jnp.sum(jnp.exp(gl - gmax), axis=1, keepdims=True)
    grp = gidx - N_EXPERTS
    el = jnp.where((lane < N_EXPERTS) & (lane // EXPERTS_PER_GROUP == grp), logits, neg)
    m1, i1 = first_argmax(el)
    m2, i2 = first_argmax(jnp.where(lane == i1, neg, el))
    e21 = jnp.exp(m2 - m1)
    t1 = 1.0 / (1.0 + e21)
    t2 = e21 / (1.0 + e21)
    route = jnp.where(lane == 0, i1.astype(F32),
            jnp.where(lane == 1, i2.astype(F32),
            jnp.where(lane == 2, g_w * t1,
            jnp.where(lane == 3, g_w * t2, 0.0))))
    route_ref[...] = route


def _outproj_route(x2d, sb, df, wsb, wdf, g, wr, br, tm):
    n, d = x2d.shape
    row = lambda i: (i, 0)
    fixed = lambda i: (0, 0)
    return pl.pallas_call(
        _outproj_kernel,
        grid=(n // tm,),
        in_specs=[pl.BlockSpec((tm, d), row), pl.BlockSpec((tm, GROUP_W), row), pl.BlockSpec((tm, GROUP_W), row),
                  pl.BlockSpec(wsb.shape, fixed), pl.BlockSpec(wdf.shape, fixed), pl.BlockSpec((1, d), fixed),
                  pl.BlockSpec(wr.shape, fixed), pl.BlockSpec((1, LANES), fixed)],
        out_specs=[pl.BlockSpec((tm, d), row), pl.BlockSpec((tm, LANES), row)],
        out_shape=[jax.ShapeDtypeStruct((n, d), F32), jax.ShapeDtypeStruct((n, LANES), F32)],
        compiler_params=_cparams(("parallel",)),
        name="outproj_route",
    )(x2d, sb, df, wsb, wdf, g, wr, br)


def _row_copy(src, s, dst, d, sem):
    return pltpu.make_async_copy(src.at[pl.ds(s, 1)], dst.at[pl.ds(d, 1)], sem)


def _dispatch_kernel(d0_ref, d1_ref, hp_ref, hs_ref, xs_hbm, sem, *, tm, n_prompt_tiles):
    def issue_from(src_ref):
        def issue(r, c):
            _row_copy(src_ref, r, xs_hbm, d0_ref[0, 0, r], sem).start(priority=0)
            _row_copy(src_ref, r, xs_hbm, d1_ref[0, 0, r], sem).start(priority=1)
            return c

        lax.fori_loop(0, tm, issue, 0, unroll=8)

    is_prompt = pl.program_id(0) < n_prompt_tiles
    pl.when(is_prompt)(lambda: issue_from(hp_ref))
    pl.when(jnp.logical_not(is_prompt))(lambda: issue_from(hs_ref))
    for _ in range(2):
        pltpu.make_async_copy(hp_ref, xs_hbm.at[pl.ds(0, tm)], sem).wait()


def _dispatch(dest0, dest1, h_prompt, h_sample, tm):
    n_p, d = h_prompt.shape
    n = n_p + h_sample.shape[0]
    npt = n_p // tm
    idx = lambda i: (i, 0, 0)
    smem = functools.partial(pl.BlockSpec, memory_space=pltpu.SMEM)
    return pl.pallas_call(
        functools.partial(_dispatch_kernel, tm=tm, n_prompt_tiles=npt),
        grid=(n // tm,),
        in_specs=[smem((1, 1, tm), idx), smem((1, 1, tm), idx),
                  pl.BlockSpec((tm, d), lambda i: (jnp.minimum(i, npt - 1), 0)),
                  pl.BlockSpec((tm, d), lambda i: (jnp.maximum(i - npt, 0), 0))],
        out_specs=pl.BlockSpec(memory_space=pl.ANY),
        out_shape=jax.ShapeDtypeStruct((2 * n, d), F32),
        scratch_shapes=[pltpu.SemaphoreType.DMA(())],
        compiler_params=_cparams(("arbitrary",)),
        name="moe_dispatch",
    )(dest0.reshape(n // tm, 1, tm), dest1.reshape(n // tm, 1, tm), h_prompt, h_sample)


def _expert_kernel(vt_ref, ve_ref, seg_ref, nv_ref, h_ref, g_ref, wgu_ref, wd_ref, y_ref, *, de, tm):
    v = pl.program_id(0)

    @pl.when(v < nv_ref[0])
    def _():
        e = ve_ref[v]
        t = vt_ref[v]
        h_in = h_ref[...]
        x = (h_in * lax.rsqrt(jnp.mean(h_in * h_in, axis=-1, keepdims=True) + EPS) * g_ref[...]).astype(BF16)
        gu = jnp.dot(x, wgu_ref[0], preferred_element_type=F32)
        g = gu[:, :de]
        h = (g * (1.0 / (1.0 + jnp.exp(-g)))) * gu[:, de:]
        y = jnp.dot(h.astype(BF16), wd_ref[0], preferred_element_type=F32)
        row = t * tm + lax.broadcasted_iota(jnp.int32, (tm, 1), 0)
        mine = (row >= seg_ref[e]) & (row < seg_ref[N_EXPERTS + e])
        first_visit = (v == 0) | (vt_ref[jnp.maximum(v - 1, 0)] != t)

        @pl.when(first_visit)
        def _():
            y_ref[...] = jnp.where(mine, y, 0.0)

        @pl.when(jnp.logical_not(first_visit))
        def _():
            y_ref[...] = jnp.where(mine, y, y_ref[...])


def _experts(visit_tile, visit_expert, segments, n_visits, xs, g, wgu, wd, tm):
    p, d = xs.shape
    de = wd.shape[1]
    grid_spec = pltpu.PrefetchScalarGridSpec(
        num_scalar_prefetch=4,
        grid=(visit_tile.shape[0],),
        in_specs=[pl.BlockSpec((tm, d), lambda v, vt, ve, sg, nv: (vt[v], 0)),
                  pl.BlockSpec((1, d), lambda v, vt, ve, sg, nv: (0, 0)),
                  pl.BlockSpec((1, d, 2 * de), lambda v, vt, ve, sg, nv: (ve[v], 0, 0)),
                  pl.BlockSpec((1, de, d), lambda v, vt, ve, sg, nv: (ve[v], 0, 0))],
        out_specs=pl.BlockSpec((tm, d), lambda v, vt, ve, sg, nv: (vt[v], 0)),
    )
    return pl.pallas_call(
        functools.partial(_expert_kernel, de=de, tm=tm),
        grid_spec=grid_spec,
        out_shape=jax.ShapeDtypeStruct((p, d), F32),
        compiler_params=_cparams(("arbitrary",)),
        name="moe_experts",
    )(visit_tile, visit_expert, segments, n_visits, xs, g, wgu, wd)


def _combine_kernel(d0_ref, d1_ref, n0_ref, n1_ref, h_ref, gate_ref, g_ref, ys_hbm, o_ref,
                    y0_ref, y1_ref, sem, *, tm):
    i = pl.program_id(0)
    slot = i % 2

    def gather(a_ref, b_ref, to):
        def issue(r, c):
            _row_copy(ys_hbm, a_ref[0, 0, r], y0_ref.at[to], r, sem.at[to]).start(priority=0)
            _row_copy(ys_hbm, b_ref[0, 0, r], y1_ref.at[to], r, sem.at[to]).start(priority=1)
            return c

        lax.fori_loop(0, tm, issue, 0, unroll=8)

    pl.when(i == 0)(lambda: gather(d0_ref, d1_ref, slot))
    pl.when(i + 1 < pl.num_programs(0))(lambda: gather(n0_ref, n1_ref, 1 - slot))
    for y_ref in (y0_ref, y1_ref):
        pltpu.make_async_copy(ys_hbm.at[pl.ds(0, tm)], y_ref.at[slot], sem.at[slot]).wait()
    gate = gate_ref[...]
    h = h_ref[...] + (y0_ref[slot] * gate[:, 2:3] + y1_ref[slot] * gate[:, 3:4])
    o_ref[...] = h * lax.rsqrt(jnp.mean(h * h, axis=-1, keepdims=True) + EPS) * g_ref[...]


def _combine(dest0, dest1, h2d, route, g, ys, tm):
    n, d = h2d.shape
    nt = n // tm
    idx = lambda i: (i, 0, 0)
    nxt = lambda i: (jnp.minimum(i + 1, nt - 1), 0, 0)
    row = lambda i: (i, 0)
    smem = functools.partial(pl.BlockSpec, memory_space=pltpu.SMEM)
    d0 = dest0.reshape(nt, 1, tm)
    d1 = dest1.reshape(nt, 1, tm)
    return pl.pallas_call(
        functools.partial(_combine_kernel, tm=tm),
        grid=(nt,),
        in_specs=[smem((1, 1, tm), idx), smem((1, 1, tm), idx), smem((1, 1, tm), nxt), smem((1, 1, tm), nxt),
                  pl.BlockSpec((tm, d), row), pl.BlockSpec((tm, LANES), row),
                  pl.BlockSpec((1, d), lambda i: (0, 0)), pl.BlockSpec(memory_space=pl.ANY)],
        out_specs=pl.BlockSpec((tm, d), row),
        out_shape=jax.ShapeDtypeStruct((n, d), F32),
        scratch_shapes=[pltpu.VMEM((2, tm, d), F32), pltpu.VMEM((2, tm, d), F32), pltpu.SemaphoreType.DMA((2,))],
        compiler_params=_cparams(("arbitrary",)),
        name="moe_combine",
    )(d0, d1, d0, d1, h2d, route, g, ys)


def _routing_plan(route, tm):
    t = route.shape[0]
    e = route[:, :2].astype(jnp.int32).reshape(-1)
    onehot = (e[:, None] == jnp.arange(N_EXPERTS, dtype=jnp.int32)[None, :])
    blk = 512
    oh = onehot.astype(BF16).reshape(2 * t // blk, blk, N_EXPERTS)
    tri = (jnp.arange(blk)[:, None] > jnp.arange(blk)[None, :]).astype(BF16)
    within = jnp.einsum("ij,bjk->bik", tri, oh, preferred_element_type=F32)
    totals = jnp.sum(oh.astype(F32), axis=1)
    before = jnp.cumsum(totals, axis=0) - totals
    rank_all = (within + before[:, None, :]).reshape(2 * t, N_EXPERTS)
    rank = jnp.sum(jnp.where(onehot, rank_all, 0.0), axis=1).astype(jnp.int32)
    counts = jnp.sum(totals, axis=0).astype(jnp.int32)
    ends = jnp.cumsum(counts)
    starts = ends - counts
    dest = (starts[e] + rank).reshape(t, 2)
    first_tile = starts // tm
    n_vis = jnp.where(counts > 0, (ends - 1) // tm - first_tile + 1, 0)
    vis_end = jnp.cumsum(n_vis)
    vis_start = vis_end - n_vis
    n_visits = vis_end[-1]
    max_visits = 2 * t // tm + N_EXPERTS - 1
    v = jnp.minimum(jnp.arange(max_visits, dtype=jnp.int32), n_visits - 1)
    visit_expert = jnp.minimum(jnp.sum(v[:, None] >= vis_end[None, :], axis=1), N_EXPERTS - 1).astype(jnp.int32)
    visit_tile = (first_tile[visit_expert] + v - vis_start[visit_expert]).astype(jnp.int32)
    segments = jnp.concatenate([starts, ends]).astype(jnp.int32)
    return dest[:, 0], dest[:, 1], visit_tile, visit_expert, segments, n_visits.reshape(1).astype(jnp.int32)


def _pick(n, prefs):
    for p in prefs:
        if n % p == 0:
            return p
    return n


def kernel(x_prompt, x_sample, cache_sb_k, cache_sb_v, cache_diff_k, cache_diff_v, meta_tokens, norm_mix_g, w_in, lambda_q1, lambda_k1, lambda_q2, lambda_k2, subln_g, w_out, norm_ffn_g, w_group, b_group, w_router, b_router, w_gate, w_up, w_down, final_norm_g):
    depth = w_in.shape[0]
    assert depth == 1, "meta-token rows are only dropped after a single layer"
    b, s, d = x_prompt.shape
    bs, ss, _ = x_sample.shape
    past = cache_sb_k.shape[2]
    n_meta = meta_tokens.shape[0]
    lam_init = 0.8 - 0.6 * math.exp(-0.3 * 0)

    g_mix = norm_mix_g[0].reshape(1, d)
    w_in_bf = w_in[0].astype(BF16)
    pos_p = n_meta + jnp.arange(s, dtype=jnp.int32)
    pos_s = past + jnp.arange(ss, dtype=jnp.int32)
    pos_m = jnp.arange(n_meta, dtype=jnp.int32)
    xp2 = x_prompt.reshape(b * s, d)
    xs2 = x_sample.reshape(bs * ss, d)

    tm_p = _pick(s, (512, 256, 128))
    (qsb_p, qdf_p, ksb_p, vsb_p, kdf_p, vdf_p,
     ksb_pb, vsb_pb, kdf_pb, vdf_pb) = _inproj(xp2, g_mix, w_in_bf, pos_p, tm_p, feature_major=True)
    tm_s = _pick(bs * ss, (512, 256, 128))
    (qsb_s, qdf_s, ksb_s, vsb_s, kdf_s, vdf_s,
     ksb_sb, vsb_sb, kdf_sb, vdf_sb) = _inproj(xs2, g_mix, w_in_bf, pos_s, tm_s)
    (_, _, ksb_m, vsb_m, kdf_m, vdf_m,
     ksb_mb, vsb_mb, kdf_mb, vdf_mb) = _inproj(meta_tokens, g_mix, w_in_bf, pos_m, n_meta)

    r3 = lambda a, bb, t: a.reshape(bb, t, GROUP_W)
    lam_vecs = [v[0].reshape(1, HEAD_DIM) for v in (lambda_q1, lambda_k1, lambda_q2, lambda_k2)]
    sub_g = subln_g[0].reshape(1, LANES)

    sb_p = _sb_attention(r3(qsb_p, b, s), ksb_mb[None], vsb_mb[None], r3(ksb_pb, b, s), r3(vsb_pb, b, s))
    df_p = _df_attention(r3(qdf_p, b, s), kdf_mb[None], vdf_mb[None], r3(kdf_pb, b, s), r3(vdf_pb, b, s),
                         *lam_vecs, sub_g, lam_init)
    feature_major = lambda c: jnp.transpose(c[0], (0, 2, 3, 1)).reshape(bs, GROUP_W, past)
    sb_s = _sb_attention(r3(qsb_s, bs, ss), feature_major(cache_sb_k), feature_major(cache_sb_v),
                         r3(ksb_sb, bs, ss), r3(vsb_sb, bs, ss), cached=True)
    df_s = _df_attention(r3(qdf_s, bs, ss), feature_major(cache_diff_k),
                         cache_diff_v[0].reshape(bs, past * DF_HEADS, LANES), r3(kdf_sb, bs, ss), r3(vdf_sb, bs, ss),
                         *lam_vecs, sub_g, lam_init, cached=True)

    w_out_bf = w_out[0].astype(BF16)
    wsb, wdf = w_out_bf[:GROUP_W], w_out_bf[GROUP_W:]
    g_ffn = norm_ffn_g[0].reshape(1, d)
    pad_cols = LANES - N_EXPERTS - N_GROUPS
    wr = jnp.concatenate([w_router[0], w_group[0], jnp.zeros((d, pad_cols), F32)], axis=1).astype(BF16)
    br = jnp.concatenate([b_router[0], b_group[0], jnp.zeros((pad_cols,), F32)]).reshape(1, LANES)

    h_p, route_p = _outproj_route(xp2, sb_p.reshape(b * s, GROUP_W), df_p.reshape(b * s, GROUP_W),
                                        wsb, wdf, g_ffn, wr, br, tm_p)
    h_s, route_s = _outproj_route(xs2, sb_s.reshape(bs * ss, GROUP_W), df_s.reshape(bs * ss, GROUP_W),
                                        wsb, wdf, g_ffn, wr, br, tm_s)

    tm_e = 512
    n_p = b * s
    tm_d = _pick(math.gcd(n_p, bs * ss), (512, 256, 128))
    route = jnp.concatenate([route_p, route_s], axis=0)
    dest0, dest1, visit_tile, visit_expert, segments, n_visits = _routing_plan(route, tm_e)
    xsorted = _dispatch(dest0, dest1, h_p, h_s, tm_d)
    wgu = jnp.concatenate([w_gate[0], w_up[0]], axis=-1).astype(BF16)
    ysorted = _experts(visit_tile, visit_expert, segments, n_visits, xsorted, g_ffn, wgu,
                       w_down[0].astype(BF16), tm_e)
    g_fin = final_norm_g.reshape(1, d)
    y_p = _combine(dest0[:n_p], dest1[:n_p], h_p, route_p, g_fin, ysorted, tm_d)
    y_s = _combine(dest0[n_p:], dest1[n_p:], h_s, route_s, g_fin, ysorted, tm_d)

    def with_meta(meta_rows, frames, heads, hd):
        m = jnp.broadcast_to(meta_rows[None], (b, n_meta, GROUP_W))
        return jnp.concatenate([m, frames.reshape(b, s, GROUP_W)], axis=1).reshape(1, b, n_meta + s, heads, hd)

    def with_meta_t(meta_rows, frames_t, heads, hd):
        m = jnp.broadcast_to(meta_rows.T[None], (b, GROUP_W, n_meta))
        full = jnp.concatenate([m, frames_t], axis=2).reshape(b, heads, hd, n_meta + s)
        return jnp.transpose(full, (0, 3, 1, 2))[None]

    shp = lambda a, heads, hd: a.reshape(1, bs, ss, heads, hd)
    return (y_p.reshape(b, s, d), y_s.reshape(bs, ss, d),
            with_meta_t(ksb_m, ksb_p, SB_HEADS, HEAD_DIM), with_meta_t(vsb_m, vsb_p, SB_HEADS, HEAD_DIM),
            with_meta_t(kdf_m, kdf_p, 2 * DF_HEADS, HEAD_DIM), with_meta(vdf_m, vdf_p, DF_HEADS, 2 * HEAD_DIM),
            shp(ksb_s, SB_HEADS, HEAD_DIM), shp(vsb_s, SB_HEADS, HEAD_DIM),
            shp(kdf_s, 2 * DF_HEADS, HEAD_DIM), shp(vdf_s, DF_HEADS, 2 * HEAD_DIM))
```

```python
import functools
import math

import jax
import jax.numpy as jnp
from jax import lax
from jax.experimental import pallas as pl
from jax.experimental.pallas import tpu as pltpu

F32 = jnp.float32
BF16 = jnp.bfloat16

CHUNK = 64
SB_HEADS = 8
DF_HEADS = 4
HEAD_DIM = 64
GROUP_W = 512
ROPE_THETA = 500000.0
ROT_DIM = 16
N_GROUPS = 4
EXPERTS_PER_GROUP = 8
N_EXPERTS = N_GROUPS * EXPERTS_PER_GROUP
EPS = 1e-6
LANES = 128
SB_CUTOFF = -88.0
VMEM_LIMIT = 56 * 1024 * 1024
SOFTMAX_SHIFT_SLACK = 80.0
FOLDED_PREFIX_ROWS = 128


def _cparams(sem):
    return pltpu.CompilerParams(dimension_semantics=sem, vmem_limit_bytes=VMEM_LIMIT)


def _inproj_kernel(x_ref, g_ref, w_ref, cos_ref, sin_ref,
                   qsb_ref, qdf_ref, ksb_ref, vsb_ref, kdf_ref, vdf_ref,
                   ksb_bf_ref, vsb_bf_ref, kdf_bf_ref, vdf_bf_ref, *, feature_major):
    x = x_ref[...]
    ms = jnp.mean(x * x, axis=-1, keepdims=True)
    xn = (x * lax.rsqrt(ms + EPS) * g_ref[...]).astype(BF16)
    proj = jnp.dot(xn, w_ref[...], preferred_element_type=F32)
    cos = cos_ref[...]
    sin = sin_ref[...]
    lane = lax.broadcasted_iota(jnp.int32, cos.shape, 1)
    first_half = (lane % ROT_DIM) < (ROT_DIM // 2)

    def rope(a):
        cols = []
        for j in range(GROUP_W // LANES):
            aj = a[:, j * LANES:(j + 1) * LANES]
            partner = jnp.where(first_half,
                                pltpu.roll(aj, LANES - ROT_DIM // 2, 1),
                                pltpu.roll(aj, ROT_DIM // 2, 1))
            cols.append(aj * cos + partner * sin)
        return jnp.concatenate(cols, axis=1)

    scale = HEAD_DIM ** -0.5
    qsb_ref[...] = (proj[:, 0:GROUP_W] * scale).astype(BF16)
    qdf_ref[...] = (rope(proj[:, 3 * GROUP_W:4 * GROUP_W]) * scale).astype(BF16)
    kdf = rope(proj[:, 4 * GROUP_W:5 * GROUP_W])
    for f32_ref, bf_ref, val in ((ksb_ref, ksb_bf_ref, proj[:, GROUP_W:2 * GROUP_W]),
                                 (vsb_ref, vsb_bf_ref, proj[:, 2 * GROUP_W:3 * GROUP_W]),
                                 (kdf_ref, kdf_bf_ref, kdf),
                                 (vdf_ref, vdf_bf_ref, proj[:, 5 * GROUP_W:6 * GROUP_W])):
        if feature_major and f32_ref is not vdf_ref:
            f32_ref[0] = val.T
        else:
            f32_ref[...] = val
        bf_ref[...] = val.astype(BF16)


def _rope_tables(pos):
    half = ROT_DIM // 2
    inv = ROPE_THETA ** (-jnp.arange(half, dtype=F32) * 2.0 / ROT_DIM)
    ang = pos.astype(F32)[:, None] * inv[None, :]
    cos, sin = jnp.cos(ang), jnp.sin(ang)
    n = pos.shape[0]
    pad = HEAD_DIM - ROT_DIM
    cos_h = jnp.concatenate([cos, cos, jnp.ones((n, pad), F32)], axis=1)
    sin_h = jnp.concatenate([-sin, sin, jnp.zeros((n, pad), F32)], axis=1)
    reps = LANES // HEAD_DIM
    return jnp.tile(cos_h, (1, reps)), jnp.tile(sin_h, (1, reps))


def _inproj(x2d, g, w_bf, pos, tm, feature_major=False):
    n, d = x2d.shape
    cos_t, sin_t = _rope_tables(pos)
    period = pos.shape[0]
    if period < tm:
        cos_t = jnp.tile(cos_t, (tm // period, 1))
        sin_t = jnp.tile(sin_t, (tm // period, 1))
    ntab = cos_t.shape[0] // tm
    row = lambda i: (i, 0)
    fixed = lambda i: (0, 0)
    tab = lambda i: (i % ntab, 0)
    out_spec = pl.BlockSpec((tm, GROUP_W), row)
    f32_spec, f32_shape = out_spec, jax.ShapeDtypeStruct((n, GROUP_W), F32)
    if feature_major:
        f32_spec = pl.BlockSpec((1, GROUP_W, tm), lambda i: (i // ntab, 0, i % ntab))
        f32_shape = jax.ShapeDtypeStruct((n // period, GROUP_W, period), F32)
    return pl.pallas_call(
        functools.partial(_inproj_kernel, feature_major=feature_major),
        grid=(n // tm,),
        in_specs=[pl.BlockSpec((tm, d), row), pl.BlockSpec((1, d), fixed),
                  pl.BlockSpec(w_bf.shape, fixed),
                  pl.BlockSpec((tm, LANES), tab), pl.BlockSpec((tm, LANES), tab)],
        out_specs=[out_spec] * 2 + [f32_spec] * 3 + [out_spec] * 5,
        out_shape=[jax.ShapeDtypeStruct((n, GROUP_W), BF16)] * 2
                  + [f32_shape] * 3 + [jax.ShapeDtypeStruct((n, GROUP_W), F32)]
                  + [jax.ShapeDtypeStruct((n, GROUP_W), BF16)] * 4,
        compiler_params=_cparams(("parallel",)),
        name="inproj",
    )(x2d, g, w_bf, cos_t, sin_t)


def _sb_kernel(q_ref, kp_ref, vp_ref, ks_ref, vs_ref, o_ref, acc_ref, car_ref, *cache_scratch,
               tq, n_pre, bq, bp, ways, cached):
    lane = lax.broadcasted_iota(jnp.int32, (bq, LANES), 1)
    lo = lane < HEAD_DIM
    contract_last = (((1,), (1,)), ((), ()))
    wn = 2 * bq if tq >= 2 * bq else bq
    step = pl.program_id(0)

    def neg_suffix_matrix(n):
        j = lax.broadcasted_iota(jnp.int32, (n, n + LANES), 0)
        s = lax.broadcasted_iota(jnp.int32, (n, n + LANES), 1)
        return jnp.where((j > s) | (s >= n), -1.0, 0.0).astype(BF16)

    suffixes = {n: neg_suffix_matrix(n) for n in {wn, bq, bp}}

    def row_major(k_ref, v_ref, k0, n):
        def load(hp):
            cols = slice(hp * LANES, (hp + 1) * LANES)
            return k_ref[0, pl.ds(k0, n), cols].astype(BF16), v_ref[0, pl.ds(k0, n), cols].astype(BF16)
        return load

    def sweep(load, feature_major, q0, koff, n, causal, first, w):
        suffix = suffixes[n]
        n_pairs = SB_HEADS // 2
        if causal:
            r = lax.broadcasted_iota(jnp.int32, (2 * bq, n), 0)
            c = lax.broadcasted_iota(jnp.int32, (2 * bq, n), 1)
            vis = (c + koff) < jnp.where(r >= bq, r - bq, r)
        log_sig, masked_sp, values = [], [], []
        for hp in range(n_pairs):
            cols = slice(hp * LANES, (hp + 1) * LANES)
            k2, v2 = load(hp)
            q2 = q_ref[0, pl.ds(q0, bq), cols]
            zero = jnp.zeros_like(q2)
            qs = jnp.concatenate([jnp.where(lo, q2, zero), jnp.where(lo, zero, q2)], axis=0)
            if feature_major:
                s = jnp.dot(qs, k2, preferred_element_type=F32)
            else:
                s = lax.dot_general(qs, k2, contract_last, preferred_element_type=F32)
            sp = jnp.maximum(s, 0.0) + jnp.log(1.0 + jnp.exp(-jnp.abs(s)))
            log_sig.append(s - sp)
            masked_sp.append((jnp.where(vis, sp, 0.0) if causal else sp).astype(BF16))
            values.append(v2)
        cr_all = jnp.dot(jnp.concatenate(masked_sp, axis=0), suffix, preferred_element_type=F32)
        worst = None
        for hp in range(n_pairs):
            cols = slice(hp * LANES, (hp + 1) * LANES)
            cr = cr_all[2 * bq * hp:2 * bq * (hp + 1)]
            between = cr[:, :n]
            car = cr[:, n:]
            if not first:
                old = car_ref[w, hp]
                between = between + (jnp.tile(old, (1, n // LANES)) if n % LANES == 0 else old[:, :n])
                car = car + old
            p = jnp.exp(log_sig[hp] + between)
            if causal:
                p = jnp.where(vis, p, 0.0)
            if feature_major:
                pv = lax.dot_general(p.astype(BF16), values[hp], contract_last, preferred_element_type=F32)
            else:
                pv = jnp.dot(p.astype(BF16), values[hp], preferred_element_type=F32)
            car_ref[w, hp] = car
            worst = car if worst is None else jnp.maximum(worst, car)
            both = jnp.where(lo, pv[:bq], pv[bq:])
            if first:
                acc_ref[w, :, cols] = both
            else:
                acc_ref[w, :, cols] += both
        return jnp.max(worst) > SB_CUTOFF

    if cached:
        kwin_ref, vwin_ref, kblk_ref, vblk_ref, win_sem, blk_sem = cache_scratch
        pw = min(2 * bp, n_pre)

        def window_copies(bi, slot):
            return [pltpu.make_async_copy(src.at[bi, :, pl.ds(n_pre - pw, pw)], dst.at[slot], win_sem.at[a, slot])
                    for a, (src, dst) in enumerate(((kp_ref, kwin_ref), (vp_ref, vwin_ref)))]

        def block_copies(k0):
            return [pltpu.make_async_copy(src.at[step, :, pl.ds(k0, bp)], dst, blk_sem.at[a])
                    for a, (src, dst) in enumerate(((kp_ref, kblk_ref), (vp_ref, vblk_ref)))]

        slot = step % 2

        @pl.when(step == 0)
        def _():
            for cp in window_copies(step, slot):
                cp.start()

        @pl.when(step + 1 < pl.num_programs(0))
        def _():
            for cp in window_copies(step + 1, 1 - slot):
                cp.start()

        for cp in window_copies(step, slot):
            cp.wait()

    def cond(st):
        return (st[0] >= 0) & st[1]

    def first_sweep(qi, w):
        q0 = pl.multiple_of(qi * bq, bq)
        first_blk = jnp.maximum(qi + 1 - wn // bq, 0)
        k0 = pl.multiple_of(first_blk * bq, bq)
        return q0, first_blk, sweep(row_major(ks_ref, vs_ref, k0, wn), False, q0, k0 - q0, wn, True, True, w)

    def finish(w, q0, first_blk, go):
        def self_body(st):
            kj = pl.multiple_of(st[0] * bq, bq)
            return st[0] - 1, sweep(row_major(ks_ref, vs_ref, kj, bq), False, q0, 0, bq, False, False, w)

        _, go = lax.while_loop(cond, self_body, (first_blk - 1, go))

        if not cached:
            def pre_body(st):
                kj = pl.multiple_of(st[0] * bp, bp)
                return st[0] - 1, sweep(row_major(kp_ref, vp_ref, kj, bp), False, q0, 0, bp, False, False, w)

            lax.while_loop(cond, pre_body, (jnp.int32(n_pre // bp - 1), go))
        else:
            def window_block(off):
                def load(hp):
                    rows = slice(hp * LANES, (hp + 1) * LANES)
                    return (kwin_ref[slot, rows, off:off + bp].astype(BF16),
                            vwin_ref[slot, rows, off:off + bp].astype(BF16))
                return load

            for off in range(pw - bp, -1, -bp):
                go = lax.cond(go, lambda off=off: sweep(window_block(off), True, q0, 0, bp, False, False, w),
                              lambda: jnp.zeros((), jnp.bool_))

            def older_body(st):
                for cp in block_copies(pl.multiple_of(st[0] * bp, bp)):
                    cp.start()
                for cp in block_copies(pl.multiple_of(st[0] * bp, bp)):
                    cp.wait()

                def load(hp):
                    rows = slice(hp * LANES, (hp + 1) * LANES)
                    return kblk_ref[rows, :].astype(BF16), vblk_ref[rows, :].astype(BF16)

                return st[0] - 1, sweep(load, True, q0, 0, bp, False, False, w)

            lax.while_loop(cond, older_body, (jnp.int32((n_pre - pw) // bp - 1), go))
        o_ref[0, pl.ds(q0, bq), :] = acc_ref[w].astype(o_ref.dtype)

    def q_group(g, carry):
        started = [first_sweep(g * ways + w, w) for w in range(ways)]
        for w in range(ways):
            finish(w, *started[w])
        return carry

    lax.fori_loop(0, tq // bq // ways, q_group, 0)


def _sb_attention(q, k_pre, v_pre, k_self, v_self, cached=False):
    b, tq, w = q.shape
    n_pre = k_pre.shape[2] if cached else k_pre.shape[1]
    bq = min(128, tq)
    bp = min(128, n_pre)
    ways = 2 if (tq // bq) % 2 == 0 else 1
    bat = lambda i: (i, 0, 0)
    scratch = [pltpu.VMEM((ways, bq, w), F32), pltpu.VMEM((ways, SB_HEADS // 2, 2 * bq, LANES), F32)]
    if cached:
        pre_spec = pl.BlockSpec(memory_space=pl.ANY)
        pw = min(2 * bp, n_pre)
        scratch += [pltpu.VMEM((2, w, pw), F32), pltpu.VMEM((2, w, pw), F32),
                    pltpu.VMEM((w, bp), F32), pltpu.VMEM((w, bp), F32),
                    pltpu.SemaphoreType.DMA((2, 2)), pltpu.SemaphoreType.DMA((2,))]
    else:
        pre_map = (lambda i: (i, 0, 0)) if k_pre.shape[0] == b else (lambda i: (0, 0, 0))
        pre_spec = pl.BlockSpec((1, n_pre, w), pre_map)
    kern = functools.partial(_sb_kernel, tq=tq, n_pre=n_pre, bq=bq, bp=bp, ways=ways, cached=cached)
    return pl.pallas_call(
        kern,
        grid=(b,),
        in_specs=[pl.BlockSpec((1, tq, w), bat), pre_spec, pre_spec,
                  pl.BlockSpec((1, tq, w), bat), pl.BlockSpec((1, tq, w), bat)],
        out_specs=pl.BlockSpec((1, tq, w), bat),
        out_shape=jax.ShapeDtypeStruct((b, tq, w), BF16),
        scratch_shapes=scratch,
        compiler_params=_cparams(("arbitrary",) if cached else ("parallel",)),
        name="sb_attention",
    )(q, k_pre, v_pre, k_self, v_self)


def _df_kernel(q_ref, kp_ref, vp_ref, ks_ref, vs_ref, lq1_ref, lk1_ref, lq2_ref, lk2_ref, g_ref,
               o_ref, vxp_ref, vxs_ref, qm_ref, mx_ref, acc_ref, *fold_scratch,
               tq, n_pre, bq, bp, lam_init, cached, fold):
    lane = lax.broadcasted_iota(jnp.int32, (bq, LANES), 1)
    lo = lane < HEAD_DIM
    contract_last = (((1,), (1,)), ((), ()))
    lam = (jnp.exp(jnp.sum(lq1_ref[...] * lk1_ref[...], axis=-1, keepdims=True))
           - jnp.exp(jnp.sum(lq2_ref[...] * lk2_ref[...], axis=-1, keepdims=True)) + lam_init)

    def extend_values(v_ref, vx_ref, head_rows):
        n = vx_ref.shape[0]
        ones = jnp.ones((n, LANES), BF16)
        for h in range(DF_HEADS):
            if head_rows:
                v_h = v_ref[0, pl.ds(h, n, stride=DF_HEADS), :]
            else:
                v_h = v_ref[0, :, h * LANES:(h + 1) * LANES]
            vx_ref[:, 2 * h * LANES:(2 * h + 1) * LANES] = v_h.astype(BF16)
            vx_ref[:, (2 * h + 1) * LANES:(2 * h + 2) * LANES] = ones

    extend_values(vp_ref, vxp_ref, cached)
    extend_values(vs_ref, vxs_ref, False)
    if fold:
        kpad_ref, vxpad_ref = fold_scratch
        kpad_ref[...] = jnp.zeros_like(kpad_ref)
        vxpad_ref[...] = jnp.zeros_like(vxpad_ref)
        kpad_ref[0:n_pre, :] = kp_ref[0].astype(BF16)
        vxpad_ref[0:n_pre, :] = vxp_ref[...]

    sel_row = lax.broadcasted_iota(jnp.int32, (LANES, 2 * LANES), 0) // HEAD_DIM
    sel_col = lax.broadcasted_iota(jnp.int32, (LANES, 2 * LANES), 1) // LANES
    half_selector = jnp.where(sel_row == sel_col, 1.0, 0.0).astype(BF16)

    def max_half_norm2(x):
        rep = jnp.dot((x * x).astype(BF16), half_selector, preferred_element_type=F32)
        col_max = jnp.max(rep, axis=0, keepdims=True)
        return [jnp.max(col_max[:, m * LANES:(m + 1) * LANES], axis=1, keepdims=True) for m in range(2)]

    kmax2 = []
    for h in range(DF_HEADS):
        halves = max_half_norm2(ks_ref[0, :, h * LANES:(h + 1) * LANES].astype(BF16).astype(F32))
        if cached:
            kp = kp_ref[0, h * LANES:(h + 1) * LANES, :].astype(BF16).astype(F32)
            sq = kp * kp
            pre = [jnp.max(jnp.sum(sq[m * HEAD_DIM:(m + 1) * HEAD_DIM], axis=0, keepdims=True), axis=1, keepdims=True)
                   for m in range(2)]
        else:
            pre = max_half_norm2(kp_ref[0, :, h * LANES:(h + 1) * LANES].astype(BF16).astype(F32))
        kmax2 += [jnp.maximum(halves[m], pre[m]) for m in range(2)]

    shift = []
    for h in range(DF_HEADS):
        qmax2 = max_half_norm2(q_ref[0, :, h * LANES:(h + 1) * LANES].astype(F32))
        shift += [jnp.sqrt(qmax2[m] * kmax2[2 * h + m]) * 1.004 + 1e-6 for m in range(2)]
    worst_shift = shift[0]
    for u in shift[1:]:
        worst_shift = jnp.maximum(worst_shift, u)
    bound_is_tight = jnp.max(worst_shift) * 2.0 < SOFTMAX_SHIFT_SLACK

    def scores(h, k_ref, k0, n, chunk_mask):
        if chunk_mask and fold:
            k2 = jnp.concatenate([k_ref[0, pl.ds(k0, n), h * LANES:(h + 1) * LANES].astype(BF16),
                                  kpad_ref[:, h * LANES:(h + 1) * LANES]], axis=0)
            s = lax.dot_general(qm_ref[h], k2, contract_last, preferred_element_type=F32)
            r = lax.broadcasted_iota(jnp.int32, s.shape, 0)
            r = jnp.where(r >= bq, r - bq, r)
            col = lax.broadcasted_iota(jnp.int32, s.shape, 1)
            vis = ((col < n) & ((col // CHUNK) <= (r // CHUNK))) | ((col >= n) & (col < n + n_pre))
            return jnp.where(vis, s, -jnp.inf)
        if cached and k_ref is kp_ref:
            k2 = k_ref[0, h * LANES:(h + 1) * LANES, pl.ds(k0, n)].astype(BF16)
            s = jnp.dot(qm_ref[h], k2, preferred_element_type=F32)
        else:
            k2 = k_ref[0, pl.ds(k0, n), h * LANES:(h + 1) * LANES].astype(BF16)
            s = lax.dot_general(qm_ref[h], k2, contract_last, preferred_element_type=F32)
        if chunk_mask:
            r = lax.broadcasted_iota(jnp.int32, (2 * bq, n), 0)
            r = jnp.where(r >= bq, r - bq, r)
            col = lax.broadcasted_iota(jnp.int32, (2 * bq, n), 1)
            s = jnp.where((col // CHUNK) <= (r // CHUNK), s, -jnp.inf)
        return s

    def max_tile(k_ref, k0, n, chunk_mask):
        for h in range(DF_HEADS):
            s = scores(h, k_ref, k0, n, chunk_mask)
            if s.shape[1] % LANES == 0:
                m = s[:, :LANES]
                for j in range(1, s.shape[1] // LANES):
                    m = jnp.maximum(m, s[:, j * LANES:(j + 1) * LANES])
            else:
                m = jnp.broadcast_to(jnp.max(s, axis=1, keepdims=True), (2 * bq, LANES))
            mx_ref[h] = jnp.maximum(mx_ref[h], m)

    def value_tile(k_ref, vx_ref, k0, n, chunk_mask):
        for h in range(DF_HEADS):
            s = scores(h, k_ref, k0, n, chunk_mask)
            row_max = mx_ref[h]
            width = s.shape[1]
            if width % LANES == 0:
                p = jnp.concatenate([jnp.exp(s[:, j * LANES:(j + 1) * LANES] - row_max)
                                     for j in range(width // LANES)], axis=1)
            else:
                p = jnp.exp(s - row_max[:, :width])
            vx = vx_ref[pl.ds(k0, n), 2 * h * LANES:(2 * h + 2) * LANES]
            if chunk_mask and fold:
                vx = jnp.concatenate([vx, vxpad_ref[:, 2 * h * LANES:(2 * h + 2) * LANES]], axis=0)
            acc_ref[h] += jnp.dot(p.astype(BF16), vx, preferred_element_type=F32)

    def q_block(qi, carry):
        q0 = pl.multiple_of(qi * bq, bq)
        for h in range(DF_HEADS):
            q2 = q_ref[0, pl.ds(q0, bq), h * LANES:(h + 1) * LANES]
            zero = jnp.zeros_like(q2)
            qm_ref[h] = jnp.concatenate([jnp.where(lo, q2, zero), jnp.where(lo, zero, q2)], axis=0)
        acc_ref[...] = jnp.zeros_like(acc_ref)

        for h in range(DF_HEADS):
            for m in range(2):
                mx_ref[h, m * bq:(m + 1) * bq, :] = jnp.broadcast_to(shift[2 * h + m], (bq, LANES))

        def sweep(pre_fn, self_fn):
            def pre_body(j, c):
                pre_fn(pl.multiple_of(j * bp, bp))
                return c

            def self_body(j, c):
                self_fn(pl.multiple_of(j * bq, bq), False)
                return c

            if not fold:
                lax.fori_loop(0, n_pre // bp, pre_body, 0)
            lax.fori_loop(0, qi, self_body, 0)
            self_fn(q0, True)

        @pl.when(jnp.logical_not(bound_is_tight))
        def _():
            mx_ref[...] = jnp.full_like(mx_ref, -jnp.inf)
            sweep(lambda k0: max_tile(kp_ref, k0, bp, False),
                  lambda k0, msk: max_tile(ks_ref, k0, bq, msk))
            for h in range(DF_HEADS):
                mx_ref[h] = jnp.broadcast_to(jnp.max(mx_ref[h], axis=1, keepdims=True), (2 * bq, LANES))

        sweep(lambda k0: value_tile(kp_ref, vxp_ref, k0, bp, False),
              lambda k0, msk: value_tile(ks_ref, vxs_ref, k0, bq, msk))

        for h in range(DF_HEADS):
            a0 = acc_ref[h, :bq, :]
            a1 = acc_ref[h, bq:, :]
            o = a0[:, :LANES] / a0[:, LANES:] - lam * (a1[:, :LANES] / a1[:, LANES:])
            y = o * lax.rsqrt(jnp.mean(o * o, axis=-1, keepdims=True) + EPS) * g_ref[...]
            o_ref[0, pl.ds(q0, bq), h * LANES:(h + 1) * LANES] = (y * (1.0 - lam_init)).astype(o_ref.dtype)
        return carry

    lax.fori_loop(0, tq // bq, q_block, 0)


def _df_attention(q, k_pre, v_pre, k_self, v_self, lq1, lk1, lq2, lk2, subln_g, lam_init, cached=False):
    b, tq, w = q.shape
    n_pre = k_pre.shape[2] if cached else k_pre.shape[1]
    bq = min(256, tq)
    bp = min(256, n_pre)
    pre_map = (lambda i: (i, 0, 0)) if k_pre.shape[0] == b else (lambda i: (0, 0, 0))
    bat = lambda i: (i, 0, 0)
    fixed = lambda i: (0, 0)
    fold = (not cached) and n_pre <= FOLDED_PREFIX_ROWS and bq % LANES == 0
    kern = functools.partial(_df_kernel, tq=tq, n_pre=n_pre, bq=bq, bp=bp, lam_init=lam_init, cached=cached,
                             fold=fold)
    fold_scratch = [pltpu.VMEM((FOLDED_PREFIX_ROWS, w), BF16), pltpu.VMEM((FOLDED_PREFIX_ROWS, 2 * w), BF16)]
    kp_spec = pl.BlockSpec((1,) + k_pre.shape[1:], pre_map)
    vp_spec = pl.BlockSpec((1,) + v_pre.shape[1:], pre_map)
    vec = pl.BlockSpec((1, HEAD_DIM), fixed)
    return pl.pallas_call(
        kern,
        grid=(b,),
        in_specs=[pl.BlockSpec((1, tq, w), bat),
                  kp_spec, vp_spec,
                  pl.BlockSpec((1, tq, w), bat), pl.BlockSpec((1, tq, w), bat),
                  vec, vec, vec, vec, pl.BlockSpec((1, LANES), fixed)],
        out_specs=pl.BlockSpec((1, tq, w), bat),
        out_shape=jax.ShapeDtypeStruct((b, tq, w), BF16),
        scratch_shapes=[pltpu.VMEM((n_pre, 2 * w), BF16), pltpu.VMEM((tq, 2 * w), BF16),
                        pltpu.VMEM((DF_HEADS, 2 * bq, LANES), BF16), pltpu.VMEM((DF_HEADS, 2 * bq, LANES), F32),
                        pltpu.VMEM((DF_HEADS, 2 * bq, 2 * LANES), F32)] + (fold_scratch if fold else []),
        compiler_params=_cparams(("parallel",)),
        name="df_attention",
    )(q, k_pre, v_pre, k_self, v_self, lq1, lk1, lq2, lk2, subln_g)


def _outproj_kernel(x_ref, sb_ref, df_ref, wsb_ref, wdf_ref, g_ref, wr_ref, br_ref,
                    h_ref, route_ref):
    h = (x_ref[...]
         + jnp.dot(sb_ref[...], wsb_ref[...], preferred_element_type=F32)
         + jnp.dot(df_ref[...], wdf_ref[...], preferred_element_type=F32))
    h_ref[...] = h
    xn = h * lax.rsqrt(jnp.mean(h * h, axis=-1, keepdims=True) + EPS) * g_ref[...]
    logits = jnp.dot(xn.astype(BF16), wr_ref[...], preferred_element_type=F32) + br_ref[...]
    lane = lax.broadcasted_iota(jnp.int32, logits.shape, 1)
    big = jnp.int32(LANES)
    neg = -jnp.inf

    def first_argmax(v):
        mx = jnp.max(v, axis=1, keepdims=True)
        idx = jnp.min(jnp.where(v == mx, lane, big), axis=1, keepdims=True)
        return mx, idx

    gl = jnp.where((lane >= N_EXPERTS) & (lane < N_EXPERTS + N_GROUPS), logits, neg)
    gmax, gidx = first_argmax(gl)
    g_w = 1.0 / jnp.sum(jnp.exp(gl - gmax), axis=1, keepdims=True)
    grp = gidx - N_EXPERTS
    el = jnp.where((lane < N_EXPERTS) & (lane // EXPERTS_PER_GROUP == grp), logits, neg)
    m1, i1 = first_argmax(el)
    m2, i2 = first_argmax(jnp.where(lane == i1, neg, el))
    e21 = jnp.exp(m2 - m1)
    t1 = 1.0 / (1.0 + e21)
    t2 = e21 / (1.0 + e21)
    route = jnp.where(lane == 0, i1.astype(F32),
            jnp.where(lane == 1, i2.astype(F32),
            jnp.where(lane == 2, g_w * t1,
            jnp.where(lane == 3, g_w * t2, 0.0))))
    route_ref[...] = route


def _outproj_route(x2d, sb, df, wsb, wdf, g, wr, br, tm):
    n, d = x2d.shape
    row = lambda i: (i, 0)
    fixed = lambda i: (0, 0)
    return pl.pallas_call(
        _outproj_kernel,
        grid=(n // tm,),
        in_specs=[pl.BlockSpec((tm, d), row), pl.BlockSpec((tm, GROUP_W), row), pl.BlockSpec((tm, GROUP_W), row),
                  pl.BlockSpec(wsb.shape, fixed), pl.BlockSpec(wdf.shape, fixed), pl.BlockSpec((1, d), fixed),
                  pl.BlockSpec(wr.shape, fixed), pl.BlockSpec((1, LANES), fixed)],
        out_specs=[pl.BlockSpec((tm, d), row), pl.BlockSpec((tm, LANES), row)],
        out_shape=[jax.ShapeDtypeStruct((n, d), F32), jax.ShapeDtypeStruct((n, LANES), F32)],
        compiler_params=_cparams(("parallel",)),
        name="outproj_route",
    )(x2d, sb, df, wsb, wdf, g, wr, br)


def _row_copy(src, s, dst, d, sem):
    return pltpu.make_async_copy(src.at[pl.ds(s, 1)], dst.at[pl.ds(d, 1)], sem)


def _dispatch_kernel(d_ref, hp_ref, hs_ref, xs_hbm, sem, *, tm, n_prompt_tiles):
    def issue_from(src_ref):
        def issue(r, c):
            _row_copy(src_ref, r, xs_hbm, d_ref[0, 0, r], sem).start(priority=0)
            _row_copy(src_ref, r, xs_hbm, d_ref[0, 1, r], sem).start(priority=1)
            return c

        lax.fori_loop(0, tm, issue, 0, unroll=8)

    is_prompt = pl.program_id(0) < n_prompt_tiles
    pl.when(is_prompt)(lambda: issue_from(hp_ref))
    pl.when(jnp.logical_not(is_prompt))(lambda: issue_from(hs_ref))
    for _ in range(2):
        pltpu.make_async_copy(hp_ref, xs_hbm.at[pl.ds(0, tm)], sem).wait()


def _dispatch(dest, h_prompt, h_sample, tm):
    n_p, d = h_prompt.shape
    n = n_p + h_sample.shape[0]
    npt = n_p // tm
    idx = lambda i: (i, 0, 0)
    smem = functools.partial(pl.BlockSpec, memory_space=pltpu.SMEM)
    return pl.pallas_call(
        functools.partial(_dispatch_kernel, tm=tm, n_prompt_tiles=npt),
        grid=(n // tm,),
        in_specs=[smem((1,) + dest.shape[1:], idx),
                  pl.BlockSpec((tm, d), lambda i: (jnp.minimum(i, npt - 1), 0)),
                  pl.BlockSpec((tm, d), lambda i: (jnp.maximum(i - npt, 0), 0))],
        out_specs=pl.BlockSpec(memory_space=pl.ANY),
        out_shape=jax.ShapeDtypeStruct((2 * n, d), F32),
        scratch_shapes=[pltpu.SemaphoreType.DMA(())],
        compiler_params=_cparams(("arbitrary",)),
        name="moe_dispatch",
    )(dest, h_prompt, h_sample)


def _expert_kernel(vt_ref, ve_ref, seg_ref, nv_ref, h_ref, g_ref, wgu_ref, wd_ref, y_ref, *, de, tm):
    v = pl.program_id(0)

    @pl.when(v < nv_ref[0])
    def _():
        e = ve_ref[v]
        t = vt_ref[v]
        h_in = h_ref[...]
        x = (h_in * lax.rsqrt(jnp.mean(h_in * h_in, axis=-1, keepdims=True) + EPS) * g_ref[...]).astype(BF16)
        gu = jnp.dot(x, wgu_ref[0], preferred_element_type=F32)
        g = gu[:, :de]
        h = (g * (1.0 / (1.0 + jnp.exp(-g)))) * gu[:, de:]
        y = jnp.dot(h.astype(BF16), wd_ref[0], preferred_element_type=F32)
        row = t * tm + lax.broadcasted_iota(jnp.int32, (tm, 1), 0)
        mine = (row >= seg_ref[e]) & (row < seg_ref[N_EXPERTS + e])
        first_visit = (v == 0) | (vt_ref[jnp.maximum(v - 1, 0)] != t)

        @pl.when(first_visit)
        def _():
            y_ref[...] = jnp.where(mine, y, 0.0)

        @pl.when(jnp.logical_not(first_visit))
        def _():
            y_ref[...] = jnp.where(mine, y, y_ref[...])


def _experts(visit_tile, visit_expert, segments, n_visits, xs, g, wgu, wd, tm):
    p, d = xs.shape
    de = wd.shape[1]
    grid_spec = pltpu.PrefetchScalarGridSpec(
        num_scalar_prefetch=4,
        grid=(visit_tile.shape[0],),
        in_specs=[pl.BlockSpec((tm, d), lambda v, vt, ve, sg, nv: (vt[v], 0)),
                  pl.BlockSpec((1, d), lambda v, vt, ve, sg, nv: (0, 0)),
                  pl.BlockSpec((1, d, 2 * de), lambda v, vt, ve, sg, nv: (ve[v], 0, 0)),
                  pl.BlockSpec((1, de, d), lambda v, vt, ve, sg, nv: (ve[v], 0, 0))],
        out_specs=pl.BlockSpec((tm, d), lambda v, vt, ve, sg, nv: (vt[v], 0)),
    )
    return pl.pallas_call(
        functools.partial(_expert_kernel, de=de, tm=tm),
        grid_spec=grid_spec,
        out_shape=jax.ShapeDtypeStruct((p, d), F32),
        compiler_params=_cparams(("arbitrary",)),
        name="moe_experts",
    )(visit_tile, visit_expert, segments, n_visits, xs, g, wgu, wd)


def _combine_kernel(d_ref, dn_ref, h_ref, gate_ref, g_ref, ys_hbm, o_ref, y0_ref, y1_ref, sem, *, tm):
    i = pl.program_id(0)
    slot = i % 2

    def gather(idx_ref, to):
        def issue(r, c):
            _row_copy(ys_hbm, idx_ref[0, 0, r], y0_ref.at[to], r, sem.at[to]).start(priority=0)
            _row_copy(ys_hbm, idx_ref[0, 1, r], y1_ref.at[to], r, sem.at[to]).start(priority=1)
            return c

        lax.fori_loop(0, tm, issue, 0, unroll=8)

    pl.when(i == 0)(lambda: gather(d_ref, slot))
    pl.when(i + 1 < pl.num_programs(0))(lambda: gather(dn_ref, 1 - slot))
    for y_ref in (y0_ref, y1_ref):
        pltpu.make_async_copy(ys_hbm.at[pl.ds(0, tm)], y_ref.at[slot], sem.at[slot]).wait()
    gate = gate_ref[...]
    h = h_ref[...] + (y0_ref[slot] * gate[:, 2:3] + y1_ref[slot] * gate[:, 3:4])
    o_ref[...] = h * lax.rsqrt(jnp.mean(h * h, axis=-1, keepdims=True) + EPS) * g_ref[...]


def _combine(dest, h2d, route, g, ys, tm):
    n, d = h2d.shape
    nt = n // tm
    idx = lambda i: (i, 0, 0)
    nxt = lambda i: (jnp.minimum(i + 1, nt - 1), 0, 0)
    row = lambda i: (i, 0)
    smem = functools.partial(pl.BlockSpec, memory_space=pltpu.SMEM)
    blk = (1,) + dest.shape[1:]
    return pl.pallas_call(
        functools.partial(_combine_kernel, tm=tm),
        grid=(nt,),
        in_specs=[smem(blk, idx), smem(blk, nxt),
                  pl.BlockSpec((tm, d), row), pl.BlockSpec((tm, LANES), row),
                  pl.BlockSpec((1, d), lambda i: (0, 0)), pl.BlockSpec(memory_space=pl.ANY)],
        out_specs=pl.BlockSpec((tm, d), row),
        out_shape=jax.ShapeDtypeStruct((n, d), F32),
        scratch_shapes=[pltpu.VMEM((2, tm, d), F32), pltpu.VMEM((2, tm, d), F32), pltpu.SemaphoreType.DMA((2,))],
        compiler_params=_cparams(("arbitrary",)),
        name="moe_combine",
    )(dest, dest, h2d, route, g, ys)


def _plan_kernel(route_ref, dest_ref, counts_ref, run_ref, base_ref, *, tm):
    pas = pl.program_id(0)
    i = pl.program_id(1)
    lane = lax.broadcasted_iota(jnp.int32, (tm, LANES), 1)
    r = route_ref[...]
    e0 = r[:, 0:1].astype(jnp.int32)
    e1 = r[:, 1:2].astype(jnp.int32)
    hot = jnp.where((lane == e0) | (lane == e1), 1.0, 0.0)
    tile_counts = jnp.sum(hot, axis=0, keepdims=True)

    @pl.when(i == 0)
    def _():
        @pl.when(pas == 1)
        def _():
            counts = run_ref[...]
            counts_ref[...] = counts
            lane1 = lax.broadcasted_iota(jnp.int32, (1, LANES), 1)
            incl = counts
            for sh in (1, 2, 4, 8, 16, 32, 64):
                incl = incl + jnp.where(lane1 >= sh, pltpu.roll(incl, sh, 1), 0.0)
            base_ref[...] = incl - counts

        run_ref[...] = jnp.zeros_like(run_ref)

    @pl.when(pas == 1)
    def _():
        rr = lax.broadcasted_iota(jnp.int32, (tm, tm), 0)
        cc = lax.broadcasted_iota(jnp.int32, (tm, tm), 1)
        earlier = jnp.where(cc < rr, 1.0, 0.0).astype(BF16)
        before = jnp.dot(earlier, hot.astype(BF16), preferred_element_type=F32)
        slot = before + (base_ref[...] + run_ref[...])
        s0 = jnp.sum(jnp.where(lane == e0, slot, 0.0), axis=1, keepdims=True)
        s1 = jnp.sum(jnp.where(lane == e1, slot, 0.0), axis=1, keepdims=True)
        both = jnp.where(lane == 0, s0, jnp.where(lane == 1, s1, 0.0))
        dest_ref[0] = both.T[0:8, :].astype(jnp.int32)

    run_ref[...] += tile_counts


def _routing_plan(route, tm_plan, tm):
    t = route.shape[0]
    nt = t // tm_plan
    dest, counts = pl.pallas_call(
        functools.partial(_plan_kernel, tm=tm_plan),
        grid=(2, nt),
        in_specs=[pl.BlockSpec((tm_plan, LANES), lambda p, i: (i, 0))],
        out_specs=[pl.BlockSpec((1, 8, tm_plan), lambda p, i: (i * p, 0, 0)),
                   pl.BlockSpec((1, LANES), lambda p, i: (0, 0))],
        out_shape=[jax.ShapeDtypeStruct((nt, 8, tm_plan), jnp.int32), jax.ShapeDtypeStruct((1, LANES), F32)],
        scratch_shapes=[pltpu.VMEM((1, LANES), F32), pltpu.VMEM((1, LANES), F32)],
        compiler_params=_cparams(("arbitrary", "arbitrary")),
        name="moe_plan",
    )(route)
    counts = counts[0, :N_EXPERTS].astype(jnp.int32)
    ends = jnp.cumsum(counts)
    starts = ends - counts
    first_tile = starts // tm
    n_vis = jnp.where(counts > 0, (ends - 1) // tm - first_tile + 1, 0)
    vis_end = jnp.cumsum(n_vis)
    vis_start = vis_end - n_vis
    n_visits = vis_end[-1]
    max_visits = 2 * t // tm + N_EXPERTS - 1
    v = jnp.minimum(jnp.arange(max_visits, dtype=jnp.int32), n_visits - 1)
    visit_expert = jnp.minimum(jnp.sum(v[:, None] >= vis_end[None, :], axis=1), N_EXPERTS - 1).astype(jnp.int32)
    visit_tile = (first_tile[visit_expert] + v - vis_start[visit_expert]).astype(jnp.int32)
    segments = jnp.concatenate([starts, ends]).astype(jnp.int32)
    return dest, visit_tile, visit_expert, segments, n_visits.reshape(1).astype(jnp.int32)


def _pick(n, prefs):
    for p in prefs:
        if n % p == 0:
            return p
    return n


def kernel(x_prompt, x_sample, cache_sb_k, cache_sb_v, cache_diff_k, cache_diff_v, meta_tokens, norm_mix_g, w_in, lambda_q1, lambda_k1, lambda_q2, lambda_k2, subln_g, w_out, norm_ffn_g, w_group, b_group, w_router, b_router, w_gate, w_up, w_down, final_norm_g):
    depth = w_in.shape[0]
    assert depth == 1, "meta-token rows are only dropped after a single layer"
    b, s, d = x_prompt.shape
    bs, ss, _ = x_sample.shape
    past = cache_sb_k.shape[2]
    n_meta = meta_tokens.shape[0]
    lam_init = 0.8 - 0.6 * math.exp(-0.3 * 0)

    g_mix = norm_mix_g[0].reshape(1, d)
    w_in_bf = w_in[0].astype(BF16)
    pos_p = n_meta + jnp.arange(s, dtype=jnp.int32)
    pos_s = past + jnp.arange(ss, dtype=jnp.int32)
    pos_m = jnp.arange(n_meta, dtype=jnp.int32)
    xp2 = x_prompt.reshape(b * s, d)
    xs2 = x_sample.reshape(bs * ss, d)

    tm_p = _pick(s, (512, 256, 128))
    (qsb_p, qdf_p, ksb_p, vsb_p, kdf_p, vdf_p,
     ksb_pb, vsb_pb, kdf_pb, vdf_pb) = _inproj(xp2, g_mix, w_in_bf, pos_p, tm_p, feature_major=True)
    tm_s = _pick(bs * ss, (512, 256, 128))
    (qsb_s, qdf_s, ksb_s, vsb_s, kdf_s, vdf_s,
     ksb_sb, vsb_sb, kdf_sb, vdf_sb) = _inproj(xs2, g_mix, w_in_bf, pos_s, tm_s)
    (_, _, ksb_m, vsb_m, kdf_m, vdf_m,
     ksb_mb, vsb_mb, kdf_mb, vdf_mb) = _inproj(meta_tokens, g_mix, w_in_bf, pos_m, n_meta)

    r3 = lambda a, bb, t: a.reshape(bb, t, GROUP_W)
    lam_vecs = [v[0].reshape(1, HEAD_DIM) for v in (lambda_q1, lambda_k1, lambda_q2, lambda_k2)]
    sub_g = subln_g[0].reshape(1, LANES)

    sb_p = _sb_attention(r3(qsb_p, b, s), ksb_mb[None], vsb_mb[None], r3(ksb_pb, b, s), r3(vsb_pb, b, s))
    df_p = _df_attention(r3(qdf_p, b, s), kdf_mb[None], vdf_mb[None], r3(kdf_pb, b, s), r3(vdf_pb, b, s),
                         *lam_vecs, sub_g, lam_init)
    feature_major = lambda c: jnp.transpose(c[0], (0, 2, 3, 1)).reshape(bs, GROUP_W, past)
    sb_s = _sb_attention(r3(qsb_s, bs, ss), feature_major(cache_sb_k), feature_major(cache_sb_v),
                         r3(ksb_sb, bs, ss), r3(vsb_sb, bs, ss), cached=True)
    df_s = _df_attention(r3(qdf_s, bs, ss), feature_major(cache_diff_k),
                         cache_diff_v[0].reshape(bs, past * DF_HEADS, LANES), r3(kdf_sb, bs, ss), r3(vdf_sb, bs, ss),
                         *lam_vecs, sub_g, lam_init, cached=True)

    w_out_bf = w_out[0].astype(BF16)
    wsb, wdf = w_out_bf[:GROUP_W], w_out_bf[GROUP_W:]
    g_ffn = norm_ffn_g[0].reshape(1, d)
    pad_cols = LANES - N_EXPERTS - N_GROUPS
    wr = jnp.concatenate([w_router[0], w_group[0], jnp.zeros((d, pad_cols), F32)], axis=1).astype(BF16)
    br = jnp.concatenate([b_router[0], b_group[0], jnp.zeros((pad_cols,), F32)]).reshape(1, LANES)

    h_p, route_p = _outproj_route(xp2, sb_p.reshape(b * s, GROUP_W), df_p.reshape(b * s, GROUP_W),
                                        wsb, wdf, g_ffn, wr, br, tm_p)
    h_s, route_s = _outproj_route(xs2, sb_s.reshape(bs * ss, GROUP_W), df_s.reshape(bs * ss, GROUP_W),
                                        wsb, wdf, g_ffn, wr, br, tm_s)

    tm_e = 512
    n_p = b * s
    tm_d = _pick(math.gcd(n_p, bs * ss), (512, 256, 128))
    route = jnp.concatenate([route_p, route_s], axis=0)
    dest, visit_tile, visit_expert, segments, n_visits = _routing_plan(route, tm_d, tm_e)
    xsorted = _dispatch(dest, h_p, h_s, tm_d)
    wgu = jnp.concatenate([w_gate[0], w_up[0]], axis=-1).astype(BF16)
    ysorted = _experts(visit_tile, visit_expert, segments, n_visits, xsorted, g_ffn, wgu,
                       w_down[0].astype(BF16), tm_e)
    g_fin = final_norm_g.reshape(1, d)
    y_p = _combine(dest[:n_p // tm_d], h_p, route_p, g_fin, ysorted, tm_d)
    y_s = _combine(dest[n_p // tm_d:], h_s, route_s, g_fin, ysorted, tm_d)

    def with_meta(meta_rows, frames, heads, hd):
        m = jnp.broadcast_to(meta_rows[None], (b, n_meta, GROUP_W))
        return jnp.concatenate([m, frames.reshape(b, s, GROUP_W)], axis=1).reshape(1, b, n_meta + s, heads, hd)

    def with_meta_t(meta_rows, frames_t, heads, hd):
        m = jnp.broadcast_to(meta_rows.T[None], (b, GROUP_W, n_meta))
        full = jnp.concatenate([m, frames_t], axis=2).reshape(b, heads, hd, n_meta + s)
        return jnp.transpose(full, (0, 3, 1, 2))[None]

    shp = lambda a, heads, hd: a.reshape(1, bs, ss, heads, hd)
    return (y_p.reshape(b, s, d), y_s.reshape(bs, ss, d),
            with_meta_t(ksb_m, ksb_p, SB_HEADS, HEAD_DIM), with_meta_t(vsb_m, vsb_p, SB_HEADS, HEAD_DIM),
            with_meta_t(kdf_m, kdf_p, 2 * DF_HEADS, HEAD_DIM), with_meta(vdf_m, vdf_p, DF_HEADS, 2 * HEAD_DIM),
            shp(ksb_s, SB_HEADS, HEAD_DIM), shp(vsb_s, SB_HEADS, HEAD_DIM),
            shp(kdf_s, 2 * DF_HEADS, HEAD_DIM), shp(vdf_s, DF_HEADS, 2 * HEAD_DIM))
```

```python
import functools
import math

import jax
import jax.numpy as jnp
from jax import lax
from jax.experimental import pallas as pl
from jax.experimental.pallas import tpu as pltpu

F32 = jnp.float32
BF16 = jnp.bfloat16

CHUNK = 64
SB_HEADS = 8
DF_HEADS = 4
HEAD_DIM = 64
GROUP_W = 512
ROPE_THETA = 500000.0
ROT_DIM = 16
N_GROUPS = 4
EXPERTS_PER_GROUP = 8
N_EXPERTS = N_GROUPS * EXPERTS_PER_GROUP
EPS = 1e-6
LANES = 128
SB_CUTOFF = -88.0
VMEM_LIMIT = 56 * 1024 * 1024
SOFTMAX_SHIFT_SLACK = 80.0
FOLDED_PREFIX_ROWS = 128


def _cparams(sem):
    return pltpu.CompilerParams(dimension_semantics=sem, vmem_limit_bytes=VMEM_LIMIT)


def _inproj_kernel(x_ref, g_ref, w_ref, cos_ref, sin_ref,
                   qsb_ref, qdf_ref, ksb_ref, vsb_ref, kdf_ref, vdf_ref,
                   ksb_bf_ref, vsb_bf_ref, kdf_bf_ref, vdf_bf_ref, *, feature_major):
    x = x_ref[...]
    ms = jnp.mean(x * x, axis=-1, keepdims=True)
    xn = (x * lax.rsqrt(ms + EPS) * g_ref[...]).astype(BF16)
    proj = jnp.dot(xn, w_ref[...], preferred_element_type=F32)
    cos = cos_ref[...]
    sin = sin_ref[...]
    lane = lax.broadcasted_iota(jnp.int32, cos.shape, 1)
    first_half = (lane % ROT_DIM) < (ROT_DIM // 2)

    def rope(a):
        cols = []
        for j in range(GROUP_W // LANES):
            aj = a[:, j * LANES:(j + 1) * LANES]
            partner = jnp.where(first_half,
                                pltpu.roll(aj, LANES - ROT_DIM // 2, 1),
                                pltpu.roll(aj, ROT_DIM // 2, 1))
            cols.append(aj * cos + partner * sin)
        return jnp.concatenate(cols, axis=1)

    scale = HEAD_DIM ** -0.5
    qsb_ref[...] = (proj[:, 0:GROUP_W] * scale).astype(BF16)
    qdf_ref[...] = (rope(proj[:, 3 * GROUP_W:4 * GROUP_W]) * scale).astype(BF16)
    kdf = rope(proj[:, 4 * GROUP_W:5 * GROUP_W])
    for f32_ref, bf_ref, val in ((ksb_ref, ksb_bf_ref, proj[:, GROUP_W:2 * GROUP_W]),
                                 (vsb_ref, vsb_bf_ref, proj[:, 2 * GROUP_W:3 * GROUP_W]),
                                 (kdf_ref, kdf_bf_ref, kdf),
                                 (vdf_ref, vdf_bf_ref, proj[:, 5 * GROUP_W:6 * GROUP_W])):
        if feature_major and f32_ref is not vdf_ref:
            f32_ref[0] = val.T
        else:
            f32_ref[...] = val
        bf_ref[...] = val.astype(BF16)


def _rope_tables(pos):
    half = ROT_DIM // 2
    inv = ROPE_THETA ** (-jnp.arange(half, dtype=F32) * 2.0 / ROT_DIM)
    ang = pos.astype(F32)[:, None] * inv[None, :]
    cos, sin = jnp.cos(ang), jnp.sin(ang)
    n = pos.shape[0]
    pad = HEAD_DIM - ROT_DIM
    cos_h = jnp.concatenate([cos, cos, jnp.ones((n, pad), F32)], axis=1)
    sin_h = jnp.concatenate([-sin, sin, jnp.zeros((n, pad), F32)], axis=1)
    reps = LANES // HEAD_DIM
    return jnp.tile(cos_h, (1, reps)), jnp.tile(sin_h, (1, reps))


def _inproj(x2d, g, w_bf, pos, tm, feature_major=False):
    n, d = x2d.shape
    cos_t, sin_t = _rope_tables(pos)
    period = pos.shape[0]
    if period < tm:
        cos_t = jnp.tile(cos_t, (tm // period, 1))
        sin_t = jnp.tile(sin_t, (tm // period, 1))
    ntab = cos_t.shape[0] // tm
    row = lambda i: (i, 0)
    fixed = lambda i: (0, 0)
    tab = lambda i: (i % ntab, 0)
    out_spec = pl.BlockSpec((tm, GROUP_W), row)
    f32_spec, f32_shape = out_spec, jax.ShapeDtypeStruct((n, GROUP_W), F32)
    if feature_major:
        f32_spec = pl.BlockSpec((1, GROUP_W, tm), lambda i: (i // ntab, 0, i % ntab))
        f32_shape = jax.ShapeDtypeStruct((n // period, GROUP_W, period), F32)
    return pl.pallas_call(
        functools.partial(_inproj_kernel, feature_major=feature_major),
        grid=(n // tm,),
        in_specs=[pl.BlockSpec((tm, d), row), pl.BlockSpec((1, d), fixed),
                  pl.BlockSpec(w_bf.shape, fixed),
                  pl.BlockSpec((tm, LANES), tab), pl.BlockSpec((tm, LANES), tab)],
        out_specs=[out_spec] * 2 + [f32_spec] * 3 + [out_spec] * 5,
        out_shape=[jax.ShapeDtypeStruct((n, GROUP_W), BF16)] * 2
                  + [f32_shape] * 3 + [jax.ShapeDtypeStruct((n, GROUP_W), F32)]
                  + [jax.ShapeDtypeStruct((n, GROUP_W), BF16)] * 4,
        compiler_params=_cparams(("parallel",)),
        name="inproj",
    )(x2d, g, w_bf, cos_t, sin_t)


def _sb_kernel(q_ref, kp_ref, vp_ref, ks_ref, vs_ref, o_ref, acc_ref, car_ref, *cache_scratch,
               tq, n_pre, bq, bp, ways, cached):
    lane = lax.broadcasted_iota(jnp.int32, (bq, LANES), 1)
    lo = lane < HEAD_DIM
    contract_last = (((1,), (1,)), ((), ()))
    wn = 2 * bq if tq >= 2 * bq else bq
    step = pl.program_id(0)

    def neg_suffix_matrix(n):
        j = lax.broadcasted_iota(jnp.int32, (n, n + LANES), 0)
        s = lax.broadcasted_iota(jnp.int32, (n, n + LANES), 1)
        return jnp.where((j > s) | (s >= n), -1.0, 0.0).astype(BF16)

    suffixes = {n: neg_suffix_matrix(n) for n in {wn, bq, bp}}

    def row_major(k_ref, v_ref, k0, n):
        def load(hp):
            cols = slice(hp * LANES, (hp + 1) * LANES)
            return k_ref[0, pl.ds(k0, n), cols].astype(BF16), v_ref[0, pl.ds(k0, n), cols].astype(BF16)
        return load

    def sweep(load, feature_major, q0, koff, n, causal, first, w):
        suffix = suffixes[n]
        n_pairs = SB_HEADS // 2
        if causal:
            r = lax.broadcasted_iota(jnp.int32, (2 * bq, n), 0)
            c = lax.broadcasted_iota(jnp.int32, (2 * bq, n), 1)
            vis = (c + koff) < jnp.where(r >= bq, r - bq, r)
        log_sig, masked_sp, values = [], [], []
        for hp in range(n_pairs):
            cols = slice(hp * LANES, (hp + 1) * LANES)
            k2, v2 = load(hp)
            q2 = q_ref[0, pl.ds(q0, bq), cols]
            zero = jnp.zeros_like(q2)
            qs = jnp.concatenate([jnp.where(lo, q2, zero), jnp.where(lo, zero, q2)], axis=0)
            if feature_major:
                s = jnp.dot(qs, k2, preferred_element_type=F32)
            else:
                s = lax.dot_general(qs, k2, contract_last, preferred_element_type=F32)
            sp = jnp.maximum(s, 0.0) + jnp.log(1.0 + jnp.exp(-jnp.abs(s)))
            log_sig.append(s - sp)
            masked_sp.append((jnp.where(vis, sp, 0.0) if causal else sp).astype(BF16))
            values.append(v2)
        cr_all = jnp.dot(jnp.concatenate(masked_sp, axis=0), suffix, preferred_element_type=F32)
        worst = None
        for hp in range(n_pairs):
            cols = slice(hp * LANES, (hp + 1) * LANES)
            cr = cr_all[2 * bq * hp:2 * bq * (hp + 1)]
            between = cr[:, :n]
            car = cr[:, n:]
            if not first:
                old = car_ref[w, hp]
                between = between + (jnp.tile(old, (1, n // LANES)) if n % LANES == 0 else old[:, :n])
                car = car + old
            p = jnp.exp(log_sig[hp] + between)
            if causal:
                p = jnp.where(vis, p, 0.0)
            if feature_major:
                pv = lax.dot_general(p.astype(BF16), values[hp], contract_last, preferred_element_type=F32)
            else:
                pv = jnp.dot(p.astype(BF16), values[hp], preferred_element_type=F32)
            car_ref[w, hp] = car
            worst = car if worst is None else jnp.maximum(worst, car)
            both = jnp.where(lo, pv[:bq], pv[bq:])
            if first:
                acc_ref[w, :, cols] = both
            else:
                acc_ref[w, :, cols] += both
        return jnp.max(worst) > SB_CUTOFF

    if cached:
        kwin_ref, vwin_ref, kblk_ref, vblk_ref, win_sem, blk_sem = cache_scratch
        pw = min(2 * bp, n_pre)

        def window_copies(bi, slot):
            return [pltpu.make_async_copy(src.at[bi, :, pl.ds(n_pre - pw, pw)], dst.at[slot], win_sem.at[a, slot])
                    for a, (src, dst) in enumerate(((kp_ref, kwin_ref), (vp_ref, vwin_ref)))]

        def block_copies(k0):
            return [pltpu.make_async_copy(src.at[step, :, pl.ds(k0, bp)], dst, blk_sem.at[a])
                    for a, (src, dst) in enumerate(((kp_ref, kblk_ref), (vp_ref, vblk_ref)))]

        slot = step % 2

        @pl.when(step == 0)
        def _():
            for cp in window_copies(step, slot):
                cp.start()

        @pl.when(step + 1 < pl.num_programs(0))
        def _():
            for cp in window_copies(step + 1, 1 - slot):
                cp.start()

        for cp in window_copies(step, slot):
            cp.wait()

    def cond(st):
        return (st[0] >= 0) & st[1]

    def first_sweep(qi, w):
        q0 = pl.multiple_of(qi * bq, bq)
        first_blk = jnp.maximum(qi + 1 - wn // bq, 0)
        k0 = pl.multiple_of(first_blk * bq, bq)
        return q0, first_blk, sweep(row_major(ks_ref, vs_ref, k0, wn), False, q0, k0 - q0, wn, True, True, w)

    def finish(w, q0, first_blk, go):
        def self_body(st):
            kj = pl.multiple_of(st[0] * bq, bq)
            return st[0] - 1, sweep(row_major(ks_ref, vs_ref, kj, bq), False, q0, 0, bq, False, False, w)

        _, go = lax.while_loop(cond, self_body, (first_blk - 1, go))

        if not cached:
            def pre_body(st):
                kj = pl.multiple_of(st[0] * bp, bp)
                return st[0] - 1, sweep(row_major(kp_ref, vp_ref, kj, bp), False, q0, 0, bp, False, False, w)

            lax.while_loop(cond, pre_body, (jnp.int32(n_pre // bp - 1), go))
        else:
            def window_block(off):
                def load(hp):
                    rows = slice(hp * LANES, (hp + 1) * LANES)
                    return (kwin_ref[slot, rows, off:off + bp].astype(BF16),
                            vwin_ref[slot, rows, off:off + bp].astype(BF16))
                return load

            for off in range(pw - bp, -1, -bp):
                go = lax.cond(go, lambda off=off: sweep(window_block(off), True, q0, 0, bp, False, False, w),
                              lambda: jnp.zeros((), jnp.bool_))

            def older_body(st):
                for cp in block_copies(pl.multiple_of(st[0] * bp, bp)):
                    cp.start()
                for cp in block_copies(pl.multiple_of(st[0] * bp, bp)):
                    cp.wait()

                def load(hp):
                    rows = slice(hp * LANES, (hp + 1) * LANES)
                    return kblk_ref[rows, :].astype(BF16), vblk_ref[rows, :].astype(BF16)

                return st[0] - 1, sweep(load, True, q0, 0, bp, False, False, w)

            lax.while_loop(cond, older_body, (jnp.int32((n_pre - pw) // bp - 1), go))
        o_ref[0, pl.ds(q0, bq), :] = acc_ref[w].astype(o_ref.dtype)

    def q_group(g, carry):
        started = [first_sweep(g * ways + w, w) for w in range(ways)]
        for w in range(ways):
            finish(w, *started[w])
        return carry

    lax.fori_loop(0, tq // bq // ways, q_group, 0)


def _sb_attention(q, k_pre, v_pre, k_self, v_self, cached=False):
    b, tq, w = q.shape
    n_pre = k_pre.shape[2] if cached else k_pre.shape[1]
    bq = min(128, tq)
    bp = min(128, n_pre)
    ways = 2 if (tq // bq) % 2 == 0 else 1
    bat = lambda i: (i, 0, 0)
    scratch = [pltpu.VMEM((ways, bq, w), F32), pltpu.VMEM((ways, SB_HEADS // 2, 2 * bq, LANES), F32)]
    if cached:
        pre_spec = pl.BlockSpec(memory_space=pl.ANY)
        pw = min(2 * bp, n_pre)
        scratch += [pltpu.VMEM((2, w, pw), F32), pltpu.VMEM((2, w, pw), F32),
                    pltpu.VMEM((w, bp), F32), pltpu.VMEM((w, bp), F32),
                    pltpu.SemaphoreType.DMA((2, 2)), pltpu.SemaphoreType.DMA((2,))]
    else:
        pre_map = (lambda i: (i, 0, 0)) if k_pre.shape[0] == b else (lambda i: (0, 0, 0))
        pre_spec = pl.BlockSpec((1, n_pre, w), pre_map)
    kern = functools.partial(_sb_kernel, tq=tq, n_pre=n_pre, bq=bq, bp=bp, ways=ways, cached=cached)
    return pl.pallas_call(
        kern,
        grid=(b,),
        in_specs=[pl.BlockSpec((1, tq, w), bat), pre_spec, pre_spec,
                  pl.BlockSpec((1, tq, w), bat), pl.BlockSpec((1, tq, w), bat)],
        out_specs=pl.BlockSpec((1, tq, w), bat),
        out_shape=jax.ShapeDtypeStruct((b, tq, w), BF16),
        scratch_shapes=scratch,
        compiler_params=_cparams(("arbitrary",) if cached else ("parallel",)),
        name="sb_attention",
    )(q, k_pre, v_pre, k_self, v_self)


def _df_kernel(q_ref, kp_ref, vp_ref, ks_ref, vs_ref, lq1_ref, lk1_ref, lq2_ref, lk2_ref, g_ref,
               o_ref, vxp_ref, vxs_ref, qm_ref, mx_ref, acc_ref, *fold_scratch,
               tq, n_pre, bq, bp, lam_init, cached, fold):
    lane = lax.broadcasted_iota(jnp.int32, (bq, LANES), 1)
    lo = lane < HEAD_DIM
    contract_last = (((1,), (1,)), ((), ()))
    lam = (jnp.exp(jnp.sum(lq1_ref[...] * lk1_ref[...], axis=-1, keepdims=True))
           - jnp.exp(jnp.sum(lq2_ref[...] * lk2_ref[...], axis=-1, keepdims=True)) + lam_init)

    def extend_values(v_ref, vx_ref, head_rows):
        n = vx_ref.shape[0]
        ones = jnp.ones((n, LANES), BF16)
        for h in range(DF_HEADS):
            if head_rows:
                v_h = v_ref[0, pl.ds(h, n, stride=DF_HEADS), :]
            else:
                v_h = v_ref[0, :, h * LANES:(h + 1) * LANES]
            vx_ref[:, 2 * h * LANES:(2 * h + 1) * LANES] = v_h.astype(BF16)
            vx_ref[:, (2 * h + 1) * LANES:(2 * h + 2) * LANES] = ones

    extend_values(vp_ref, vxp_ref, cached)
    extend_values(vs_ref, vxs_ref, False)
    if fold:
        kpad_ref, vxpad_ref = fold_scratch
        kpad_ref[...] = jnp.zeros_like(kpad_ref)
        vxpad_ref[...] = jnp.zeros_like(vxpad_ref)
        kpad_ref[0:n_pre, :] = kp_ref[0].astype(BF16)
        vxpad_ref[0:n_pre, :] = vxp_ref[...]

    sel_row = lax.broadcasted_iota(jnp.int32, (LANES, 2 * LANES), 0) // HEAD_DIM
    sel_col = lax.broadcasted_iota(jnp.int32, (LANES, 2 * LANES), 1) // LANES
    half_selector = jnp.where(sel_row == sel_col, 1.0, 0.0).astype(BF16)

    def max_half_norm2(x):
        rep = jnp.dot((x * x).astype(BF16), half_selector, preferred_element_type=F32)
        col_max = jnp.max(rep, axis=0, keepdims=True)
        return [jnp.max(col_max[:, m * LANES:(m + 1) * LANES], axis=1, keepdims=True) for m in range(2)]

    kmax2 = []
    for h in range(DF_HEADS):
        halves = max_half_norm2(ks_ref[0, :, h * LANES:(h + 1) * LANES].astype(BF16).astype(F32))
        if cached:
            kp = kp_ref[0, h * LANES:(h + 1) * LANES, :].astype(BF16).astype(F32)
            sq = kp * kp
            pre = [jnp.max(jnp.sum(sq[m * HEAD_DIM:(m + 1) * HEAD_DIM], axis=0, keepdims=True), axis=1, keepdims=True)
                   for m in range(2)]
        else:
            pre = max_half_norm2(kp_ref[0, :, h * LANES:(h + 1) * LANES].astype(BF16).astype(F32))
        kmax2 += [jnp.maximum(halves[m], pre[m]) for m in range(2)]

    shift = []
    for h in range(DF_HEADS):
        qmax2 = max_half_norm2(q_ref[0, :, h * LANES:(h + 1) * LANES].astype(F32))
        shift += [jnp.sqrt(qmax2[m] * kmax2[2 * h + m]) * 1.004 + 1e-6 for m in range(2)]
    worst_shift = shift[0]
    for u in shift[1:]:
        worst_shift = jnp.maximum(worst_shift, u)
    bound_is_tight = jnp.max(worst_shift) * 2.0 < SOFTMAX_SHIFT_SLACK

    def scores(h, k_ref, k0, n, chunk_mask):
        if chunk_mask and fold:
            k2 = jnp.concatenate([k_ref[0, pl.ds(k0, n), h * LANES:(h + 1) * LANES].astype(BF16),
                                  kpad_ref[:, h * LANES:(h + 1) * LANES]], axis=0)
            s = lax.dot_general(qm_ref[h], k2, contract_last, preferred_element_type=F32)
            r = lax.broadcasted_iota(jnp.int32, s.shape, 0)
            r = jnp.where(r >= bq, r - bq, r)
            col = lax.broadcasted_iota(jnp.int32, s.shape, 1)
            vis = ((col < n) & ((col // CHUNK) <= (r // CHUNK))) | ((col >= n) & (col < n + n_pre))
            return jnp.where(vis, s, -jnp.inf)
        if cached and k_ref is kp_ref:
            k2 = k_ref[0, h * LANES:(h + 1) * LANES, pl.ds(k0, n)].astype(BF16)
            s = jnp.dot(qm_ref[h], k2, preferred_element_type=F32)
        else:
            k2 = k_ref[0, pl.ds(k0, n), h * LANES:(h + 1) * LANES].astype(BF16)
            s = lax.dot_general(qm_ref[h], k2, contract_last, preferred_element_type=F32)
        if chunk_mask:
            r = lax.broadcasted_iota(jnp.int32, (2 * bq, n), 0)
            r = jnp.where(r >= bq, r - bq, r)
            col = lax.broadcasted_iota(jnp.int32, (2 * bq, n), 1)
            s = jnp.where((col // CHUNK) <= (r // CHUNK), s, -jnp.inf)
        return s

    def max_tile(k_ref, k0, n, chunk_mask):
        for h in range(DF_HEADS):
            s = scores(h, k_ref, k0, n, chunk_mask)
            if s.shape[1] % LANES == 0:
                m = s[:, :LANES]
                for j in range(1, s.shape[1] // LANES):
                    m = jnp.maximum(m, s[:, j * LANES:(j + 1) * LANES])
            else:
                m = jnp.broadcast_to(jnp.max(s, axis=1, keepdims=True), (2 * bq, LANES))
            mx_ref[h] = jnp.maximum(mx_ref[h], m)

    def value_tile(k_ref, vx_ref, k0, n, chunk_mask):
        for h in range(DF_HEADS):
            s = scores(h, k_ref, k0, n, chunk_mask)
            row_max = mx_ref[h]
            width = s.shape[1]
            if width % LANES == 0:
                p = jnp.concatenate([jnp.exp(s[:, j * LANES:(j + 1) * LANES] - row_max)
                                     for j in range(width // LANES)], axis=1)
            else:
                p = jnp.exp(s - row_max[:, :width])
            vx = vx_ref[pl.ds(k0, n), 2 * h * LANES:(2 * h + 2) * LANES]
            if chunk_mask and fold:
                vx = jnp.concatenate([vx, vxpad_ref[:, 2 * h * LANES:(2 * h + 2) * LANES]], axis=0)
            acc_ref[h] += jnp.dot(p.astype(BF16), vx, preferred_element_type=F32)

    def q_block(qi, carry):
        q0 = pl.multiple_of(qi * bq, bq)
        for h in range(DF_HEADS):
            q2 = q_ref[0, pl.ds(q0, bq), h * LANES:(h + 1) * LANES]
            zero = jnp.zeros_like(q2)
            qm_ref[h] = jnp.concatenate([jnp.where(lo, q2, zero), jnp.where(lo, zero, q2)], axis=0)
        acc_ref[...] = jnp.zeros_like(acc_ref)

        for h in range(DF_HEADS):
            for m in range(2):
                mx_ref[h, m * bq:(m + 1) * bq, :] = jnp.broadcast_to(shift[2 * h + m], (bq, LANES))

        def sweep(pre_fn, self_fn):
            def pre_body(j, c):
                pre_fn(pl.multiple_of(j * bp, bp))
                return c

            def self_body(j, c):
                self_fn(pl.multiple_of(j * bq, bq), False)
                return c

            if not fold:
                lax.fori_loop(0, n_pre // bp, pre_body, 0)
            lax.fori_loop(0, qi, self_body, 0)
            self_fn(q0, True)

        @pl.when(jnp.logical_not(bound_is_tight))
        def _():
            mx_ref[...] = jnp.full_like(mx_ref, -jnp.inf)
            sweep(lambda k0: max_tile(kp_ref, k0, bp, False),
                  lambda k0, msk: max_tile(ks_ref, k0, bq, msk))
            for h in range(DF_HEADS):
                mx_ref[h] = jnp.broadcast_to(jnp.max(mx_ref[h], axis=1, keepdims=True), (2 * bq, LANES))

        sweep(lambda k0: value_tile(kp_ref, vxp_ref, k0, bp, False),
              lambda k0, msk: value_tile(ks_ref, vxs_ref, k0, bq, msk))

        for h in range(DF_HEADS):
            a0 = acc_ref[h, :bq, :]
            a1 = acc_ref[h, bq:, :]
            o = a0[:, :LANES] / a0[:, LANES:] - lam * (a1[:, :LANES] / a1[:, LANES:])
            y = o * lax.rsqrt(jnp.mean(o * o, axis=-1, keepdims=True) + EPS) * g_ref[...]
            o_ref[0, pl.ds(q0, bq), h * LANES:(h + 1) * LANES] = (y * (1.0 - lam_init)).astype(o_ref.dtype)
        return carry

    lax.fori_loop(0, tq // bq, q_block, 0)


def _df_attention(q, k_pre, v_pre, k_self, v_self, lq1, lk1, lq2, lk2, subln_g, lam_init, cached=False):
    b, tq, w = q.shape
    n_pre = k_pre.shape[2] if cached else k_pre.shape[1]
    bq = min(256, tq)
    bp = min(256, n_pre)
    pre_map = (lambda i: (i, 0, 0)) if k_pre.shape[0] == b else (lambda i: (0, 0, 0))
    bat = lambda i: (i, 0, 0)
    fixed = lambda i: (0, 0)
    fold = (not cached) and n_pre <= FOLDED_PREFIX_ROWS and bq % LANES == 0
    kern = functools.partial(_df_kernel, tq=tq, n_pre=n_pre, bq=bq, bp=bp, lam_init=lam_init, cached=cached,
                             fold=fold)
    fold_scratch = [pltpu.VMEM((FOLDED_PREFIX_ROWS, w), BF16), pltpu.VMEM((FOLDED_PREFIX_ROWS, 2 * w), BF16)]
    kp_spec = pl.BlockSpec((1,) + k_pre.shape[1:], pre_map)
    vp_spec = pl.BlockSpec((1,) + v_pre.shape[1:], pre_map)
    vec = pl.BlockSpec((1, HEAD_DIM), fixed)
    return pl.pallas_call(
        kern,
        grid=(b,),
        in_specs=[pl.BlockSpec((1, tq, w), bat),
                  kp_spec, vp_spec,
                  pl.BlockSpec((1, tq, w), bat), pl.BlockSpec((1, tq, w), bat),
                  vec, vec, vec, vec, pl.BlockSpec((1, LANES), fixed)],
        out_specs=pl.BlockSpec((1, tq, w), bat),
        out_shape=jax.ShapeDtypeStruct((b, tq, w), BF16),
        scratch_shapes=[pltpu.VMEM((n_pre, 2 * w), BF16), pltpu.VMEM((tq, 2 * w), BF16),
                        pltpu.VMEM((DF_HEADS, 2 * bq, LANES), BF16), pltpu.VMEM((DF_HEADS, 2 * bq, LANES), F32),
                        pltpu.VMEM((DF_HEADS, 2 * bq, 2 * LANES), F32)] + (fold_scratch if fold else []),
        compiler_params=_cparams(("parallel",)),
        name="df_attention",
    )(q, k_pre, v_pre, k_self, v_self, lq1, lk1, lq2, lk2, subln_g)


def _outproj_kernel(x_ref, sb_ref, df_ref, wsb_ref, wdf_ref, g_ref, wr_ref, br_ref,
                    h_ref, route_ref, cnt_ref):
    h = (x_ref[...]
         + jnp.dot(sb_ref[...], wsb_ref[...], preferred_element_type=F32)
         + jnp.dot(df_ref[...], wdf_ref[...], preferred_element_type=F32))
    h_ref[...] = h
    xn = h * lax.rsqrt(jnp.mean(h * h, axis=-1, keepdims=True) + EPS) * g_ref[...]
    logits = jnp.dot(xn.astype(BF16), wr_ref[...], preferred_element_type=F32) + br_ref[...]
    lane = lax.broadcasted_iota(jnp.int32, logits.shape, 1)
    big = jnp.int32(LANES)
    neg = -jnp.inf

    def first_argmax(v):
        mx = jnp.max(v, axis=1, keepdims=True)
        idx = jnp.min(jnp.where(v == mx, lane, big), axis=1, keepdims=True)
        return mx, idx

    gl = jnp.where((lane >= N_EXPERTS) & (lane < N_EXPERTS + N_GROUPS), logits, neg)
    gmax, gidx = first_argmax(gl)
    g_w = 1.0 / jnp.sum(jnp.exp(gl - gmax), axis=1, keepdims=True)
    grp = gidx - N_EXPERTS
    el = jnp.where((lane < N_EXPERTS) & (lane // EXPERTS_PER_GROUP == grp), logits, neg)
    m1, i1 = first_argmax(el)
    m2, i2 = first_argmax(jnp.where(lane == i1, neg, el))
    e21 = jnp.exp(m2 - m1)
    t1 = 1.0 / (1.0 + e21)
    t2 = e21 / (1.0 + e21)
    route = jnp.where(lane == 0, i1.astype(F32),
            jnp.where(lane == 1, i2.astype(F32),
            jnp.where(lane == 2, g_w * t1,
            jnp.where(lane == 3, g_w * t2, 0.0))))
    route_ref[...] = route
    chosen = jnp.where((lane == i1) | (lane == i2), 1.0, 0.0)
    cnt_ref[0] = jnp.sum(chosen, axis=0, keepdims=True)


def _outproj_route(x2d, sb, df, wsb, wdf, g, wr, br, tm):
    n, d = x2d.shape
    row = lambda i: (i, 0)
    fixed = lambda i: (0, 0)
    return pl.pallas_call(
        _outproj_kernel,
        grid=(n // tm,),
        in_specs=[pl.BlockSpec((tm, d), row), pl.BlockSpec((tm, GROUP_W), row), pl.BlockSpec((tm, GROUP_W), row),
                  pl.BlockSpec(wsb.shape, fixed), pl.BlockSpec(wdf.shape, fixed), pl.BlockSpec((1, d), fixed),
                  pl.BlockSpec(wr.shape, fixed), pl.BlockSpec((1, LANES), fixed)],
        out_specs=[pl.BlockSpec((tm, d), row), pl.BlockSpec((tm, LANES), row),
                   pl.BlockSpec((1, 1, LANES), lambda i: (i, 0, 0))],
        out_shape=[jax.ShapeDtypeStruct((n, d), F32), jax.ShapeDtypeStruct((n, LANES), F32),
                   jax.ShapeDtypeStruct((n // tm, 1, LANES), F32)],
        compiler_params=_cparams(("parallel",)),
        name="outproj_route",
    )(x2d, sb, df, wsb, wdf, g, wr, br)


def _row_copy(src, s, dst, d, sem):
    return pltpu.make_async_copy(src.at[pl.ds(s, 1)], dst.at[pl.ds(d, 1)], sem)


def _dispatch_kernel(d0_ref, d1_ref, hp_ref, hs_ref, xs_hbm, sem, *, tm, n_prompt_tiles):
    def issue_from(src_ref):
        def issue(r, c):
            _row_copy(src_ref, r, xs_hbm, d0_ref[0, 0, r], sem).start(priority=0)
            _row_copy(src_ref, r, xs_hbm, d1_ref[0, 0, r], sem).start(priority=1)
            return c

        lax.fori_loop(0, tm, issue, 0, unroll=8)

    is_prompt = pl.program_id(0) < n_prompt_tiles
    pl.when(is_prompt)(lambda: issue_from(hp_ref))
    pl.when(jnp.logical_not(is_prompt))(lambda: issue_from(hs_ref))
    for _ in range(2):
        pltpu.make_async_copy(hp_ref, xs_hbm.at[pl.ds(0, tm)], sem).wait()


def _dispatch(dest0, dest1, h_prompt, h_sample, tm):
    n_p, d = h_prompt.shape
    n = n_p + h_sample.shape[0]
    npt = n_p // tm
    idx = lambda i: (i, 0, 0)
    smem = functools.partial(pl.BlockSpec, memory_space=pltpu.SMEM)
    return pl.pallas_call(
        functools.partial(_dispatch_kernel, tm=tm, n_prompt_tiles=npt),
        grid=(n // tm,),
        in_specs=[smem((1, 1, tm), idx), smem((1, 1, tm), idx),
                  pl.BlockSpec((tm, d), lambda i: (jnp.minimum(i, npt - 1), 0)),
                  pl.BlockSpec((tm, d), lambda i: (jnp.maximum(i - npt, 0), 0))],
        out_specs=pl.BlockSpec(memory_space=pl.ANY),
        out_shape=jax.ShapeDtypeStruct((2 * n, d), F32),
        scratch_shapes=[pltpu.SemaphoreType.DMA(())],
        compiler_params=_cparams(("arbitrary",)),
        name="moe_dispatch",
    )(dest0, dest1, h_prompt, h_sample)


def _expert_kernel(vt_ref, ve_ref, seg_ref, nv_ref, h_ref, g_ref, wgu_ref, wd_ref, y_ref, *, de, tm):
    v = pl.program_id(0)

    @pl.when(v < nv_ref[0])
    def _():
        e = ve_ref[v]
        t = vt_ref[v]
        h_in = h_ref[...]
        x = (h_in * lax.rsqrt(jnp.mean(h_in * h_in, axis=-1, keepdims=True) + EPS) * g_ref[...]).astype(BF16)
        gu = jnp.dot(x, wgu_ref[0], preferred_element_type=F32)
        g = gu[:, :de]
        h = (g * (1.0 / (1.0 + jnp.exp(-g)))) * gu[:, de:]
        y = jnp.dot(h.astype(BF16), wd_ref[0], preferred_element_type=F32)
        row = t * tm + lax.broadcasted_iota(jnp.int32, (tm, 1), 0)
        mine = (row >= seg_ref[e]) & (row < seg_ref[N_EXPERTS + e])
        first_visit = (v == 0) | (vt_ref[jnp.maximum(v - 1, 0)] != t)

        @pl.when(first_visit)
        def _():
            y_ref[...] = jnp.where(mine, y, 0.0)

        @pl.when(jnp.logical_not(first_visit))
        def _():
            y_ref[...] = jnp.where(mine, y, y_ref[...])


def _experts(visit_tile, visit_expert, segments, n_visits, xs, g, wgu, wd, tm):
    p, d = xs.shape
    de = wd.shape[1]
    grid_spec = pltpu.PrefetchScalarGridSpec(
        num_scalar_prefetch=4,
        grid=(visit_tile.shape[0],),
        in_specs=[pl.BlockSpec((tm, d), lambda v, vt, ve, sg, nv: (vt[v], 0)),
                  pl.BlockSpec((1, d), lambda v, vt, ve, sg, nv: (0, 0)),
                  pl.BlockSpec((1, d, 2 * de), lambda v, vt, ve, sg, nv: (ve[v], 0, 0)),
                  pl.BlockSpec((1, de, d), lambda v, vt, ve, sg, nv: (ve[v], 0, 0))],
        out_specs=pl.BlockSpec((tm, d), lambda v, vt, ve, sg, nv: (vt[v], 0)),
    )
    return pl.pallas_call(
        functools.partial(_expert_kernel, de=de, tm=tm),
        grid_spec=grid_spec,
        out_shape=jax.ShapeDtypeStruct((p, d), F32),
        compiler_params=_cparams(("arbitrary",)),
        name="moe_experts",
    )(visit_tile, visit_expert, segments, n_visits, xs, g, wgu, wd)


def _combine_kernel(d0_ref, d1_ref, n0_ref, n1_ref, h_ref, gate_ref, g_ref, ys_hbm, o_ref,
                    y0_ref, y1_ref, sem, *, tm):
    i = pl.program_id(0)
    slot = i % 2

    def gather(a_ref, b_ref, to):
        def issue(r, c):
            _row_copy(ys_hbm, a_ref[0, 0, r], y0_ref.at[to], r, sem.at[to]).start(priority=0)
            _row_copy(ys_hbm, b_ref[0, 0, r], y1_ref.at[to], r, sem.at[to]).start(priority=1)
            return c

        lax.fori_loop(0, tm, issue, 0, unroll=8)

    pl.when(i == 0)(lambda: gather(d0_ref, d1_ref, slot))
    pl.when(i + 1 < pl.num_programs(0))(lambda: gather(n0_ref, n1_ref, 1 - slot))
    for y_ref in (y0_ref, y1_ref):
        pltpu.make_async_copy(ys_hbm.at[pl.ds(0, tm)], y_ref.at[slot], sem.at[slot]).wait()
    gate = gate_ref[...]
    h = h_ref[...] + (y0_ref[slot] * gate[:, 2:3] + y1_ref[slot] * gate[:, 3:4])
    o_ref[...] = h * lax.rsqrt(jnp.mean(h * h, axis=-1, keepdims=True) + EPS) * g_ref[...]


def _combine(dest0, dest1, h2d, route, g, ys, tm):
    n, d = h2d.shape
    nt = n // tm
    idx = lambda i: (i, 0, 0)
    nxt = lambda i: (jnp.minimum(i + 1, nt - 1), 0, 0)
    row = lambda i: (i, 0)
    smem = functools.partial(pl.BlockSpec, memory_space=pltpu.SMEM)
    blk = (1, 1, tm)
    return pl.pallas_call(
        functools.partial(_combine_kernel, tm=tm),
        grid=(nt,),
        in_specs=[smem(blk, idx), smem(blk, idx), smem(blk, nxt), smem(blk, nxt),
                  pl.BlockSpec((tm, d), row), pl.BlockSpec((tm, LANES), row),
                  pl.BlockSpec((1, d), lambda i: (0, 0)), pl.BlockSpec(memory_space=pl.ANY)],
        out_specs=pl.BlockSpec((tm, d), row),
        out_shape=jax.ShapeDtypeStruct((n, d), F32),
        scratch_shapes=[pltpu.VMEM((2, tm, d), F32), pltpu.VMEM((2, tm, d), F32), pltpu.SemaphoreType.DMA((2,))],
        compiler_params=_cparams(("arbitrary",)),
        name="moe_combine",
    )(dest0, dest1, dest0, dest1, h2d, route, g, ys)


def _plan_kernel(route_ref, base_ref, dest0_ref, dest1_ref, earlier_ref, *, tm):
    @pl.when(pl.program_id(0) == 0)
    def _():
        rr = lax.broadcasted_iota(jnp.int32, (tm, tm), 0)
        cc = lax.broadcasted_iota(jnp.int32, (tm, tm), 1)
        earlier_ref[...] = jnp.where(cc < rr, 1.0, 0.0).astype(BF16)

    lane = lax.broadcasted_iota(jnp.int32, (tm, LANES), 1)
    r = route_ref[...]
    e0 = r[:, 0:1].astype(jnp.int32)
    e1 = r[:, 1:2].astype(jnp.int32)
    hot = jnp.where((lane == e0) | (lane == e1), 1.0, 0.0).astype(BF16)
    slot = jnp.dot(earlier_ref[...], hot, preferred_element_type=F32) + base_ref[0]
    s0 = jnp.sum(jnp.where(lane == e0, slot, 0.0), axis=1, keepdims=True)
    s1 = jnp.sum(jnp.where(lane == e1, slot, 0.0), axis=1, keepdims=True)
    both = jnp.where(lane == 0, s0, jnp.where(lane == 1, s1, 0.0)).T
    dest0_ref[0] = both[0:1, :].astype(jnp.int32)
    dest1_ref[0] = both[1:2, :].astype(jnp.int32)


def _routing_plan(route, tile_counts, tm_plan, tm):
    t = route.shape[0]
    nt = t // tm_plan
    totals = jnp.sum(tile_counts, axis=0)
    expert_start = jnp.cumsum(totals) - totals
    base = (expert_start[None, :] + jnp.cumsum(tile_counts, axis=0) - tile_counts).reshape(nt, 1, LANES)
    tile3 = lambda i: (i, 0, 0)
    dest0, dest1 = pl.pallas_call(
        functools.partial(_plan_kernel, tm=tm_plan),
        grid=(nt,),
        in_specs=[pl.BlockSpec((tm_plan, LANES), lambda i: (i, 0)), pl.BlockSpec((1, 1, LANES), tile3)],
        out_specs=[pl.BlockSpec((1, 1, tm_plan), tile3)] * 2,
        out_shape=[jax.ShapeDtypeStruct((nt, 1, tm_plan), jnp.int32)] * 2,
        scratch_shapes=[pltpu.VMEM((tm_plan, tm_plan), BF16)],
        compiler_params=_cparams(("arbitrary",)),
        name="moe_plan",
    )(route, base)
    counts = totals[:N_EXPERTS].astype(jnp.int32)
    ends = jnp.cumsum(counts)
    starts = ends - counts
    first_tile = starts // tm
    n_vis = jnp.where(counts > 0, (ends - 1) // tm - first_tile + 1, 0)
    vis_end = jnp.cumsum(n_vis)
    vis_start = vis_end - n_vis
    n_visits = vis_end[-1]
    max_visits = 2 * t // tm + N_EXPERTS - 1
    v = jnp.minimum(jnp.arange(max_visits, dtype=jnp.int32), n_visits - 1)
    visit_expert = jnp.minimum(jnp.sum(v[:, None] >= vis_end[None, :], axis=1), N_EXPERTS - 1).astype(jnp.int32)
    visit_tile = (first_tile[visit_expert] + v - vis_start[visit_expert]).astype(jnp.int32)
    segments = jnp.concatenate([starts, ends]).astype(jnp.int32)
    return dest0, dest1, visit_tile, visit_expert, segments, n_visits.reshape(1).astype(jnp.int32)


def _pick(n, prefs):
    for p in prefs:
        if n % p == 0:
            return p
    return n


def kernel(x_prompt, x_sample, cache_sb_k, cache_sb_v, cache_diff_k, cache_diff_v, meta_tokens, norm_mix_g, w_in, lambda_q1, lambda_k1, lambda_q2, lambda_k2, subln_g, w_out, norm_ffn_g, w_group, b_group, w_router, b_router, w_gate, w_up, w_down, final_norm_g):
    depth = w_in.shape[0]
    assert depth == 1, "meta-token rows are only dropped after a single layer"
    b, s, d = x_prompt.shape
    bs, ss, _ = x_sample.shape
    past = cache_sb_k.shape[2]
    n_meta = meta_tokens.shape[0]
    lam_init = 0.8 - 0.6 * math.exp(-0.3 * 0)

    g_mix = norm_mix_g[0].reshape(1, d)
    w_in_bf = w_in[0].astype(BF16)
    pos_p = n_meta + jnp.arange(s, dtype=jnp.int32)
    pos_s = past + jnp.arange(ss, dtype=jnp.int32)
    pos_m = jnp.arange(n_meta, dtype=jnp.int32)
    xp2 = x_prompt.reshape(b * s, d)
    xs2 = x_sample.reshape(bs * ss, d)

    tm_p = _pick(s, (512, 256, 128))
    (qsb_p, qdf_p, ksb_p, vsb_p, kdf_p, vdf_p,
     ksb_pb, vsb_pb, kdf_pb, vdf_pb) = _inproj(xp2, g_mix, w_in_bf, pos_p, tm_p, feature_major=True)
    tm_s = _pick(bs * ss, (512, 256, 128))
    (qsb_s, qdf_s, ksb_s, vsb_s, kdf_s, vdf_s,
     ksb_sb, vsb_sb, kdf_sb, vdf_sb) = _inproj(xs2, g_mix, w_in_bf, pos_s, tm_s)
    (_, _, ksb_m, vsb_m, kdf_m, vdf_m,
     ksb_mb, vsb_mb, kdf_mb, vdf_mb) = _inproj(meta_tokens, g_mix, w_in_bf, pos_m, n_meta)

    r3 = lambda a, bb, t: a.reshape(bb, t, GROUP_W)
    lam_vecs = [v[0].reshape(1, HEAD_DIM) for v in (lambda_q1, lambda_k1, lambda_q2, lambda_k2)]
    sub_g = subln_g[0].reshape(1, LANES)

    sb_p = _sb_attention(r3(qsb_p, b, s), ksb_mb[None], vsb_mb[None], r3(ksb_pb, b, s), r3(vsb_pb, b, s))
    df_p = _df_attention(r3(qdf_p, b, s), kdf_mb[None], vdf_mb[None], r3(kdf_pb, b, s), r3(vdf_pb, b, s),
                         *lam_vecs, sub_g, lam_init)
    feature_major = lambda c: jnp.transpose(c[0], (0, 2, 3, 1)).reshape(bs, GROUP_W, past)
    sb_s = _sb_attention(r3(qsb_s, bs, ss), feature_major(cache_sb_k), feature_major(cache_sb_v),
                         r3(ksb_sb, bs, ss), r3(vsb_sb, bs, ss), cached=True)
    df_s = _df_attention(r3(qdf_s, bs, ss), feature_major(cache_diff_k),
                         cache_diff_v[0].reshape(bs, past * DF_HEADS, LANES), r3(kdf_sb, bs, ss), r3(vdf_sb, bs, ss),
                         *lam_vecs, sub_g, lam_init, cached=True)

    w_out_bf = w_out[0].astype(BF16)
    wsb, wdf = w_out_bf[:GROUP_W], w_out_bf[GROUP_W:]
    g_ffn = norm_ffn_g[0].reshape(1, d)
    pad_cols = LANES - N_EXPERTS - N_GROUPS
    wr = jnp.concatenate([w_router[0], w_group[0], jnp.zeros((d, pad_cols), F32)], axis=1).astype(BF16)
    br = jnp.concatenate([b_router[0], b_group[0], jnp.zeros((pad_cols,), F32)]).reshape(1, LANES)

    h_p, route_p, cnt_p = _outproj_route(xp2, sb_p.reshape(b * s, GROUP_W), df_p.reshape(b * s, GROUP_W),
                                        wsb, wdf, g_ffn, wr, br, tm_p)
    h_s, route_s, cnt_s = _outproj_route(xs2, sb_s.reshape(bs * ss, GROUP_W), df_s.reshape(bs * ss, GROUP_W),
                                        wsb, wdf, g_ffn, wr, br, tm_s)

    tm_e = 512
    n_p = b * s
    tm_d = _pick(math.gcd(n_p, bs * ss), (512, 256, 128))
    route = jnp.concatenate([route_p, route_s], axis=0)
    assert tm_p == tm_d and tm_s == tm_d, "per-tile expert counts must line up with the plan tiles"
    tile_counts = jnp.concatenate([cnt_p[:, 0, :], cnt_s[:, 0, :]], axis=0)
    dest0, dest1, visit_tile, visit_expert, segments, n_visits = _routing_plan(route, tile_counts, tm_d, tm_e)
    xsorted = _dispatch(dest0, dest1, h_p, h_s, tm_d)
    wgu = jnp.concatenate([w_gate[0], w_up[0]], axis=-1).astype(BF16)
    ysorted = _experts(visit_tile, visit_expert, segments, n_visits, xsorted, g_ffn, wgu,
                       w_down[0].astype(BF16), tm_e)
    g_fin = final_norm_g.reshape(1, d)
    npt = n_p // tm_d
    y_p = _combine(dest0[:npt], dest1[:npt], h_p, route_p, g_fin, ysorted, tm_d)
    y_s = _combine(dest0[npt:], dest1[npt:], h_s, route_s, g_fin, ysorted, tm_d)

    def with_meta(meta_rows, frames, heads, hd):
        m = jnp.broadcast_to(meta_rows[None], (b, n_meta, GROUP_W))
        return jnp.concatenate([m, frames.reshape(b, s, GROUP_W)], axis=1).reshape(1, b, n_meta + s, heads, hd)

    def with_meta_t(meta_rows, frames_t, heads, hd):
        m = jnp.broadcast_to(meta_rows.T[None], (b, GROUP_W, n_meta))
        full = jnp.concatenate([m, frames_t], axis=2).reshape(b, heads, hd, n_meta + s)
        return jnp.transpose(full, (0, 3, 1, 2))[None]

    shp = lambda a, heads, hd: a.reshape(1, bs, ss, heads, hd)
    return (y_p.reshape(b, s, d), y_s.reshape(bs, ss, d),
            with_meta_t(ksb_m, ksb_p, SB_HEADS, HEAD_DIM), with_meta_t(vsb_m, vsb_p, SB_HEADS, HEAD_DIM),
            with_meta_t(kdf_m, kdf_p, 2 * DF_HEADS, HEAD_DIM), with_meta(vdf_m, vdf_p, DF_HEADS, 2 * HEAD_DIM),
            shp(ksb_s, SB_HEADS, HEAD_DIM), shp(vsb_s, SB_HEADS, HEAD_DIM),
            shp(kdf_s, 2 * DF_HEADS, HEAD_DIM), shp(vdf_s, DF_HEADS, 2 * HEAD_DIM))
```

```python
import functools
import math

import jax
import jax.numpy as jnp
from jax import lax
from jax.experimental import pallas as pl
from jax.experimental.pallas import tpu as pltpu

F32 = jnp.float32
BF16 = jnp.bfloat16

CHUNK = 64
SB_HEADS = 8
DF_HEADS = 4
HEAD_DIM = 64
GROUP_W = 512
ROPE_THETA = 500000.0
ROT_DIM = 16
N_GROUPS = 4
EXPERTS_PER_GROUP = 8
N_EXPERTS = N_GROUPS * EXPERTS_PER_GROUP
EPS = 1e-6
LANES = 128
SB_CUTOFF = -88.0
VMEM_LIMIT = 56 * 1024 * 1024
SOFTMAX_SHIFT_SLACK = 80.0
FOLDED_PREFIX_ROWS = 128


def _cparams(sem):
    return pltpu.CompilerParams(dimension_semantics=sem, vmem_limit_bytes=VMEM_LIMIT)


def _inproj_kernel(x_ref, g_ref, w_ref, cos_ref, sin_ref,
                   qsb_ref, qdf_ref, ksb_ref, vsb_ref, kdf_ref, vdf_ref,
                   ksb_bf_ref, vsb_bf_ref, kdf_bf_ref, vdf_bf_ref, *, feature_major):
    x = x_ref[...]
    ms = jnp.mean(x * x, axis=-1, keepdims=True)
    xn = (x * lax.rsqrt(ms + EPS) * g_ref[...]).astype(BF16)
    proj = jnp.dot(xn, w_ref[...], preferred_element_type=F32)
    cos = cos_ref[...]
    sin = sin_ref[...]
    lane = lax.broadcasted_iota(jnp.int32, cos.shape, 1)
    first_half = (lane % ROT_DIM) < (ROT_DIM // 2)

    def rope(a):
        cols = []
        for j in range(GROUP_W // LANES):
            aj = a[:, j * LANES:(j + 1) * LANES]
            partner = jnp.where(first_half,
                                pltpu.roll(aj, LANES - ROT_DIM // 2, 1),
                                pltpu.roll(aj, ROT_DIM // 2, 1))
            cols.append(aj * cos + partner * sin)
        return jnp.concatenate(cols, axis=1)

    scale = HEAD_DIM ** -0.5
    qsb_ref[...] = (proj[:, 0:GROUP_W] * scale).astype(BF16)
    qdf_ref[...] = (rope(proj[:, 3 * GROUP_W:4 * GROUP_W]) * scale).astype(BF16)
    kdf = rope(proj[:, 4 * GROUP_W:5 * GROUP_W])
    for f32_ref, bf_ref, val in ((ksb_ref, ksb_bf_ref, proj[:, GROUP_W:2 * GROUP_W]),
                                 (vsb_ref, vsb_bf_ref, proj[:, 2 * GROUP_W:3 * GROUP_W]),
                                 (kdf_ref, kdf_bf_ref, kdf),
                                 (vdf_ref, vdf_bf_ref, proj[:, 5 * GROUP_W:6 * GROUP_W])):
        if feature_major and f32_ref is not vdf_ref:
            f32_ref[0] = val.T
        else:
            f32_ref[...] = val
        bf_ref[...] = val.astype(BF16)


def _rope_tables(pos):
    half = ROT_DIM // 2
    inv = ROPE_THETA ** (-jnp.arange(half, dtype=F32) * 2.0 / ROT_DIM)
    ang = pos.astype(F32)[:, None] * inv[None, :]
    cos, sin = jnp.cos(ang), jnp.sin(ang)
    n = pos.shape[0]
    pad = HEAD_DIM - ROT_DIM
    cos_h = jnp.concatenate([cos, cos, jnp.ones((n, pad), F32)], axis=1)
    sin_h = jnp.concatenate([-sin, sin, jnp.zeros((n, pad), F32)], axis=1)
    reps = LANES // HEAD_DIM
    return jnp.tile(cos_h, (1, reps)), jnp.tile(sin_h, (1, reps))


def _inproj(x2d, g, w_bf, pos, tm, feature_major=False):
    n, d = x2d.shape
    cos_t, sin_t = _rope_tables(pos)
    period = pos.shape[0]
    if period < tm:
        cos_t = jnp.tile(cos_t, (tm // period, 1))
        sin_t = jnp.tile(sin_t, (tm // period, 1))
    ntab = cos_t.shape[0] // tm
    row = lambda i: (i, 0)
    fixed = lambda i: (0, 0)
    tab = lambda i: (i % ntab, 0)
    out_spec = pl.BlockSpec((tm, GROUP_W), row)
    f32_spec, f32_shape = out_spec, jax.ShapeDtypeStruct((n, GROUP_W), F32)
    if feature_major:
        f32_spec = pl.BlockSpec((1, GROUP_W, tm), lambda i: (i // ntab, 0, i % ntab))
        f32_shape = jax.ShapeDtypeStruct((n // period, GROUP_W, period), F32)
    return pl.pallas_call(
        functools.partial(_inproj_kernel, feature_major=feature_major),
        grid=(n // tm,),
        in_specs=[pl.BlockSpec((tm, d), row), pl.BlockSpec((1, d), fixed),
                  pl.BlockSpec(w_bf.shape, fixed),
                  pl.BlockSpec((tm, LANES), tab), pl.BlockSpec((tm, LANES), tab)],
        out_specs=[out_spec] * 2 + [f32_spec] * 3 + [out_spec] * 5,
        out_shape=[jax.ShapeDtypeStruct((n, GROUP_W), BF16)] * 2
                  + [f32_shape] * 3 + [jax.ShapeDtypeStruct((n, GROUP_W), F32)]
                  + [jax.ShapeDtypeStruct((n, GROUP_W), BF16)] * 4,
        compiler_params=_cparams(("parallel",)),
        name="inproj",
    )(x2d, g, w_bf, cos_t, sin_t)


def _sb_kernel(q_ref, kp_ref, vp_ref, ks_ref, vs_ref, o_ref, acc_ref, car_ref, *cache_scratch,
               tq, n_pre, bq, bp, ways, cached):
    lane = lax.broadcasted_iota(jnp.int32, (bq, LANES), 1)
    lo = lane < HEAD_DIM
    contract_last = (((1,), (1,)), ((), ()))
    wn = 2 * bq if tq >= 2 * bq else bq
    step = pl.program_id(0)

    def neg_suffix_matrix(n):
        j = lax.broadcasted_iota(jnp.int32, (n, n + LANES), 0)
        s = lax.broadcasted_iota(jnp.int32, (n, n + LANES), 1)
        return jnp.where((j > s) | (s >= n), -1.0, 0.0).astype(BF16)

    suffixes = {n: neg_suffix_matrix(n) for n in {wn, bq, bp}}

    def row_major(k_ref, v_ref, k0, n):
        def load(hp):
            cols = slice(hp * LANES, (hp + 1) * LANES)
            return k_ref[0, pl.ds(k0, n), cols].astype(BF16), v_ref[0, pl.ds(k0, n), cols].astype(BF16)
        return load

    def sweep(load, feature_major, q0, koff, n, causal, first, w):
        suffix = suffixes[n]
        n_pairs = SB_HEADS // 2
        if causal:
            r = lax.broadcasted_iota(jnp.int32, (2 * bq, n), 0)
            c = lax.broadcasted_iota(jnp.int32, (2 * bq, n), 1)
            vis = (c + koff) < jnp.where(r >= bq, r - bq, r)
        log_sig, masked_sp, values = [], [], []
        for hp in range(n_pairs):
            cols = slice(hp * LANES, (hp + 1) * LANES)
            k2, v2 = load(hp)
            q2 = q_ref[0, pl.ds(q0, bq), cols]
            zero = jnp.zeros_like(q2)
            qs = jnp.concatenate([jnp.where(lo, q2, zero), jnp.where(lo, zero, q2)], axis=0)
            if feature_major:
                s = jnp.dot(qs, k2, preferred_element_type=F32)
            else:
                s = lax.dot_general(qs, k2, contract_last, preferred_element_type=F32)
            sp = jnp.maximum(s, 0.0) + jnp.log(1.0 + jnp.exp(-jnp.abs(s)))
            log_sig.append(s - sp)
            masked_sp.append((jnp.where(vis, sp, 0.0) if causal else sp).astype(BF16))
            values.append(v2)
        cr_all = jnp.dot(jnp.concatenate(masked_sp, axis=0), suffix, preferred_element_type=F32)
        worst = None
        for hp in range(n_pairs):
            cols = slice(hp * LANES, (hp + 1) * LANES)
            cr = cr_all[2 * bq * hp:2 * bq * (hp + 1)]
            between = cr[:, :n]
            car = cr[:, n:]
            if not first:
                old = car_ref[w, hp]
                between = between + (jnp.tile(old, (1, n // LANES)) if n % LANES == 0 else old[:, :n])
                car = car + old
            p = jnp.exp(log_sig[hp] + between)
            if causal:
                p = jnp.where(vis, p, 0.0)
            if feature_major:
                pv = lax.dot_general(p.astype(BF16), values[hp], contract_last, preferred_element_type=F32)
            else:
                pv = jnp.dot(p.astype(BF16), values[hp], preferred_element_type=F32)
            car_ref[w, hp] = car
            worst = car if worst is None else jnp.maximum(worst, car)
            both = jnp.where(lo, pv[:bq], pv[bq:])
            if first:
                acc_ref[w, :, cols] = both
            else:
                acc_ref[w, :, cols] += both
        return jnp.max(worst) > SB_CUTOFF

    if cached:
        kwin_ref, vwin_ref, kblk_ref, vblk_ref, win_sem, blk_sem = cache_scratch
        pw = min(2 * bp, n_pre)

        def window_copies(bi, slot):
            return [pltpu.make_async_copy(src.at[bi, :, pl.ds(n_pre - pw, pw)], dst.at[slot], win_sem.at[a, slot])
                    for a, (src, dst) in enumerate(((kp_ref, kwin_ref), (vp_ref, vwin_ref)))]

        def block_copies(k0):
            return [pltpu.make_async_copy(src.at[step, :, pl.ds(k0, bp)], dst, blk_sem.at[a])
                    for a, (src, dst) in enumerate(((kp_ref, kblk_ref), (vp_ref, vblk_ref)))]

        slot = step % 2

        @pl.when(step == 0)
        def _():
            for cp in window_copies(step, slot):
                cp.start()

        @pl.when(step + 1 < pl.num_programs(0))
        def _():
            for cp in window_copies(step + 1, 1 - slot):
                cp.start()

        for cp in window_copies(step, slot):
            cp.wait()

    def cond(st):
        return (st[0] >= 0) & st[1]

    def first_sweep(qi, w):
        q0 = pl.multiple_of(qi * bq, bq)
        first_blk = jnp.maximum(qi + 1 - wn // bq, 0)
        k0 = pl.multiple_of(first_blk * bq, bq)
        return q0, first_blk, sweep(row_major(ks_ref, vs_ref, k0, wn), False, q0, k0 - q0, wn, True, True, w)

    def finish(w, q0, first_blk, go):
        def self_body(st):
            kj = pl.multiple_of(st[0] * bq, bq)
            return st[0] - 1, sweep(row_major(ks_ref, vs_ref, kj, bq), False, q0, 0, bq, False, False, w)

        _, go = lax.while_loop(cond, self_body, (first_blk - 1, go))

        if not cached:
            def pre_body(st):
                kj = pl.multiple_of(st[0] * bp, bp)
                return st[0] - 1, sweep(row_major(kp_ref, vp_ref, kj, bp), False, q0, 0, bp, False, False, w)

            lax.while_loop(cond, pre_body, (jnp.int32(n_pre // bp - 1), go))
        else:
            def window_block(off):
                def load(hp):
                    rows = slice(hp * LANES, (hp + 1) * LANES)
                    return (kwin_ref[slot, rows, off:off + bp].astype(BF16),
                            vwin_ref[slot, rows, off:off + bp].astype(BF16))
                return load

            for off in range(pw - bp, -1, -bp):
                go = lax.cond(go, lambda off=off: sweep(window_block(off), True, q0, 0, bp, False, False, w),
                              lambda: jnp.zeros((), jnp.bool_))

            def older_body(st):
                for cp in block_copies(pl.multiple_of(st[0] * bp, bp)):
                    cp.start()
                for cp in block_copies(pl.multiple_of(st[0] * bp, bp)):
                    cp.wait()

                def load(hp):
                    rows = slice(hp * LANES, (hp + 1) * LANES)
                    return kblk_ref[rows, :].astype(BF16), vblk_ref[rows, :].astype(BF16)

                return st[0] - 1, sweep(load, True, q0, 0, bp, False, False, w)

            lax.while_loop(cond, older_body, (jnp.int32((n_pre - pw) // bp - 1), go))
        o_ref[0, pl.ds(q0, bq), :] = acc_ref[w].astype(o_ref.dtype)

    def q_group(g, carry):
        started = [first_sweep(g * ways + w, w) for w in range(ways)]
        for w in range(ways):
            finish(w, *started[w])
        return carry

    lax.fori_loop(0, tq // bq // ways, q_group, 0)


def _sb_attention(q, k_pre, v_pre, k_self, v_self, cached=False):
    b, tq, w = q.shape
    n_pre = k_pre.shape[2] if cached else k_pre.shape[1]
    bq = min(128, tq)
    bp = min(128, n_pre)
    ways = 2 if (tq // bq) % 2 == 0 else 1
    bat = lambda i: (i, 0, 0)
    scratch = [pltpu.VMEM((ways, bq, w), F32), pltpu.VMEM((ways, SB_HEADS // 2, 2 * bq, LANES), F32)]
    if cached:
        pre_spec = pl.BlockSpec(memory_space=pl.ANY)
        pw = min(2 * bp, n_pre)
        scratch += [pltpu.VMEM((2, w, pw), F32), pltpu.VMEM((2, w, pw), F32),
                    pltpu.VMEM((w, bp), F32), pltpu.VMEM((w, bp), F32),
                    pltpu.SemaphoreType.DMA((2, 2)), pltpu.SemaphoreType.DMA((2,))]
    else:
        pre_map = (lambda i: (i, 0, 0)) if k_pre.shape[0] == b else (lambda i: (0, 0, 0))
        pre_spec = pl.BlockSpec((1, n_pre, w), pre_map)
    kern = functools.partial(_sb_kernel, tq=tq, n_pre=n_pre, bq=bq, bp=bp, ways=ways, cached=cached)
    return pl.pallas_call(
        kern,
        grid=(b,),
        in_specs=[pl.BlockSpec((1, tq, w), bat), pre_spec, pre_spec,
                  pl.BlockSpec((1, tq, w), bat), pl.BlockSpec((1, tq, w), bat)],
        out_specs=pl.BlockSpec((1, tq, w), bat),
        out_shape=jax.ShapeDtypeStruct((b, tq, w), BF16),
        scratch_shapes=scratch,
        compiler_params=_cparams(("arbitrary",) if cached else ("parallel",)),
        name="sb_attention",
    )(q, k_pre, v_pre, k_self, v_self)


def _df_kernel(q_ref, kp_ref, vp_ref, ks_ref, vs_ref, lq1_ref, lk1_ref, lq2_ref, lk2_ref, g_ref,
               o_ref, vxp_ref, vxs_ref, qm_ref, mx_ref, acc_ref, *fold_scratch,
               tq, n_pre, bq, bp, lam_init, cached, fold):
    lane = lax.broadcasted_iota(jnp.int32, (bq, LANES), 1)
    lo = lane < HEAD_DIM
    contract_last = (((1,), (1,)), ((), ()))
    lam = (jnp.exp(jnp.sum(lq1_ref[...] * lk1_ref[...], axis=-1, keepdims=True))
           - jnp.exp(jnp.sum(lq2_ref[...] * lk2_ref[...], axis=-1, keepdims=True)) + lam_init)

    def extend_values(v_ref, vx_ref, head_rows):
        n = vx_ref.shape[0]
        ones = jnp.ones((n, LANES), BF16)
        for h in range(DF_HEADS):
            if head_rows:
                v_h = v_ref[0, pl.ds(h, n, stride=DF_HEADS), :]
            else:
                v_h = v_ref[0, :, h * LANES:(h + 1) * LANES]
            vx_ref[:, 2 * h * LANES:(2 * h + 1) * LANES] = v_h.astype(BF16)
            vx_ref[:, (2 * h + 1) * LANES:(2 * h + 2) * LANES] = ones

    extend_values(vp_ref, vxp_ref, cached)
    extend_values(vs_ref, vxs_ref, False)
    if fold:
        kpad_ref, vxpad_ref = fold_scratch
        kpad_ref[...] = jnp.zeros_like(kpad_ref)
        vxpad_ref[...] = jnp.zeros_like(vxpad_ref)
        kpad_ref[0:n_pre, :] = kp_ref[0].astype(BF16)
        vxpad_ref[0:n_pre, :] = vxp_ref[...]

    sel_row = lax.broadcasted_iota(jnp.int32, (LANES, 2 * LANES), 0) // HEAD_DIM
    sel_col = lax.broadcasted_iota(jnp.int32, (LANES, 2 * LANES), 1) // LANES
    half_selector = jnp.where(sel_row == sel_col, 1.0, 0.0).astype(BF16)

    def max_half_norm2(x):
        rep = jnp.dot((x * x).astype(BF16), half_selector, preferred_element_type=F32)
        col_max = jnp.max(rep, axis=0, keepdims=True)
        return [jnp.max(col_max[:, m * LANES:(m + 1) * LANES], axis=1, keepdims=True) for m in range(2)]

    kmax2 = []
    for h in range(DF_HEADS):
        halves = max_half_norm2(ks_ref[0, :, h * LANES:(h + 1) * LANES].astype(BF16).astype(F32))
        if cached:
            kp = kp_ref[0, h * LANES:(h + 1) * LANES, :].astype(BF16).astype(F32)
            sq = kp * kp
            pre = [jnp.max(jnp.sum(sq[m * HEAD_DIM:(m + 1) * HEAD_DIM], axis=0, keepdims=True), axis=1, keepdims=True)
                   for m in range(2)]
        else:
            pre = max_half_norm2(kp_ref[0, :, h * LANES:(h + 1) * LANES].astype(BF16).astype(F32))
        kmax2 += [jnp.maximum(halves[m], pre[m]) for m in range(2)]

    shift = []
    for h in range(DF_HEADS):
        qmax2 = max_half_norm2(q_ref[0, :, h * LANES:(h + 1) * LANES].astype(F32))
        shift += [jnp.sqrt(qmax2[m] * kmax2[2 * h + m]) * 1.004 + 1e-6 for m in range(2)]
    worst_shift = shift[0]
    for u in shift[1:]:
        worst_shift = jnp.maximum(worst_shift, u)
    bound_is_tight = jnp.max(worst_shift) * 2.0 < SOFTMAX_SHIFT_SLACK

    def scores(h, k_ref, k0, n, chunk_mask):
        if chunk_mask and fold:
            k2 = jnp.concatenate([k_ref[0, pl.ds(k0, n), h * LANES:(h + 1) * LANES].astype(BF16),
                                  kpad_ref[:, h * LANES:(h + 1) * LANES]], axis=0)
            s = lax.dot_general(qm_ref[h], k2, contract_last, preferred_element_type=F32)
            r = lax.broadcasted_iota(jnp.int32, s.shape, 0)
            r = jnp.where(r >= bq, r - bq, r)
            col = lax.broadcasted_iota(jnp.int32, s.shape, 1)
            vis = ((col < n) & ((col // CHUNK) <= (r // CHUNK))) | ((col >= n) & (col < n + n_pre))
            return jnp.where(vis, s, -jnp.inf)
        if cached and k_ref is kp_ref:
            k2 = k_ref[0, h * LANES:(h + 1) * LANES, pl.ds(k0, n)].astype(BF16)
            s = jnp.dot(qm_ref[h], k2, preferred_element_type=F32)
        else:
            k2 = k_ref[0, pl.ds(k0, n), h * LANES:(h + 1) * LANES].astype(BF16)
            s = lax.dot_general(qm_ref[h], k2, contract_last, preferred_element_type=F32)
        if chunk_mask:
            r = lax.broadcasted_iota(jnp.int32, (2 * bq, n), 0)
            r = jnp.where(r >= bq, r - bq, r)
            col = lax.broadcasted_iota(jnp.int32, (2 * bq, n), 1)
            s = jnp.where((col // CHUNK) <= (r // CHUNK), s, -jnp.inf)
        return s

    def max_tile(k_ref, k0, n, chunk_mask):
        for h in range(DF_HEADS):
            s = scores(h, k_ref, k0, n, chunk_mask)
            if s.shape[1] % LANES == 0:
                m = s[:, :LANES]
                for j in range(1, s.shape[1] // LANES):
                    m = jnp.maximum(m, s[:, j * LANES:(j + 1) * LANES])
            else:
                m = jnp.broadcast_to(jnp.max(s, axis=1, keepdims=True), (2 * bq, LANES))
            mx_ref[h] = jnp.maximum(mx_ref[h], m)

    def value_tile(k_ref, vx_ref, k0, n, chunk_mask):
        for h in range(DF_HEADS):
            s = scores(h, k_ref, k0, n, chunk_mask)
            row_max = mx_ref[h]
            width = s.shape[1]
            if width % LANES == 0:
                p = jnp.concatenate([jnp.exp(s[:, j * LANES:(j + 1) * LANES] - row_max)
                                     for j in range(width // LANES)], axis=1)
            else:
                p = jnp.exp(s - row_max[:, :width])
            vx = vx_ref[pl.ds(k0, n), 2 * h * LANES:(2 * h + 2) * LANES]
            if chunk_mask and fold:
                vx = jnp.concatenate([vx, vxpad_ref[:, 2 * h * LANES:(2 * h + 2) * LANES]], axis=0)
            acc_ref[h] += jnp.dot(p.astype(BF16), vx, preferred_element_type=F32)

    def q_block(qi, carry):
        q0 = pl.multiple_of(qi * bq, bq)
        for h in range(DF_HEADS):
            q2 = q_ref[0, pl.ds(q0, bq), h * LANES:(h + 1) * LANES]
            zero = jnp.zeros_like(q2)
            qm_ref[h] = jnp.concatenate([jnp.where(lo, q2, zero), jnp.where(lo, zero, q2)], axis=0)
        acc_ref[...] = jnp.zeros_like(acc_ref)

        for h in range(DF_HEADS):
            for m in range(2):
                mx_ref[h, m * bq:(m + 1) * bq, :] = jnp.broadcast_to(shift[2 * h + m], (bq, LANES))

        def sweep(pre_fn, self_fn):
            def pre_body(j, c):
                pre_fn(pl.multiple_of(j * bp, bp))
                return c

            def self_body(j, c):
                self_fn(pl.multiple_of(j * bq, bq), False)
                return c

            if not fold:
                lax.fori_loop(0, n_pre // bp, pre_body, 0)
            lax.fori_loop(0, qi, self_body, 0)
            self_fn(q0, True)

        @pl.when(jnp.logical_not(bound_is_tight))
        def _():
            mx_ref[...] = jnp.full_like(mx_ref, -jnp.inf)
            sweep(lambda k0: max_tile(kp_ref, k0, bp, False),
                  lambda k0, msk: max_tile(ks_ref, k0, bq, msk))
            for h in range(DF_HEADS):
                mx_ref[h] = jnp.broadcast_to(jnp.max(mx_ref[h], axis=1, keepdims=True), (2 * bq, LANES))

        sweep(lambda k0: value_tile(kp_ref, vxp_ref, k0, bp, False),
              lambda k0, msk: value_tile(ks_ref, vxs_ref, k0, bq, msk))

        for h in range(DF_HEADS):
            a0 = acc_ref[h, :bq, :]
            a1 = acc_ref[h, bq:, :]
            o = a0[:, :LANES] / a0[:, LANES:] - lam * (a1[:, :LANES] / a1[:, LANES:])
            y = o * lax.rsqrt(jnp.mean(o * o, axis=-1, keepdims=True) + EPS) * g_ref[...]
            o_ref[0, pl.ds(q0, bq), h * LANES:(h + 1) * LANES] = (y * (1.0 - lam_init)).astype(o_ref.dtype)
        return carry

    lax.fori_loop(0, tq // bq, q_block, 0)


def _df_attention(q, k_pre, v_pre, k_self, v_self, lq1, lk1, lq2, lk2, subln_g, lam_init, cached=False):
    b, tq, w = q.shape
    n_pre = k_pre.shape[2] if cached else k_pre.shape[1]
    bq = min(256, tq)
    bp = min(256, n_pre)
    pre_map = (lambda i: (i, 0, 0)) if k_pre.shape[0] == b else (lambda i: (0, 0, 0))
    bat = lambda i: (i, 0, 0)
    fixed = lambda i: (0, 0)
    fold = (not cached) and n_pre <= FOLDED_PREFIX_ROWS and bq % LANES == 0
    kern = functools.partial(_df_kernel, tq=tq, n_pre=n_pre, bq=bq, bp=bp, lam_init=lam_init, cached=cached,
                             fold=fold)
    fold_scratch = [pltpu.VMEM((FOLDED_PREFIX_ROWS, w), BF16), pltpu.VMEM((FOLDED_PREFIX_ROWS, 2 * w), BF16)]
    kp_spec = pl.BlockSpec((1,) + k_pre.shape[1:], pre_map)
    vp_spec = pl.BlockSpec((1,) + v_pre.shape[1:], pre_map)
    vec = pl.BlockSpec((1, HEAD_DIM), fixed)
    return pl.pallas_call(
        kern,
        grid=(b,),
        in_specs=[pl.BlockSpec((1, tq, w), bat),
                  kp_spec, vp_spec,
                  pl.BlockSpec((1, tq, w), bat), pl.BlockSpec((1, tq, w), bat),
                  vec, vec, vec, vec, pl.BlockSpec((1, LANES), fixed)],
        out_specs=pl.BlockSpec((1, tq, w), bat),
        out_shape=jax.ShapeDtypeStruct((b, tq, w), BF16),
        scratch_shapes=[pltpu.VMEM((n_pre, 2 * w), BF16), pltpu.VMEM((tq, 2 * w), BF16),
                        pltpu.VMEM((DF_HEADS, 2 * bq, LANES), BF16), pltpu.VMEM((DF_HEADS, 2 * bq, LANES), F32),
                        pltpu.VMEM((DF_HEADS, 2 * bq, 2 * LANES), F32)] + (fold_scratch if fold else []),
        compiler_params=_cparams(("parallel",)),
        name="df_attention",
    )(q, k_pre, v_pre, k_self, v_self, lq1, lk1, lq2, lk2, subln_g)


def _outproj_kernel(x_ref, sb_ref, df_ref, wsb_ref, wdf_ref, g_ref, wr_ref, br_ref,
                    h_ref, route_ref, cnt_ref):
    h = (x_ref[...]
         + jnp.dot(sb_ref[...], wsb_ref[...], preferred_element_type=F32)
         + jnp.dot(df_ref[...], wdf_ref[...], preferred_element_type=F32))
    h_ref[...] = h
    xn = h * lax.rsqrt(jnp.mean(h * h, axis=-1, keepdims=True) + EPS) * g_ref[...]
    logits = lax.dot_general(wr_ref[...], xn.astype(BF16), (((1,), (1,)), ((), ())),
                             preferred_element_type=F32) + br_ref[...]
    n_rows = 40
    lg = logits[0:n_rows]
    row = lax.broadcasted_iota(jnp.int32, lg.shape, 0)
    big = jnp.int32(LANES)
    neg = -jnp.inf

    def first_argmax(v):
        mx = jnp.max(v, axis=0, keepdims=True)
        idx = jnp.min(jnp.where(v == mx, row, big), axis=0, keepdims=True)
        return mx, idx

    gl = jnp.where((row >= N_EXPERTS) & (row < N_EXPERTS + N_GROUPS), lg, neg)
    gmax, gidx = first_argmax(gl)
    g_w = 1.0 / jnp.sum(jnp.exp(gl - gmax), axis=0, keepdims=True)
    grp = gidx - N_EXPERTS
    el = jnp.where((row < N_EXPERTS) & (row // EXPERTS_PER_GROUP == grp), lg, neg)
    m1, i1 = first_argmax(el)
    m2, i2 = first_argmax(jnp.where(row == i1, neg, el))
    e21 = jnp.exp(m2 - m1)
    t1 = 1.0 / (1.0 + e21)
    t2 = e21 / (1.0 + e21)
    prow = lax.broadcasted_iota(jnp.int32, logits.shape, 0)
    packed = jnp.where(prow == 0, i1.astype(F32),
             jnp.where(prow == 1, i2.astype(F32),
             jnp.where(prow == 2, g_w * t1,
             jnp.where(prow == 3, g_w * t2, 0.0))))
    route = packed.T
    lane = lax.broadcasted_iota(jnp.int32, route.shape, 1)
    i1 = route[:, 0:1].astype(jnp.int32)
    i2 = route[:, 1:2].astype(jnp.int32)
    route_ref[...] = route
    chosen = jnp.where((lane == i1) | (lane == i2), 1.0, 0.0)
    cnt_ref[0] = jnp.sum(chosen, axis=0, keepdims=True)


def _outproj_route(x2d, sb, df, wsb, wdf, g, wr, br, tm):
    n, d = x2d.shape
    row = lambda i: (i, 0)
    fixed = lambda i: (0, 0)
    return pl.pallas_call(
        _outproj_kernel,
        grid=(n // tm,),
        in_specs=[pl.BlockSpec((tm, d), row), pl.BlockSpec((tm, GROUP_W), row), pl.BlockSpec((tm, GROUP_W), row),
                  pl.BlockSpec(wsb.shape, fixed), pl.BlockSpec(wdf.shape, fixed), pl.BlockSpec((1, d), fixed),
                  pl.BlockSpec(wr.shape, fixed), pl.BlockSpec((LANES, 1), fixed)],
        out_specs=[pl.BlockSpec((tm, d), row), pl.BlockSpec((tm, LANES), row),
                   pl.BlockSpec((1, 1, LANES), lambda i: (i, 0, 0))],
        out_shape=[jax.ShapeDtypeStruct((n, d), F32), jax.ShapeDtypeStruct((n, LANES), F32),
                   jax.ShapeDtypeStruct((n // tm, 1, LANES), F32)],
        compiler_params=_cparams(("parallel",)),
        name="outproj_route",
    )(x2d, sb, df, wsb, wdf, g, wr, br)


def _row_copy(src, s, dst, d, sem):
    return pltpu.make_async_copy(src.at[pl.ds(s, 1)], dst.at[pl.ds(d, 1)], sem)


def _dispatch_kernel(d0_ref, d1_ref, hp_ref, hs_ref, xs_hbm, sem, *, tm, n_prompt_tiles):
    def issue_from(src_ref):
        def issue(r, c):
            _row_copy(src_ref, r, xs_hbm, d0_ref[0, 0, r], sem).start(priority=0)
            _row_copy(src_ref, r, xs_hbm, d1_ref[0, 0, r], sem).start(priority=1)
            return c

        lax.fori_loop(0, tm, issue, 0, unroll=8)

    is_prompt = pl.program_id(0) < n_prompt_tiles
    pl.when(is_prompt)(lambda: issue_from(hp_ref))
    pl.when(jnp.logical_not(is_prompt))(lambda: issue_from(hs_ref))
    for _ in range(2):
        pltpu.make_async_copy(hp_ref, xs_hbm.at[pl.ds(0, tm)], sem).wait()


def _dispatch(dest0, dest1, h_prompt, h_sample, tm):
    n_p, d = h_prompt.shape
    n = n_p + h_sample.shape[0]
    npt = n_p // tm
    idx = lambda i: (i, 0, 0)
    smem = functools.partial(pl.BlockSpec, memory_space=pltpu.SMEM)
    return pl.pallas_call(
        functools.partial(_dispatch_kernel, tm=tm, n_prompt_tiles=npt),
        grid=(n // tm,),
        in_specs=[smem((1, 1, tm), idx), smem((1, 1, tm), idx),
                  pl.BlockSpec((tm, d), lambda i: (jnp.minimum(i, npt - 1), 0)),
                  pl.BlockSpec((tm, d), lambda i: (jnp.maximum(i - npt, 0), 0))],
        out_specs=pl.BlockSpec(memory_space=pl.ANY),
        out_shape=jax.ShapeDtypeStruct((2 * n, d), F32),
        scratch_shapes=[pltpu.SemaphoreType.DMA(())],
        compiler_params=_cparams(("arbitrary",)),
        name="moe_dispatch",
    )(dest0, dest1, h_prompt, h_sample)


def _expert_kernel(vt_ref, ve_ref, seg_ref, nv_ref, h_ref, g_ref, wgu_ref, wd_ref, y_ref, *, de, tm):
    v = pl.program_id(0)

    @pl.when(v < nv_ref[0])
    def _():
        e = ve_ref[v]
        t = vt_ref[v]
        h_in = h_ref[...]
        x = (h_in * lax.rsqrt(jnp.mean(h_in * h_in, axis=-1, keepdims=True) + EPS) * g_ref[...]).astype(BF16)
        gu = jnp.dot(x, wgu_ref[0], preferred_element_type=F32)
        g = gu[:, :de]
        h = (g * (1.0 / (1.0 + jnp.exp(-g)))) * gu[:, de:]
        y = jnp.dot(h.astype(BF16), wd_ref[0], preferred_element_type=F32)
        row = t * tm + lax.broadcasted_iota(jnp.int32, (tm, 1), 0)
        mine = (row >= seg_ref[e]) & (row < seg_ref[N_EXPERTS + e])
        first_visit = (v == 0) | (vt_ref[jnp.maximum(v - 1, 0)] != t)

        @pl.when(first_visit)
        def _():
            y_ref[...] = jnp.where(mine, y, 0.0)

        @pl.when(jnp.logical_not(first_visit))
        def _():
            y_ref[...] = jnp.where(mine, y, y_ref[...])


def _experts(visit_tile, visit_expert, segments, n_visits, xs, g, wgu, wd, tm):
    p, d = xs.shape
    de = wd.shape[1]
    grid_spec = pltpu.PrefetchScalarGridSpec(
        num_scalar_prefetch=4,
        grid=(visit_tile.shape[0],),
        in_specs=[pl.BlockSpec((tm, d), lambda v, vt, ve, sg, nv: (vt[v], 0)),
                  pl.BlockSpec((1, d), lambda v, vt, ve, sg, nv: (0, 0)),
                  pl.BlockSpec((1, d, 2 * de), lambda v, vt, ve, sg, nv: (ve[v], 0, 0)),
                  pl.BlockSpec((1, de, d), lambda v, vt, ve, sg, nv: (ve[v], 0, 0))],
        out_specs=pl.BlockSpec((tm, d), lambda v, vt, ve, sg, nv: (vt[v], 0)),
    )
    return pl.pallas_call(
        functools.partial(_expert_kernel, de=de, tm=tm),
        grid_spec=grid_spec,
        out_shape=jax.ShapeDtypeStruct((p, d), F32),
        compiler_params=_cparams(("arbitrary",)),
        name="moe_experts",
    )(visit_tile, visit_expert, segments, n_visits, xs, g, wgu, wd)


def _combine_kernel(d0_ref, d1_ref, n0_ref, n1_ref, h_ref, gate_ref, g_ref, ys_hbm, o_ref,
                    y0_ref, y1_ref, sem, *, tm):
    i = pl.program_id(0)
    slot = i % 2

    def gather(a_ref, b_ref, to):
        def issue(r, c):
            _row_copy(ys_hbm, a_ref[0, 0, r], y0_ref.at[to], r, sem.at[to]).start(priority=0)
            _row_copy(ys_hbm, b_ref[0, 0, r], y1_ref.at[to], r, sem.at[to]).start(priority=1)
            return c

        lax.fori_loop(0, tm, issue, 0, unroll=8)

    pl.when(i == 0)(lambda: gather(d0_ref, d1_ref, slot))
    pl.when(i + 1 < pl.num_programs(0))(lambda: gather(n0_ref, n1_ref, 1 - slot))
    for y_ref in (y0_ref, y1_ref):
        pltpu.make_async_copy(ys_hbm.at[pl.ds(0, tm)], y_ref.at[slot], sem.at[slot]).wait()
    gate = gate_ref[...]
    h = h_ref[...] + (y0_ref[slot] * gate[:, 2:3] + y1_ref[slot] * gate[:, 3:4])
    o_ref[...] = h * lax.rsqrt(jnp.mean(h * h, axis=-1, keepdims=True) + EPS) * g_ref[...]


def _combine(dest0, dest1, h2d, route, g, ys, tm):
    n, d = h2d.shape
    nt = n // tm
    idx = lambda i: (i, 0, 0)
    nxt = lambda i: (jnp.minimum(i + 1, nt - 1), 0, 0)
    row = lambda i: (i, 0)
    smem = functools.partial(pl.BlockSpec, memory_space=pltpu.SMEM)
    blk = (1, 1, tm)
    return pl.pallas_call(
        functools.partial(_combine_kernel, tm=tm),
        grid=(nt,),
        in_specs=[smem(blk, idx), smem(blk, idx), smem(blk, nxt), smem(blk, nxt),
                  pl.BlockSpec((tm, d), row), pl.BlockSpec((tm, LANES), row),
                  pl.BlockSpec((1, d), lambda i: (0, 0)), pl.BlockSpec(memory_space=pl.ANY)],
        out_specs=pl.BlockSpec((tm, d), row),
        out_shape=jax.ShapeDtypeStruct((n, d), F32),
        scratch_shapes=[pltpu.VMEM((2, tm, d), F32), pltpu.VMEM((2, tm, d), F32), pltpu.SemaphoreType.DMA((2,))],
        compiler_params=_cparams(("arbitrary",)),
        name="moe_combine",
    )(dest0, dest1, dest0, dest1, h2d, route, g, ys)


def _plan_kernel(route_ref, base_ref, dest0_ref, dest1_ref, earlier_ref, *, tm):
    @pl.when(pl.program_id(0) == 0)
    def _():
        rr = lax.broadcasted_iota(jnp.int32, (tm, tm), 0)
        cc = lax.broadcasted_iota(jnp.int32, (tm, tm), 1)
        earlier_ref[...] = jnp.where(cc < rr, 1.0, 0.0).astype(BF16)

    lane = lax.broadcasted_iota(jnp.int32, (tm, LANES), 1)
    r = route_ref[...]
    e0 = r[:, 0:1].astype(jnp.int32)
    e1 = r[:, 1:2].astype(jnp.int32)
    hot = jnp.where((lane == e0) | (lane == e1), 1.0, 0.0).astype(BF16)
    slot = jnp.dot(earlier_ref[...], hot, preferred_element_type=F32) + base_ref[0]
    s0 = jnp.sum(jnp.where(lane == e0, slot, 0.0), axis=1, keepdims=True)
    s1 = jnp.sum(jnp.where(lane == e1, slot, 0.0), axis=1, keepdims=True)
    both = jnp.where(lane == 0, s0, jnp.where(lane == 1, s1, 0.0)).T
    dest0_ref[0] = both[0:1, :].astype(jnp.int32)
    dest1_ref[0] = both[1:2, :].astype(jnp.int32)


def _routing_plan(route, tile_counts, tm_plan, tm):
    t = route.shape[0]
    nt = t // tm_plan
    totals = jnp.sum(tile_counts, axis=0)
    expert_start = jnp.cumsum(totals) - totals
    base = (expert_start[None, :] + jnp.cumsum(tile_counts, axis=0) - tile_counts).reshape(nt, 1, LANES)
    tile3 = lambda i: (i, 0, 0)
    dest0, dest1 = pl.pallas_call(
        functools.partial(_plan_kernel, tm=tm_plan),
        grid=(nt,),
        in_specs=[pl.BlockSpec((tm_plan, LANES), lambda i: (i, 0)), pl.BlockSpec((1, 1, LANES), tile3)],
        out_specs=[pl.BlockSpec((1, 1, tm_plan), tile3)] * 2,
        out_shape=[jax.ShapeDtypeStruct((nt, 1, tm_plan), jnp.int32)] * 2,
        scratch_shapes=[pltpu.VMEM((tm_plan, tm_plan), BF16)],
        compiler_params=_cparams(("arbitrary",)),
        name="moe_plan",
    )(route, base)
    counts = totals[:N_EXPERTS].astype(jnp.int32)
    ends = jnp.cumsum(counts)
    starts = ends - counts
    first_tile = starts // tm
    n_vis = jnp.where(counts > 0, (ends - 1) // tm - first_tile + 1, 0)
    vis_end = jnp.cumsum(n_vis)
    vis_start = vis_end - n_vis
    n_visits = vis_end[-1]
    max_visits = 2 * t // tm + N_EXPERTS - 1
    v = jnp.minimum(jnp.arange(max_visits, dtype=jnp.int32), n_visits - 1)
    visit_expert = jnp.minimum(jnp.sum(v[:, None] >= vis_end[None, :], axis=1), N_EXPERTS - 1).astype(jnp.int32)
    visit_tile = (first_tile[visit_expert] + v - vis_start[visit_expert]).astype(jnp.int32)
    segments = jnp.concatenate([starts, ends]).astype(jnp.int32)
    return dest0, dest1, visit_tile, visit_expert, segments, n_visits.reshape(1).astype(jnp.int32)


def _pick(n, prefs):
    for p in prefs:
        if n % p == 0:
            return p
    return n


def kernel(x_prompt, x_sample, cache_sb_k, cache_sb_v, cache_diff_k, cache_diff_v, meta_tokens, norm_mix_g, w_in, lambda_q1, lambda_k1, lambda_q2, lambda_k2, subln_g, w_out, norm_ffn_g, w_group, b_group, w_router, b_router, w_gate, w_up, w_down, final_norm_g):
    depth = w_in.shape[0]
    assert depth == 1, "meta-token rows are only dropped after a single layer"
    b, s, d = x_prompt.shape
    bs, ss, _ = x_sample.shape
    past = cache_sb_k.shape[2]
    n_meta = meta_tokens.shape[0]
    lam_init = 0.8 - 0.6 * math.exp(-0.3 * 0)

    g_mix = norm_mix_g[0].reshape(1, d)
    w_in_bf = w_in[0].astype(BF16)
    pos_p = n_meta + jnp.arange(s, dtype=jnp.int32)
    pos_s = past + jnp.arange(ss, dtype=jnp.int32)
    pos_m = jnp.arange(n_meta, dtype=jnp.int32)
    xp2 = x_prompt.reshape(b * s, d)
    xs2 = x_sample.reshape(bs * ss, d)

    tm_p = _pick(s, (512, 256, 128))
    (qsb_p, qdf_p, ksb_p, vsb_p, kdf_p, vdf_p,
     ksb_pb, vsb_pb, kdf_pb, vdf_pb) = _inproj(xp2, g_mix, w_in_bf, pos_p, tm_p, feature_major=True)
    tm_s = _pick(bs * ss, (512, 256, 128))
    (qsb_s, qdf_s, ksb_s, vsb_s, kdf_s, vdf_s,
     ksb_sb, vsb_sb, kdf_sb, vdf_sb) = _inproj(xs2, g_mix, w_in_bf, pos_s, tm_s)
    (_, _, ksb_m, vsb_m, kdf_m, vdf_m,
     ksb_mb, vsb_mb, kdf_mb, vdf_mb) = _inproj(meta_tokens, g_mix, w_in_bf, pos_m, n_meta)

    r3 = lambda a, bb, t: a.reshape(bb, t, GROUP_W)
    lam_vecs = [v[0].reshape(1, HEAD_DIM) for v in (lambda_q1, lambda_k1, lambda_q2, lambda_k2)]
    sub_g = subln_g[0].reshape(1, LANES)

    sb_p = _sb_attention(r3(qsb_p, b, s), ksb_mb[None], vsb_mb[None], r3(ksb_pb, b, s), r3(vsb_pb, b, s))
    df_p = _df_attention(r3(qdf_p, b, s), kdf_mb[None], vdf_mb[None], r3(kdf_pb, b, s), r3(vdf_pb, b, s),
                         *lam_vecs, sub_g, lam_init)
    feature_major = lambda c: jnp.transpose(c[0], (0, 2, 3, 1)).reshape(bs, GROUP_W, past)
    sb_s = _sb_attention(r3(qsb_s, bs, ss), feature_major(cache_sb_k), feature_major(cache_sb_v),
                         r3(ksb_sb, bs, ss), r3(vsb_sb, bs, ss), cached=True)
    df_s = _df_attention(r3(qdf_s, bs, ss), feature_major(cache_diff_k),
                         cache_diff_v[0].reshape(bs, past * DF_HEADS, LANES), r3(kdf_sb, bs, ss), r3(vdf_sb, bs, ss),
                         *lam_vecs, sub_g, lam_init, cached=True)

    w_out_bf = w_out[0].astype(BF16)
    wsb, wdf = w_out_bf[:GROUP_W], w_out_bf[GROUP_W:]
    g_ffn = norm_ffn_g[0].reshape(1, d)
    pad_cols = LANES - N_EXPERTS - N_GROUPS
    wr = jnp.concatenate([w_router[0], w_group[0], jnp.zeros((d, pad_cols), F32)], axis=1).astype(BF16).T
    br = jnp.concatenate([b_router[0], b_group[0], jnp.zeros((pad_cols,), F32)]).reshape(LANES, 1)

    h_p, route_p, cnt_p = _outproj_route(xp2, sb_p.reshape(b * s, GROUP_W), df_p.reshape(b * s, GROUP_W),
                                        wsb, wdf, g_ffn, wr, br, tm_p)
    h_s, route_s, cnt_s = _outproj_route(xs2, sb_s.reshape(bs * ss, GROUP_W), df_s.reshape(bs * ss, GROUP_W),
                                        wsb, wdf, g_ffn, wr, br, tm_s)

    tm_e = 512
    n_p = b * s
    tm_d = _pick(math.gcd(n_p, bs * ss), (512, 256, 128))
    route = jnp.concatenate([route_p, route_s], axis=0)
    assert tm_p == tm_d and tm_s == tm_d, "per-tile expert counts must line up with the plan tiles"
    tile_counts = jnp.concatenate([cnt_p[:, 0, :], cnt_s[:, 0, :]], axis=0)
    dest0, dest1, visit_tile, visit_expert, segments, n_visits = _routing_plan(route, tile_counts, tm_d, tm_e)
    xsorted = _dispatch(dest0, dest1, h_p, h_s, tm_d)
    wgu = jnp.concatenate([w_gate[0], w_up[0]], axis=-1).astype(BF16)
    ysorted = _experts(visit_tile, visit_expert, segments, n_visits, xsorted, g_ffn, wgu,
                       w_down[0].astype(BF16), tm_e)
    g_fin = final_norm_g.reshape(1, d)
    npt = n_p // tm_d
    y_p = _combine(dest0[:npt], dest1[:npt], h_p, route_p, g_fin, ysorted, tm_d)
    y_s = _combine(dest0[npt:], dest1[npt:], h_s, route_s, g_fin, ysorted, tm_d)

    def with_meta(meta_rows, frames, heads, hd):
        m = jnp.broadcast_to(meta_rows[None], (b, n_meta, GROUP_W))
        return jnp.concatenate([m, frames.reshape(b, s, GROUP_W)], axis=1).reshape(1, b, n_meta + s, heads, hd)

    def with_meta_t(meta_rows, frames_t, heads, hd):
        m = jnp.broadcast_to(meta_rows.T[None], (b, GROUP_W, n_meta))
        full = jnp.concatenate([m, frames_t], axis=2).reshape(b, heads, hd, n_meta + s)
        return jnp.transpose(full, (0, 3, 1, 2))[None]

    shp = lambda a, heads, hd: a.reshape(1, bs, ss, heads, hd)
    return (y_p.reshape(b, s, d), y_s.reshape(bs, ss, d),
            with_meta_t(ksb_m, ksb_p, SB_HEADS, HEAD_DIM), with_meta_t(vsb_m, vsb_p, SB_HEADS, HEAD_DIM),
            with_meta_t(kdf_m, kdf_p, 2 * DF_HEADS, HEAD_DIM), with_meta(vdf_m, vdf_p, DF_HEADS, 2 * HEAD_DIM),
            shp(ksb_s, SB_HEADS, HEAD_DIM), shp(vsb_s, SB_HEADS, HEAD_DIM),
            shp(kdf_s, 2 * DF_HEADS, HEAD_DIM), shp(vdf_s, DF_HEADS, 2 * HEAD_DIM))
```
